```python
import jax, jax.numpy as jnp
from jax import lax
import numpy as np

D_MODEL = 2048
BATCH = 2
SEQ = 16384
DEPTH = 1

PLE_DIM = 256
D_FF = 5632
RET_HEADS = 8
RET_DK = 128
RET_DV = 256
RET_CHUNK = 128
RET_THETA = 10000.0
ATT_HEADS = 16
ATT_KV_HEADS = 4
ATT_DH = 64
WINDOW = 128
ATT_BLOCK = 128
ROPE_THETA = 500000.0
ROPE_DIMS = ATT_DH // 4
LN_EPS = 1e-5
GN_EPS = 1e-5

RET_QK = RET_HEADS * RET_DK
RET_V = RET_HEADS * RET_DV
ATT_Q = ATT_HEADS * ATT_DH
ATT_KV = ATT_KV_HEADS * ATT_DH
IN_SIZES = (RET_QK, RET_QK, RET_V, RET_V, ATT_Q, ATT_KV, ATT_KV, D_MODEL, D_MODEL)
D_IN = sum(IN_SIZES)
SPLIT_POINTS = tuple(int(v) for v in np.cumsum(IN_SIZES)[:-1])

kernel_name = "hybrid_retention_swa_sink_macaron_deepnorm"


def layer_norm(x, g, b):
    xf = x.astype(jnp.float32)
    mu = jnp.mean(xf, axis=-1, keepdims=True)
    var = jnp.mean(jnp.square(xf - mu), axis=-1, keepdims=True)
    y = (xf - mu) * lax.rsqrt(var + LN_EPS)
    return (y * g.astype(jnp.float32) + b.astype(jnp.float32)).astype(x.dtype)


def swiglu(x, w_gu, w_down):
    g, u = jnp.split(x @ w_gu, 2, axis=-1)
    return (jax.nn.silu(g) * u) @ w_down


def rotary(x, pos, n_rot, theta):
    half = n_rot // 2
    inv = 1.0 / (theta ** (jnp.arange(half, dtype=jnp.float32) / half))
    ang = pos.astype(jnp.float32)[..., None] * inv
    cos = jnp.cos(ang)[:, :, None, :]
    sin = jnp.sin(ang)[:, :, None, :]
    xr = x[..., :n_rot].astype(jnp.float32)
    x1, x2 = xr[..., :half], xr[..., half:]
    rot = jnp.concatenate([x1 * cos - x2 * sin, x2 * cos + x1 * sin], axis=-1).astype(x.dtype)
    return jnp.concatenate([rot, x[..., n_rot:]], axis=-1)


def retention_chunkwise(q, k, v):
    B, S, H, dk = q.shape
    dv = v.shape[-1]
    C = RET_CHUNK
    N = S // C
    f32 = jnp.float32
    log_g = jnp.log1p(-jnp.exp2(-5.0 - jnp.arange(H, dtype=f32)))
    idx = jnp.arange(C, dtype=f32)
    q_decay = jnp.exp((idx[:, None] + 1.0) * log_g)[:, :, None]
    k_decay = jnp.exp((C - 1.0 - idx)[:, None] * log_g)[:, :, None]
    diff = idx[:, None] - idx[None, :]
    inner_decay = jnp.where(diff[None] >= 0,
                            jnp.exp(jnp.maximum(diff, 0.0)[None] * log_g[:, None, None]),
                            0.0)
    chunk_decay = jnp.exp(C * log_g)[None, :, None, None]

    def to_chunks(t):
        return jnp.swapaxes(t.astype(f32).reshape(B, N, C, H, t.shape[-1]), 0, 1)

    qc, kc, vc = to_chunks(q), to_chunks(k), to_chunks(v)

    def step(state, inp):
        qn, kn, vn = inp
        scores = jnp.einsum('bihd,bjhd->bhij', qn, kn) * inner_decay
        inner = jnp.einsum('bhij,bjhe->bihe', scores, vn)
        cross = jnp.einsum('bihd,bhde->bihe', qn * q_decay, state)
        new_state = state * chunk_decay + jnp.einsum('bjhd,bjhe->bhde', kn * k_decay, vn)
        return new_state, inner + cross

    state0 = jnp.zeros((B, H, dk, dv), f32)
    _, out = lax.scan(step, state0, (qc, kc, vc))
    return jnp.swapaxes(out, 0, 1).reshape(B, S, H, dv)


def head_group_norm(o, g):
    B, S = o.shape[0], o.shape[1]
    mu = jnp.mean(o, axis=-1, keepdims=True)
    var = jnp.mean(jnp.square(o - mu), axis=-1, keepdims=True)
    y = (o - mu) * lax.rsqrt(var + GN_EPS)
    return y.reshape(B, S, -1) * g.astype(jnp.float32)


def sliding_window_attention_sinks(q, k, v, sinks):
    B, S, Hq, d = q.shape
    Hkv = k.shape[2]
    G = Hq // Hkv
    C = ATT_BLOCK
    N = S // C
    qb = q.reshape(B, N, C, Hkv, G, d)

    def band(t):
        tb = t.reshape(B, N, C, Hkv, d)
        prev = jnp.pad(tb, ((0, 0), (1, 0), (0, 0), (0, 0), (0, 0)))[:, :N]
        return jnp.concatenate([prev, tb], axis=2)

    kb, vb = band(k), band(v)
    s = jnp.einsum('bnihgd,bnjhd->bnhgij', qb, kb).astype(jnp.float32) * (d ** -0.5)
    qi = jnp.arange(C)[:, None] + C
    kj = jnp.arange(2 * C)[None, :]
    rel = qi - kj
    blk = jnp.arange(N)[:, None, None]
    valid = (rel >= 0) & (rel < WINDOW) & ((blk > 0) | (kj >= C))[...]
    s = jnp.where(valid[None, :, None, None], s, -jnp.inf)
    sink = sinks.astype(jnp.float32).reshape(Hkv, G)[None, None, :, :, None, None]
    m = jnp.maximum(jnp.max(s, axis=-1, keepdims=True), sink)
    e = jnp.exp(s - m)
    denom = jnp.sum(e, axis=-1, keepdims=True) + jnp.exp(sink - m)
    pr = (e / denom).astype(v.dtype)
    o = jnp.einsum('bnhgij,bnjhd->bnihgd', pr, vb)
    return o.reshape(B, S, Hq * d)


def setup_inputs(seed: int = 0) -> dict:
    key = jax.random.key(seed)
    ks = jax.random.split(key, 20)
    f32 = jnp.float32
    beta = (8.0 * DEPTH) ** -0.25

    def nrm(k, shape, scale):
        return jax.random.normal(k, shape, f32) * scale

    x = jax.random.normal(ks[0], (BATCH, SEQ, D_MODEL), f32)
    p = jax.random.normal(ks[1], (DEPTH, BATCH, SEQ, PLE_DIM), f32)
    offset = jax.random.randint(ks[2], (BATCH, 1), 0, 4096, dtype=jnp.int32)
    positions = offset + jnp.arange(SEQ, dtype=jnp.int32)[None, :]
    ln_g = 1.0 + nrm(ks[3], (DEPTH, 4, D_MODEL), 0.02)
    ln_b = nrm(ks[4], (DEPTH, 4, D_MODEL), 0.02)
    w_ffn1_gu = nrm(ks[5], (DEPTH, D_MODEL, 2 * D_FF), D_MODEL ** -0.5)
    w_ffn1_down = nrm(ks[6], (DEPTH, D_FF, D_MODEL), beta * D_FF ** -0.5)
    w_in = nrm(ks[7], (DEPTH, D_MODEL, D_IN), D_MODEL ** -0.5)
    ret_gn_g = 1.0 + nrm(ks[8], (DEPTH, RET_V), 0.02)
    att_sinks = nrm(ks[9], (DEPTH, ATT_HEADS), 1.0)
    w_ret_out = nrm(ks[10], (DEPTH, RET_V, D_MODEL), RET_V ** -0.5)
    w_att_out = nrm(ks[11], (DEPTH, ATT_Q, D_MODEL), ATT_Q ** -0.5)
    w_mix_out = nrm(ks[12], (DEPTH, D_MODEL, D_MODEL), beta * D_MODEL ** -0.5)
    w_ffn2_gu = nrm(ks[13], (DEPTH, D_MODEL, 2 * D_FF), D_MODEL ** -0.5)
    w_ffn2_down = nrm(ks[14], (DEPTH, D_FF, D_MODEL), beta * D_FF ** -0.5)
    w_ple_gate = nrm(ks[15], (DEPTH, D_MODEL, D_MODEL), D_MODEL ** -0.5)
    w_ple_proj = nrm(ks[16], (DEPTH, PLE_DIM, D_MODEL), beta * PLE_DIM ** -0.5)
    return {"x": x, "p": p, "positions": positions, "ln_g": ln_g, "ln_b": ln_b,
            "w_ffn1_gu": w_ffn1_gu, "w_ffn1_down": w_ffn1_down, "w_in": w_in,
            "ret_gn_g": ret_gn_g, "att_sinks": att_sinks, "w_ret_out": w_ret_out,
            "w_att_out": w_att_out, "w_mix_out": w_mix_out, "w_ffn2_gu": w_ffn2_gu,
            "w_ffn2_down": w_ffn2_down, "w_ple_gate": w_ple_gate, "w_ple_proj": w_ple_proj}


def reference(x, p, positions, ln_g, ln_b, w_ffn1_gu, w_ffn1_down, w_in, ret_gn_g, att_sinks,
              w_ret_out, w_att_out, w_mix_out, w_ffn2_gu, w_ffn2_down, w_ple_gate, w_ple_proj):
    B, S, _ = x.shape
    alpha = (2.0 * DEPTH) ** 0.25
    h = x
    for i in range(DEPTH):
        h = layer_norm(alpha * h + 0.5 * swiglu(h, w_ffn1_gu[i], w_ffn1_down[i]), ln_g[i, 0], ln_b[i, 0])

        rq, rk, rv, rg, aq, ak, av, gate_r, gate_a = jnp.split(h @ w_in[i], SPLIT_POINTS, axis=-1)

        rq = rotary(rq.reshape(B, S, RET_HEADS, RET_DK), positions, RET_DK, RET_THETA)
        rk = rotary(rk.reshape(B, S, RET_HEADS, RET_DK), positions, RET_DK, RET_THETA) * (RET_DK ** -0.5)
        rv = rv.reshape(B, S, RET_HEADS, RET_DV)
        ro = head_group_norm(retention_chunkwise(rq, rk, rv), ret_gn_g[i]).astype(h.dtype)
        ret_branch = (jax.nn.silu(rg) * ro) @ w_ret_out[i]

        aq = rotary(aq.reshape(B, S, ATT_HEADS, ATT_DH), positions, ROPE_DIMS, ROPE_THETA)
        ak = rotary(ak.reshape(B, S, ATT_KV_HEADS, ATT_DH), positions, ROPE_DIMS, ROPE_THETA)
        av = av.reshape(B, S, ATT_KV_HEADS, ATT_DH)
        att_branch = sliding_window_attention_sinks(aq, ak, av, att_sinks[i]) @ w_att_out[i]

        mixed = (jax.nn.sigmoid(gate_r) * ret_branch + jax.nn.sigmoid(gate_a) * att_branch) @ w_mix_out[i]
        h = layer_norm(alpha * h + mixed, ln_g[i, 1], ln_b[i, 1])

        h = layer_norm(alpha * h + 0.5 * swiglu(h, w_ffn2_gu[i], w_ffn2_down[i]), ln_g[i, 2], ln_b[i, 2])

        ple = jax.nn.sigmoid(h @ w_ple_gate[i]) * (p[i] @ w_ple_proj[i])
        h = layer_norm(alpha * h + ple, ln_g[i, 3], ln_b[i, 3])
    return h
```

```python
import functools

import jax
import jax.numpy as jnp
from jax import lax
from jax.experimental import pallas as pl
from jax.experimental.pallas import tpu as pltpu

F32 = jnp.float32
BF16 = jnp.bfloat16

D_MODEL = 2048
PLE_DIM = 256
D_FF = 5632
RET_HEADS = 8
RET_DK = 128
RET_DV = 256
RET_CHUNK = 128
RET_THETA = 10000.0
ATT_HEADS = 16
ATT_KV_HEADS = 4
ATT_DH = 64
ATT_GROUP = ATT_HEADS // ATT_KV_HEADS
WINDOW = 128
ATT_BLOCK = 128
ROPE_THETA = 500000.0
ROPE_DIMS = ATT_DH // 4
LN_EPS = 1e-5
GN_EPS = 1e-5

RET_QK = RET_HEADS * RET_DK
RET_V = RET_HEADS * RET_DV
ATT_Q = ATT_HEADS * ATT_DH
ATT_KV = ATT_KV_HEADS * ATT_DH
IN_SIZES = (RET_QK, RET_QK, RET_V, RET_V, ATT_Q, ATT_KV, ATT_KV, D_MODEL, D_MODEL)

LANES = 128
VMEM_LIMIT_BYTES = 56 * 1024 * 1024


def _params(*semantics):
    return pltpu.CompilerParams(dimension_semantics=semantics, vmem_limit_bytes=VMEM_LIMIT_BYTES)


def _resident(shape, index_map):
    return pl.BlockSpec(shape, index_map, pipeline_mode=pl.Buffered(1))


def _layer_norm(y, g, b):
    mu = jnp.mean(y, axis=-1, keepdims=True)
    d = y - mu
    var = jnp.mean(d * d, axis=-1, keepdims=True)
    return d * lax.rsqrt(var + LN_EPS) * g + b


def _dot(a, b):
    return jnp.dot(a, b, preferred_element_type=F32)


def _ffn_body(x_ref, wg_ref, wu_ref, wd_ref, g_ref, b_ref, o_ref, xb_ref, acc_ref, *, alpha):
    j = pl.program_id(1)

    @pl.when(j == 0)
    def _():
        xb_ref[...] = x_ref[...].astype(BF16)

    xb = xb_ref[...]
    gate = _dot(xb, wg_ref[...])
    up = _dot(xb, wu_ref[...])
    act = (gate * jax.nn.sigmoid(gate) * up).astype(BF16)
    part = _dot(act, wd_ref[...])

    @pl.when(j == 0)
    def _():
        acc_ref[...] = part

    @pl.when(j > 0)
    def _():
        acc_ref[...] += part

    @pl.when(j == pl.num_programs(1) - 1)
    def _():
        y = alpha * x_ref[...] + 0.5 * acc_ref[...]
        o_ref[...] = _layer_norm(y, g_ref[...], b_ref[...])


def _ffn_ln(x, w_gu, w_down, g, b, *, alpha, tm=512, tf=512):
    t, d = x.shape
    f = w_down.shape[0]
    nj = f // tf
    return pl.pallas_call(
        functools.partial(_ffn_body, alpha=alpha),
        out_shape=jax.ShapeDtypeStruct((t, d), F32),
        grid=(t // tm, nj),
        in_specs=[
            pl.BlockSpec((tm, d), lambda i, j: (i, 0)),
            pl.BlockSpec((d, tf), lambda i, j: (0, j)),
            pl.BlockSpec((d, tf), lambda i, j: (0, j + nj)),
            pl.BlockSpec((tf, d), lambda i, j: (j, 0)),
            pl.BlockSpec((1, d), lambda i, j: (0, 0)),
            pl.BlockSpec((1, d), lambda i, j: (0, 0)),
        ],
        out_specs=pl.BlockSpec((tm, d), lambda i, j: (i, 0)),
        scratch_shapes=[pltpu.VMEM((tm, d), BF16), pltpu.VMEM((tm, d), F32)],
        compiler_params=_params("parallel", "arbitrary"),
        name="ffn_ln",
    )(x, w_gu, w_gu, w_down, g, b)


def _rope_body(pos_ref, invr_ref, signr_ref, inva_ref, m1_ref, m2_ref,
               cr_ref, sr_ref, ca_ref, s1_ref, s2_ref):
    pos = pos_ref[...].astype(F32)
    ang_r = pos * invr_ref[...]
    cr_ref[...] = jnp.cos(ang_r)
    sr_ref[...] = jnp.sin(ang_r) * signr_ref[...]
    ang_a = pos * inva_ref[...]
    sin_a = jnp.sin(ang_a)
    ca_ref[...] = jnp.cos(ang_a)
    s1_ref[...] = sin_a * m1_ref[...]
    s2_ref[...] = sin_a * m2_ref[...]


def _rope_tables(pos, tm=1024):
    t = pos.shape[0]
    half_r = RET_DK // 2
    inv_r = 1.0 / (RET_THETA ** (jnp.arange(half_r, dtype=F32) / half_r))
    inv_r = jnp.concatenate([inv_r, inv_r])[None, :]
    sign_r = jnp.concatenate([-jnp.ones((half_r,), F32), jnp.ones((half_r,), F32)])[None, :]
    half_a = ROPE_DIMS // 2
    inv_a8 = 1.0 / (ROPE_THETA ** (jnp.arange(half_a, dtype=F32) / half_a))
    lane = jnp.arange(LANES) % ATT_DH
    inv_a = jnp.where(lane < ROPE_DIMS, inv_a8[lane % half_a], 0.0).astype(F32)[None, :]
    m1 = jnp.where((lane >= half_a) & (lane < ROPE_DIMS), 1.0, 0.0).astype(F32)[None, :]
    m2 = jnp.where(lane < half_a, -1.0, 0.0).astype(F32)[None, :]
    row = pl.BlockSpec((1, LANES), lambda i: (0, 0))
    tab = pl.BlockSpec((tm, LANES), lambda i: (i, 0))
    return pl.pallas_call(
        _rope_body,
        out_shape=[jax.ShapeDtypeStruct((t, LANES), F32)] * 5,
        grid=(t // tm,),
        in_specs=[pl.BlockSpec((tm, 1), lambda i: (i, 0)), row, row, row, row, row],
        out_specs=[tab] * 5,
        compiler_params=_params("parallel"),
        name="rope_tables",
    )(pos, inv_r, sign_r, inv_a, m1, m2)


def _proj_ret_qk_body(x_ref, w_ref, cr_ref, sr_ref, o_ref):
    r = _dot(x_ref[...].astype(BF16), w_ref[...])
    cr = cr_ref[...]
    sr = sr_ref[...]
    k_scale = RET_DK ** -0.5
    for s in range(2 * RET_HEADS):
        x = r[:, s * LANES:(s + 1) * LANES]
        rot = x * cr + pltpu.roll(x, RET_DK // 2, axis=1) * sr
        if s >= RET_HEADS:
            rot = rot * k_scale
        o_ref[:, s * LANES:(s + 1) * LANES] = rot.astype(BF16)


def _proj_att_body(x_ref, w_ref, ca_ref, s1_ref, s2_ref, o_ref):
    r = _dot(x_ref[...].astype(BF16), w_ref[...])
    ca = ca_ref[...]
    s1 = s1_ref[...]
    s2 = s2_ref[...]
    half = ROPE_DIMS // 2
    n_rot = (ATT_Q + ATT_KV) // LANES
    for s in range(n_rot):
        x = r[:, s * LANES:(s + 1) * LANES]
        rot = x * ca + pltpu.roll(x, half, axis=1) * s1 + pltpu.roll(x, LANES - half, axis=1) * s2
        o_ref[:, s * LANES:(s + 1) * LANES] = rot.astype(BF16)
    o_ref[:, n_rot * LANES:] = r[:, n_rot * LANES:].astype(BF16)


def _proj_plain_body(x_ref, w_ref, o_ref):
    o_ref[...] = _dot(x_ref[...].astype(BF16), w_ref[...]).astype(BF16)


def _proj(body, x, w, tables=(), *, tm=512, name):
    t, d = x.shape
    n = w.shape[1]
    tab = pl.BlockSpec((tm, LANES), lambda i: (i, 0))
    return pl.pallas_call(
        body,
        out_shape=jax.ShapeDtypeStruct((t, n), BF16),
        grid=(t // tm,),
        in_specs=[pl.BlockSpec((tm, d), lambda i: (i, 0)), _resident((d, n), lambda i: (0, 0))]
        + [tab] * len(tables),
        out_specs=pl.BlockSpec((tm, n), lambda i: (i, 0)),
        compiler_params=_params("parallel"),
        name=name,
    )(x, w, *tables)


def _retention_body(q_ref, k_ref, v_ref, g_ref, inner_ref, qd_ref, kd_ref, cd_ref, gn_ref,
                    o_ref, state_ref, *, n_chunks):
    @pl.when(pl.program_id(2) == 0)
    def _():
        state_ref[...] = jnp.zeros_like(state_ref)

    inner_decay = inner_ref[...]
    q_decay = qd_ref[...]
    k_decay = kd_ref[...]
    chunk_decay = cd_ref[...]
    gn = gn_ref[...]
    for c in range(n_chunks):
        rows = pl.ds(c * RET_CHUNK, RET_CHUNK)
        qn = q_ref[rows, :]
        kn = k_ref[rows, :]
        vn = v_ref[rows, :]
        scores = lax.dot_general(qn, kn, (((1,), (1,)), ((), ())), preferred_element_type=F32)
        inner = _dot((scores * inner_decay).astype(BF16), vn)
        state = state_ref[...]
        cross = _dot(qn, state.astype(BF16)) * q_decay
        kv = (vn.astype(F32) * k_decay).astype(BF16)
        state_ref[...] = state * chunk_decay + lax.dot_general(
            kn, kv, (((0,), (0,)), ((), ())), preferred_element_type=F32)
        o = inner + cross
        mu = jnp.mean(o, axis=-1, keepdims=True)
        d = o - mu
        var = jnp.mean(d * d, axis=-1, keepdims=True)
        y = d * lax.rsqrt(var + GN_EPS) * gn
        gate = g_ref[rows, :].astype(F32)
        o_ref[rows, :] = (gate * jax.nn.sigmoid(gate) * y).astype(BF16)


def _retention_decays():
    c = RET_CHUNK
    log_g = jnp.log1p(-jnp.exp2(-5.0 - jnp.arange(RET_HEADS, dtype=F32)))
    idx = jnp.arange(c, dtype=F32)
    q_decay = jnp.exp((idx[None, :] + 1.0) * log_g[:, None])
    k_decay = jnp.exp((c - 1.0 - idx)[None, :] * log_g[:, None])
    diff = idx[:, None] - idx[None, :]
    inner = jnp.where(diff[None] >= 0,
                      jnp.exp(jnp.maximum(diff, 0.0)[None] * log_g[:, None, None]), 0.0)
    chunk_decay = jnp.exp(c * log_g)
    q_decay = jnp.broadcast_to(q_decay[:, :, None], (RET_HEADS, c, RET_DV))
    k_decay = jnp.broadcast_to(k_decay[:, :, None], (RET_HEADS, c, RET_DV))
    chunk_decay = jnp.broadcast_to(chunk_decay[:, None, None], (RET_HEADS, 1, RET_DV))
    return inner, q_decay, k_decay, chunk_decay


def _retention(qk, vg, gn_g, batch, seq, *, ts=1024):
    t = qk.shape[0]
    ns = seq // ts
    inner, q_decay, k_decay, chunk_decay = _retention_decays()
    gn = gn_g.reshape(RET_HEADS, 1, RET_DV)

    def rows(b, h, s):
        return b * ns + s

    head_tab = lambda shape: pl.BlockSpec((None,) + shape, lambda b, h, s: (h, 0, 0))
    return pl.pallas_call(
        functools.partial(_retention_body, n_chunks=ts // RET_CHUNK),
        out_shape=jax.ShapeDtypeStruct((t, RET_V), BF16),
        grid=(batch, RET_HEADS, ns),
        in_specs=[
            pl.BlockSpec((ts, RET_DK), lambda b, h, s: (rows(b, h, s), h)),
            pl.BlockSpec((ts, RET_DK), lambda b, h, s: (rows(b, h, s), RET_HEADS + h)),
            pl.BlockSpec((ts, RET_DV), lambda b, h, s: (rows(b, h, s), h)),
            pl.BlockSpec((ts, RET_DV), lambda b, h, s: (rows(b, h, s), RET_HEADS + h)),
            head_tab((RET_CHUNK, RET_CHUNK)),
            head_tab((RET_CHUNK, RET_DV)),
            head_tab((RET_CHUNK, RET_DV)),
            head_tab((1, RET_DV)),
            head_tab((1, RET_DV)),
        ],
        out_specs=pl.BlockSpec((ts, RET_DV), lambda b, h, s: (rows(b, h, s), h)),
        scratch_shapes=[pltpu.VMEM((RET_DK, RET_DV), F32)],
        compiler_params=_params("parallel", "parallel", "arbitrary"),
        name="retention",
    )(qk, qk, vg, vg, inner, q_decay, k_decay, chunk_decay, gn)


def _swa_body(sink_ref, q_ref, kvc_ref, kvp_ref, o_ref):
    n = pl.program_id(1)
    c = ATT_BLOCK
    qi = lax.broadcasted_iota(jnp.int32, (c, 2 * c), 0) + c
    kj = lax.broadcasted_iota(jnp.int32, (c, 2 * c), 1)
    rel = qi - kj
    first_key = jnp.where(n > 0, 0, c)
    valid = (rel >= 0) & (rel < WINDOW) & (kj >= first_key)
    scale = ATT_DH ** -0.5
    for hk in range(ATT_KV_HEADS):
        kcols = slice(hk * ATT_DH, (hk + 1) * ATT_DH)
        vcols = slice(ATT_KV + hk * ATT_DH, ATT_KV + (hk + 1) * ATT_DH)
        kb = jnp.concatenate([kvp_ref[:, kcols], kvc_ref[:, kcols]], axis=0)
        vb = jnp.concatenate([kvp_ref[:, vcols], kvc_ref[:, vcols]], axis=0)
        for g in range(ATT_GROUP):
            hd = hk * ATT_GROUP + g
            qcols = slice(hd * ATT_DH, (hd + 1) * ATT_DH)
            s = lax.dot_general(q_ref[:, qcols], kb, (((1,), (1,)), ((), ())),
                                preferred_element_type=F32) * scale
            s = jnp.where(valid, s, -jnp.inf)
            sink = sink_ref[hd]
            m = jnp.maximum(jnp.max(s, axis=-1, keepdims=True), sink)
            e = jnp.exp(s - m)
            denom = jnp.sum(e, axis=-1, keepdims=True) + jnp.exp(sink - m)
            p = (e / denom).astype(BF16)
            o_ref[:, qcols] = _dot(p, vb).astype(BF16)


def _swa(aqkv, sinks, batch, seq):
    t = aqkv.shape[0]
    nb = seq // ATT_BLOCK
    kv_col = ATT_Q // (2 * ATT_KV)
    return pl.pallas_call(
        _swa_body,
        out_shape=jax.ShapeDtypeStruct((t, ATT_Q), BF16),
        grid=(batch, nb),
        in_specs=[
            pl.BlockSpec(memory_space=pltpu.SMEM),
            pl.BlockSpec((ATT_BLOCK, ATT_Q), lambda b, n: (b * nb + n, 0)),
            pl.BlockSpec((ATT_BLOCK, 2 * ATT_KV), lambda b, n: (b * nb + n, kv_col)),
            pl.BlockSpec((ATT_BLOCK, 2 * ATT_KV),
                         lambda b, n: (b * nb + jnp.maximum(n - 1, 0), kv_col)),
        ],
        out_specs=pl.BlockSpec((ATT_BLOCK, ATT_Q), lambda b, n: (b * nb + n, 0)),
        compiler_params=_params("parallel", "arbitrary"),
        name="swa",
    )(sinks, aqkv, aqkv, aqkv)


def _mix_body(h_ref, ret_ref, att_ref, gr_ref, ga_ref, wr_ref, wa_ref, wm_ref, g_ref, b_ref,
              o_ref, *, alpha):
    ret_branch = _dot(ret_ref[...], wr_ref[...])
    att_branch = _dot(att_ref[...], wa_ref[...])
    merged = (jax.nn.sigmoid(gr_ref[...].astype(F32)) * ret_branch
              + jax.nn.sigmoid(ga_ref[...].astype(F32)) * att_branch)
    mixed = _dot(merged.astype(BF16), wm_ref[...])
    o_ref[...] = _layer_norm(alpha * h_ref[...] + mixed, g_ref[...], b_ref[...])


def _mix_ln(h, ret, att, gates, w_ret, w_att, w_mix, g, b, *, alpha, tm=256):
    t, d = h.shape
    row = pl.BlockSpec((1, d), lambda i: (0, 0))
    return pl.pallas_call(
        functools.partial(_mix_body, alpha=alpha),
        out_shape=jax.ShapeDtypeStruct((t, d), F32),
        grid=(t // tm,),
        in_specs=[
            pl.BlockSpec((tm, d), lambda i: (i, 0)),
            pl.BlockSpec((tm, RET_V), lambda i: (i, 0)),
            pl.BlockSpec((tm, ATT_Q), lambda i: (i, 0)),
            pl.BlockSpec((tm, d), lambda i: (i, 0)),
            pl.BlockSpec((tm, d), lambda i: (i, 1)),
            _resident((RET_V, d), lambda i: (0, 0)),
            _resident((ATT_Q, d), lambda i: (0, 0)),
            _resident((d, d), lambda i: (0, 0)),
            row, row,
        ],
        out_specs=pl.BlockSpec((tm, d), lambda i: (i, 0)),
        compiler_params=_params("parallel"),
        name="mix_ln",
    )(h, ret, att, gates, gates, w_ret, w_att, w_mix, g, b)


def _ple_body(h_ref, p_ref, wg_ref, wp_ref, g_ref, b_ref, o_ref, *, alpha):
    h = h_ref[...]
    gate = jax.nn.sigmoid(_dot(h.astype(BF16), wg_ref[...]))
    ple = gate * _dot(p_ref[...].astype(BF16), wp_ref[...])
    o_ref[...] = _layer_norm(alpha * h + ple, g_ref[...], b_ref[...])


def _ple_ln(h, p, w_gate, w_proj, g, b, *, alpha, tm=512):
    t, d = h.shape
    pd = p.shape[1]
    row = pl.BlockSpec((1, d), lambda i: (0, 0))
    return pl.pallas_call(
        functools.partial(_ple_body, alpha=alpha),
        out_shape=jax.ShapeDtypeStruct((t, d), F32),
        grid=(t // tm,),
        in_specs=[
            pl.BlockSpec((tm, d), lambda i: (i, 0)),
            pl.BlockSpec((tm, pd), lambda i: (i, 0)),
            _resident((d, d), lambda i: (0, 0)),
            _resident((pd, d), lambda i: (0, 0)),
            row, row,
        ],
        out_specs=pl.BlockSpec((tm, d), lambda i: (i, 0)),
        compiler_params=_params("parallel"),
        name="ple_ln",
    )(h, p, w_gate, w_proj, g, b)


def kernel(x, p, positions, ln_g, ln_b, w_ffn1_gu, w_ffn1_down, w_in, ret_gn_g, att_sinks,
           w_ret_out, w_att_out, w_mix_out, w_ffn2_gu, w_ffn2_down, w_ple_gate, w_ple_proj):
    batch, seq, d = x.shape
    depth = ln_g.shape[0]
    t = batch * seq
    alpha = (2.0 * depth) ** 0.25
    h = x.reshape(t, d)
    cr, sr, ca, s1, s2 = _rope_tables(positions.reshape(t, 1))
    bounds = [0]
    for size in IN_SIZES:
        bounds.append(bounds[-1] + size)
    for i in range(depth):
        g = ln_g[i][:, None, :]
        b = ln_b[i][:, None, :]
        w_in_b = w_in[i].astype(BF16)
        w_qk = w_in_b[:, bounds[0]:bounds[2]]
        w_vg = w_in_b[:, bounds[2]:bounds[4]]
        w_att = w_in_b[:, bounds[4]:bounds[7]]
        w_gates = w_in_b[:, bounds[7]:bounds[9]]

        h = _ffn_ln(h, w_ffn1_gu[i].astype(BF16), w_ffn1_down[i].astype(BF16), g[0], b[0], alpha=alpha)

        qk = _proj(_proj_ret_qk_body, h, w_qk, (cr, sr), name="proj_ret_qk")
        vg = _proj(_proj_plain_body, h, w_vg, name="proj_ret_vg")
        aqkv = _proj(_proj_att_body, h, w_att, (ca, s1, s2), name="proj_att")
        gates = _proj(_proj_plain_body, h, w_gates, name="proj_gates")

        ret = _retention(qk, vg, ret_gn_g[i], batch, seq)
        att = _swa(aqkv, att_sinks[i], batch, seq)

        h = _mix_ln(h, ret, att, gates, w_ret_out[i].astype(BF16), w_att_out[i].astype(BF16),
                    w_mix_out[i].astype(BF16), g[1], b[1], alpha=alpha)
        h = _ffn_ln(h, w_ffn2_gu[i].astype(BF16), w_ffn2_down[i].astype(BF16), g[2], b[2], alpha=alpha)
        h = _ple_ln(h, p[i].reshape(t, PLE_DIM), w_ple_gate[i].astype(BF16),
                    w_ple_proj[i].astype(BF16), g[3], b[3], alpha=alpha)
    return h.reshape(batch, seq, d)
```

```python
import functools

import jax
import jax.numpy as jnp
from jax import lax
from jax.experimental import pallas as pl
from jax.experimental.pallas import tpu as pltpu

F32 = jnp.float32
BF16 = jnp.bfloat16

D_MODEL = 2048
PLE_DIM = 256
D_FF = 5632
RET_HEADS = 8
RET_DK = 128
RET_DV = 256
RET_CHUNK = 128
RET_THETA = 10000.0
ATT_HEADS = 16
ATT_KV_HEADS = 4
ATT_DH = 64
ATT_GROUP = ATT_HEADS // ATT_KV_HEADS
WINDOW = 128
ATT_BLOCK = 128
ROPE_THETA = 500000.0
ROPE_DIMS = ATT_DH // 4
LN_EPS = 1e-5
GN_EPS = 1e-5

RET_QK = RET_HEADS * RET_DK
RET_V = RET_HEADS * RET_DV
ATT_Q = ATT_HEADS * ATT_DH
ATT_KV = ATT_KV_HEADS * ATT_DH
IN_SIZES = (RET_QK, RET_QK, RET_V, RET_V, ATT_Q, ATT_KV, ATT_KV, D_MODEL, D_MODEL)

LANES = 128
VMEM_LIMIT_BYTES = 56 * 1024 * 1024


def _params(*semantics):
    return pltpu.CompilerParams(dimension_semantics=semantics, vmem_limit_bytes=VMEM_LIMIT_BYTES)


def _resident(shape, index_map):
    return pl.BlockSpec(shape, index_map, pipeline_mode=pl.Buffered(1))


def _layer_norm(y, g, b):
    mu = jnp.mean(y, axis=-1, keepdims=True)
    d = y - mu
    var = jnp.mean(d * d, axis=-1, keepdims=True)
    return d * lax.rsqrt(var + LN_EPS) * g + b


def _dot(a, b):
    return jnp.dot(a, b, preferred_element_type=F32)


def _ffn_body(x_ref, wg_ref, wu_ref, wd_ref, g_ref, b_ref, o_ref, xb_ref, *, alpha, n_sub):
    j = pl.program_id(1)
    sub = x_ref.shape[0] // n_sub

    @pl.when(j == 0)
    def _():
        xb_ref[...] = x_ref[...].astype(BF16)
        o_ref[...] = jnp.zeros_like(o_ref)

    for r in range(n_sub):
        rows = pl.ds(r * sub, sub)
        xb = xb_ref[rows, :]
        gate = _dot(xb, wg_ref[...])
        up = _dot(xb, wu_ref[...])
        act = (gate * jax.nn.sigmoid(gate) * up).astype(BF16)
        o_ref[rows, :] += _dot(act, wd_ref[...])

    @pl.when(j == pl.num_programs(1) - 1)
    def _():
        for r in range(n_sub):
            rows = pl.ds(r * sub, sub)
            y = alpha * x_ref[rows, :] + 0.5 * o_ref[rows, :]
            o_ref[rows, :] = _layer_norm(y, g_ref[...], b_ref[...])


def _ffn_ln(x, w_gu, w_down, g, b, *, alpha, tm=1024, tf=512, n_sub=2):
    t, d = x.shape
    f = w_down.shape[0]
    nj = f // tf
    return pl.pallas_call(
        functools.partial(_ffn_body, alpha=alpha, n_sub=n_sub),
        out_shape=jax.ShapeDtypeStruct((t, d), F32),
        grid=(t // tm, nj),
        in_specs=[
            pl.BlockSpec((tm, d), lambda i, j: (i, 0)),
            pl.BlockSpec((d, tf), lambda i, j: (0, j)),
            pl.BlockSpec((d, tf), lambda i, j: (0, j + nj)),
            pl.BlockSpec((tf, d), lambda i, j: (j, 0)),
            pl.BlockSpec((1, d), lambda i, j: (0, 0)),
            pl.BlockSpec((1, d), lambda i, j: (0, 0)),
        ],
        out_specs=pl.BlockSpec((tm, d), lambda i, j: (i, 0)),
        scratch_shapes=[pltpu.VMEM((tm, d), BF16)],
        compiler_params=_params("parallel", "arbitrary"),
        name="ffn_ln",
    )(x, w_gu, w_gu, w_down, g, b)


def _rope_body(pos_ref, invr_ref, signr_ref, inva_ref, m1_ref, m2_ref,
               cr_ref, sr_ref, ca_ref, s1_ref, s2_ref):
    pos = pos_ref[...].astype(F32)
    ang_r = pos * invr_ref[...]
    cr_ref[...] = jnp.cos(ang_r)
    sr_ref[...] = jnp.sin(ang_r) * signr_ref[...]
    ang_a = pos * inva_ref[...]
    sin_a = jnp.sin(ang_a)
    ca_ref[...] = jnp.cos(ang_a)
    s1_ref[...] = sin_a * m1_ref[...]
    s2_ref[...] = sin_a * m2_ref[...]


def _rope_tables(pos, tm=1024):
    t = pos.shape[0]
    half_r = RET_DK // 2
    inv_r = 1.0 / (RET_THETA ** (jnp.arange(half_r, dtype=F32) / half_r))
    inv_r = jnp.concatenate([inv_r, inv_r])[None, :]
    sign_r = jnp.concatenate([-jnp.ones((half_r,), F32), jnp.ones((half_r,), F32)])[None, :]
    half_a = ROPE_DIMS // 2
    inv_a8 = 1.0 / (ROPE_THETA ** (jnp.arange(half_a, dtype=F32) / half_a))
    lane = jnp.arange(LANES) % ATT_DH
    inv_a = jnp.where(lane < ROPE_DIMS, inv_a8[lane % half_a], 0.0).astype(F32)[None, :]
    m1 = jnp.where((lane >= half_a) & (lane < ROPE_DIMS), 1.0, 0.0).astype(F32)[None, :]
    m2 = jnp.where(lane < half_a, -1.0, 0.0).astype(F32)[None, :]
    row = pl.BlockSpec((1, LANES), lambda i: (0, 0))
    tab = pl.BlockSpec((tm, LANES), lambda i: (i, 0))
    return pl.pallas_call(
        _rope_body,
        out_shape=[jax.ShapeDtypeStruct((t, LANES), F32)] * 5,
        grid=(t // tm,),
        in_specs=[pl.BlockSpec((tm, 1), lambda i: (i, 0)), row, row, row, row, row],
        out_specs=[tab] * 5,
        compiler_params=_params("parallel"),
        name="rope_tables",
    )(pos, inv_r, sign_r, inv_a, m1, m2)


def _proj_ret_qk_body(x_ref, w_ref, cr_ref, sr_ref, kdec_ref, o_ref, *, n_sub):
    sub = x_ref.shape[0] // n_sub
    k_scale = RET_DK ** -0.5
    for r0 in range(0, x_ref.shape[0], sub):
        r = _dot(x_ref[r0:r0 + sub, :].astype(BF16), w_ref[...])
        cr = cr_ref[r0:r0 + sub, :]
        sr = sr_ref[r0:r0 + sub, :]
        for s in range(2 * RET_HEADS):
            x = r[:, s * LANES:(s + 1) * LANES]
            rot = x * cr + pltpu.roll(x, RET_DK // 2, axis=1) * sr
            if s < RET_HEADS:
                o_ref[r0:r0 + sub, s * LANES:(s + 1) * LANES] = rot.astype(BF16)
                continue
            rot = rot * k_scale
            o_ref[r0:r0 + sub, s * LANES:(s + 1) * LANES] = rot.astype(BF16)
            k_decay = kdec_ref[:, (s - RET_HEADS) * LANES:(s - RET_HEADS + 1) * LANES]
            d = s + RET_HEADS
            for c0 in range(0, sub, RET_CHUNK):
                o_ref[r0 + c0:r0 + c0 + RET_CHUNK, d * LANES:(d + 1) * LANES] = (
                    rot[c0:c0 + RET_CHUNK] * k_decay).astype(BF16)


def _proj_att_body(x_ref, w_ref, ca_ref, s1_ref, s2_ref, o_ref, *, n_sub):
    sub = x_ref.shape[0] // n_sub
    half = ROPE_DIMS // 2
    n_rot = (ATT_Q + ATT_KV) // LANES
    for r0 in range(0, x_ref.shape[0], sub):
        r = _dot(x_ref[r0:r0 + sub, :].astype(BF16), w_ref[...])
        ca = ca_ref[r0:r0 + sub, :]
        s1 = s1_ref[r0:r0 + sub, :]
        s2 = s2_ref[r0:r0 + sub, :]
        for s in range(n_rot):
            x = r[:, s * LANES:(s + 1) * LANES]
            rot = (x * ca + pltpu.roll(x, half, axis=1) * s1
                   + pltpu.roll(x, LANES - half, axis=1) * s2)
            o_ref[r0:r0 + sub, s * LANES:(s + 1) * LANES] = rot.astype(BF16)
        o_ref[r0:r0 + sub, n_rot * LANES:] = r[:, n_rot * LANES:].astype(BF16)


def _proj_ret_vg_body(x_ref, w_ref, o_ref):
    r = _dot(x_ref[...].astype(BF16), w_ref[...])
    o_ref[:, :RET_V] = r[:, :RET_V].astype(BF16)
    gate = r[:, RET_V:]
    o_ref[:, RET_V:] = (gate * jax.nn.sigmoid(gate)).astype(BF16)


def _proj_gates_body(x_ref, w_ref, o_ref):
    o_ref[...] = jax.nn.sigmoid(_dot(x_ref[...].astype(BF16), w_ref[...])).astype(BF16)


def _proj(body, x, w, tables=(), consts=(), *, n_out=None, tm=512, name):
    t, d = x.shape
    n = w.shape[1]
    n_out = n if n_out is None else n_out
    tab = pl.BlockSpec((tm, LANES), lambda i: (i, 0))
    return pl.pallas_call(
        body,
        out_shape=jax.ShapeDtypeStruct((t, n_out), BF16),
        grid=(t // tm,),
        in_specs=[pl.BlockSpec((tm, d), lambda i: (i, 0)), _resident((d, n), lambda i: (0, 0))]
        + [tab] * len(tables) + [_resident(c.shape, lambda i: (0, 0)) for c in consts],
        out_specs=pl.BlockSpec((tm, n_out), lambda i: (i, 0)),
        compiler_params=_params("parallel"),
        name=name,
    )(x, w, *tables, *consts)


def _retention_body(q_ref, k_ref, kd_ref, v_ref, g_ref, inner_ref, qd_ref, cd_ref, gn_ref,
                    o_ref, state_ref, *, n_chunks):
    @pl.when(pl.program_id(2) == 0)
    def _():
        state_ref[...] = jnp.zeros_like(state_ref)

    inner_decay = inner_ref[...]
    q_decay = qd_ref[...]
    chunk_decay = cd_ref[...]
    gn = gn_ref[...]
    for c in range(n_chunks):
        rows = pl.ds(c * RET_CHUNK, RET_CHUNK)
        qn = q_ref[rows, :]
        vn = v_ref[rows, :]
        scores = lax.dot_general(qn, k_ref[rows, :], (((1,), (1,)), ((), ())),
                                 preferred_element_type=F32)
        inner = _dot((scores * inner_decay).astype(BF16), vn)
        state = state_ref[...]
        cross = _dot(qn, state.astype(BF16)) * q_decay
        state_ref[...] = state * chunk_decay + lax.dot_general(
            kd_ref[rows, :], vn, (((0,), (0,)), ((), ())), preferred_element_type=F32)
        o = inner + cross
        mu = jnp.mean(o, axis=-1, keepdims=True)
        d = o - mu
        var = jnp.mean(d * d, axis=-1, keepdims=True)
        y = d * lax.rsqrt(var + GN_EPS) * gn
        o_ref[rows, :] = (g_ref[rows, :].astype(F32) * y).astype(BF16)


def _retention_decays():
    c = RET_CHUNK
    log_g = jnp.log1p(-jnp.exp2(-5.0 - jnp.arange(RET_HEADS, dtype=F32)))
    idx = jnp.arange(c, dtype=F32)
    q_decay = jnp.exp((idx[None, :] + 1.0) * log_g[:, None])
    k_decay = jnp.exp((c - 1.0 - idx)[None, :] * log_g[:, None])
    diff = idx[:, None] - idx[None, :]
    inner = jnp.where(diff[None] >= 0,
                      jnp.exp(jnp.maximum(diff, 0.0)[None] * log_g[:, None, None]), 0.0)
    chunk_decay = jnp.exp(c * log_g)
    q_decay = jnp.broadcast_to(q_decay[:, :, None], (RET_HEADS, c, RET_DV))
    k_decay = jnp.broadcast_to(k_decay.T[:, :, None], (c, RET_HEADS, RET_DK)).reshape(c, RET_QK)
    chunk_decay = jnp.broadcast_to(chunk_decay[:, None, None], (RET_HEADS, 1, RET_DV))
    return inner, q_decay, k_decay, chunk_decay


def _retention(qkd, vg, inner, q_decay, chunk_decay, gn_g, batch, seq, *, ts=4096):
    t = qkd.shape[0]
    ns = seq // ts
    gn = gn_g.reshape(RET_HEADS, 1, RET_DV)

    def rows(b, h, s):
        return b * ns + s

    head_tab = lambda shape: pl.BlockSpec((None,) + shape, lambda b, h, s: (h, 0, 0))
    return pl.pallas_call(
        functools.partial(_retention_body, n_chunks=ts // RET_CHUNK),
        out_shape=jax.ShapeDtypeStruct((t, RET_V), BF16),
        grid=(batch, RET_HEADS, ns),
        in_specs=[
            pl.BlockSpec((ts, RET_DK), lambda b, h, s: (rows(b, h, s), h)),
            pl.BlockSpec((ts, RET_DK), lambda b, h, s: (rows(b, h, s), RET_HEADS + h)),
            pl.BlockSpec((ts, RET_DK), lambda b, h, s: (rows(b, h, s), 2 * RET_HEADS + h)),
            pl.BlockSpec((ts, RET_DV), lambda b, h, s: (rows(b, h, s), h)),
            pl.BlockSpec((ts, RET_DV), lambda b, h, s: (rows(b, h, s), RET_HEADS + h)),
            head_tab((RET_CHUNK, RET_CHUNK)),
            head_tab((RET_CHUNK, RET_DV)),
            head_tab((1, RET_DV)),
            head_tab((1, RET_DV)),
        ],
        out_specs=pl.BlockSpec((ts, RET_DV), lambda b, h, s: (rows(b, h, s), h)),
        scratch_shapes=[pltpu.VMEM((RET_DK, RET_DV), F32)],
        compiler_params=_params("parallel", "parallel", "arbitrary"),
        name="retention",
    )(qkd, qkd, qkd, vg, vg, inner, q_decay, chunk_decay, gn)


def _swa_body(sink_ref, q_ref, kvc_ref, kvp_ref, o_ref):
    n = pl.program_id(1)
    c = ATT_BLOCK
    qi = lax.broadcasted_iota(jnp.int32, (c, 2 * c), 0) + c
    kj = lax.broadcasted_iota(jnp.int32, (c, 2 * c), 1)
    rel = qi - kj
    first_key = jnp.where(n > 0, 0, c)
    valid = (rel >= 0) & (rel < WINDOW) & (kj >= first_key)
    scale = ATT_DH ** -0.5
    for hk in range(ATT_KV_HEADS):
        kcols = slice(hk * ATT_DH, (hk + 1) * ATT_DH)
        vcols = slice(ATT_KV + hk * ATT_DH, ATT_KV + (hk + 1) * ATT_DH)
        kb = jnp.concatenate([kvp_ref[:, kcols], kvc_ref[:, kcols]], axis=0)
        vb = jnp.concatenate([kvp_ref[:, vcols], kvc_ref[:, vcols]], axis=0)
        heads = [hk * ATT_GROUP + g for g in range(ATT_GROUP)]
        q4 = jnp.concatenate([q_ref[:, hd * ATT_DH:(hd + 1) * ATT_DH] for hd in heads], axis=0)
        s4 = lax.dot_general(q4 * scale, kb, (((1,), (1,)), ((), ())), preferred_element_type=F32)
        probs = []
        for g, hd in enumerate(heads):
            s = jnp.where(valid, s4[g * c:(g + 1) * c], -jnp.inf)
            sink = sink_ref[hd]
            m = jnp.maximum(jnp.max(s, axis=-1, keepdims=True), sink)
            e = jnp.exp(s - m)
            denom = jnp.sum(e, axis=-1, keepdims=True) + jnp.exp(sink - m)
            probs.append((e * (1.0 / denom)).astype(BF16))
        o4 = _dot(jnp.concatenate(probs, axis=0), vb)
        for pair in range(ATT_GROUP // 2):
            lo = o4[(2 * pair) * c:(2 * pair + 1) * c]
            hi = o4[(2 * pair + 1) * c:(2 * pair + 2) * c]
            col = heads[2 * pair] * ATT_DH
            o_ref[:, col:col + 2 * ATT_DH] = jnp.concatenate([lo, hi], axis=1).astype(BF16)


def _swa(aqkv, sinks, batch, seq):
    t = aqkv.shape[0]
    nb = seq // ATT_BLOCK
    kv_col = ATT_Q // (2 * ATT_KV)
    return pl.pallas_call(
        _swa_body,
        out_shape=jax.ShapeDtypeStruct((t, ATT_Q), BF16),
        grid=(batch, nb),
        in_specs=[
            pl.BlockSpec(memory_space=pltpu.SMEM),
            pl.BlockSpec((ATT_BLOCK, ATT_Q), lambda b, n: (b * nb + n, 0)),
            pl.BlockSpec((ATT_BLOCK, 2 * ATT_KV), lambda b, n: (b * nb + n, kv_col)),
            pl.BlockSpec((ATT_BLOCK, 2 * ATT_KV),
                         lambda b, n: (b * nb + jnp.maximum(n - 1, 0), kv_col)),
        ],
        out_specs=pl.BlockSpec((ATT_BLOCK, ATT_Q), lambda b, n: (b * nb + n, 0)),
        compiler_params=_params("parallel", "arbitrary"),
        name="swa",
    )(sinks, aqkv, aqkv, aqkv)


def _mix_body(h_ref, ret_ref, att_ref, gr_ref, ga_ref, wr_ref, wa_ref, wm_ref, g_ref, b_ref,
              o_ref, *, alpha):
    ret_branch = _dot(ret_ref[...], wr_ref[...])
    att_branch = _dot(att_ref[...], wa_ref[...])
    merged = gr_ref[...].astype(F32) * ret_branch + ga_ref[...].astype(F32) * att_branch
    mixed = _dot(merged.astype(BF16), wm_ref[...])
    o_ref[...] = _layer_norm(alpha * h_ref[...] + mixed, g_ref[...], b_ref[...])


def _mix_ln(h, ret, att, gates, w_ret, w_att, w_mix, g, b, *, alpha, tm=256):
    t, d = h.shape
    row = pl.BlockSpec((1, d), lambda i: (0, 0))
    return pl.pallas_call(
        functools.partial(_mix_body, alpha=alpha),
        out_shape=jax.ShapeDtypeStruct((t, d), F32),
        grid=(t // tm,),
        in_specs=[
            pl.BlockSpec((tm, d), lambda i: (i, 0)),
            pl.BlockSpec((tm, RET_V), lambda i: (i, 0)),
            pl.BlockSpec((tm, ATT_Q), lambda i: (i, 0)),
            pl.BlockSpec((tm, d), lambda i: (i, 0)),
            pl.BlockSpec((tm, d), lambda i: (i, 1)),
            _resident((RET_V, d), lambda i: (0, 0)),
            _resident((ATT_Q, d), lambda i: (0, 0)),
            _resident((d, d), lambda i: (0, 0)),
            row, row,
        ],
        out_specs=pl.BlockSpec((tm, d), lambda i: (i, 0)),
        compiler_params=_params("parallel"),
        name="mix_ln",
    )(h, ret, att, gates, gates, w_ret, w_att, w_mix, g, b)


def _ple_body(h_ref, p_ref, wg_ref, wp_ref, g_ref, b_ref, o_ref, *, alpha, n_sub):
    sub = h_ref.shape[0] // n_sub
    for r0 in range(0, h_ref.shape[0], sub):
        h = h_ref[r0:r0 + sub, :]
        gate = jax.nn.sigmoid(_dot(h.astype(BF16), wg_ref[...]))
        ple = gate * _dot(p_ref[r0:r0 + sub, :].astype(BF16), wp_ref[...])
        o_ref[r0:r0 + sub, :] = _layer_norm(alpha * h + ple, g_ref[...], b_ref[...])


def _ple_ln(h, p, w_gate, w_proj, g, b, *, alpha, tm=1024, n_sub=2):
    t, d = h.shape
    pd = p.shape[1]
    row = pl.BlockSpec((1, d), lambda i: (0, 0))
    return pl.pallas_call(
        functools.partial(_ple_body, alpha=alpha, n_sub=n_sub),
        out_shape=jax.ShapeDtypeStruct((t, d), F32),
        grid=(t // tm,),
        in_specs=[
            pl.BlockSpec((tm, d), lambda i: (i, 0)),
            pl.BlockSpec((tm, pd), lambda i: (i, 0)),
            _resident((d, d), lambda i: (0, 0)),
            _resident((pd, d), lambda i: (0, 0)),
            row, row,
        ],
        out_specs=pl.BlockSpec((tm, d), lambda i: (i, 0)),
        compiler_params=_params("parallel"),
        name="ple_ln",
    )(h, p, w_gate, w_proj, g, b)


def kernel(x, p, positions, ln_g, ln_b, w_ffn1_gu, w_ffn1_down, w_in, ret_gn_g, att_sinks,
           w_ret_out, w_att_out, w_mix_out, w_ffn2_gu, w_ffn2_down, w_ple_gate, w_ple_proj):
    batch, seq, d = x.shape
    depth = ln_g.shape[0]
    t = batch * seq
    alpha = (2.0 * depth) ** 0.25
    h = x.reshape(t, d)
    cr, sr, ca, s1, s2 = _rope_tables(positions.reshape(t, 1))
    inner_decay, q_decay, k_decay, chunk_decay = _retention_decays()
    bounds = [0]
    for size in IN_SIZES:
        bounds.append(bounds[-1] + size)
    for i in range(depth):
        g = ln_g[i][:, None, :]
        b = ln_b[i][:, None, :]
        w_in_b = w_in[i].astype(BF16)
        w_qk = w_in_b[:, bounds[0]:bounds[2]]
        w_vg = w_in_b[:, bounds[2]:bounds[4]]
        w_att = w_in_b[:, bounds[4]:bounds[7]]
        w_gates = w_in_b[:, bounds[7]:bounds[9]]

        h = _ffn_ln(h, w_ffn1_gu[i].astype(BF16), w_ffn1_down[i].astype(BF16), g[0], b[0], alpha=alpha)

        qkd = _proj(functools.partial(_proj_ret_qk_body, n_sub=2), h, w_qk, (cr, sr), (k_decay,),
                    n_out=3 * RET_QK, tm=1024, name="proj_ret_qk")
        vg = _proj(_proj_ret_vg_body, h, w_vg, name="proj_ret_vg")
        aqkv = _proj(functools.partial(_proj_att_body, n_sub=2), h, w_att, (ca, s1, s2),
                     tm=1024, name="proj_att")
        gates = _proj(_proj_gates_body, h, w_gates, name="proj_gates")

        ret = _retention(qkd, vg, inner_decay, q_decay, chunk_decay, ret_gn_g[i], batch, seq)
        att = _swa(aqkv, att_sinks[i], batch, seq)

        h = _mix_ln(h, ret, att, gates, w_ret_out[i].astype(BF16), w_att_out[i].astype(BF16),
                    w_mix_out[i].astype(BF16), g[1], b[1], alpha=alpha)
        h = _ffn_ln(h, w_ffn2_gu[i].astype(BF16), w_ffn2_down[i].astype(BF16), g[2], b[2], alpha=alpha)
        h = _ple_ln(h, p[i].reshape(t, PLE_DIM), w_ple_gate[i].astype(BF16),
                    w_ple_proj[i].astype(BF16), g[3], b[3], alpha=alpha)
    return h.reshape(batch, seq, d)
```

```python
import functools

import jax
import jax.numpy as jnp
from jax import lax
from jax.experimental import pallas as pl
from jax.experimental.pallas import tpu as pltpu

F32 = jnp.float32
BF16 = jnp.bfloat16

D_MODEL = 2048
PLE_DIM = 256
D_FF = 5632
RET_HEADS = 8
RET_DK = 128
RET_DV = 256
RET_CHUNK = 128
RET_THETA = 10000.0
ATT_HEADS = 16
ATT_KV_HEADS = 4
ATT_DH = 64
ATT_GROUP = ATT_HEADS // ATT_KV_HEADS
WINDOW = 128
ATT_BLOCK = 128
ROPE_THETA = 500000.0
ROPE_DIMS = ATT_DH // 4
LN_EPS = 1e-5
GN_EPS = 1e-5

RET_QK = RET_HEADS * RET_DK
RET_V = RET_HEADS * RET_DV
ATT_Q = ATT_HEADS * ATT_DH
ATT_KV = ATT_KV_HEADS * ATT_DH
IN_SIZES = (RET_QK, RET_QK, RET_V, RET_V, ATT_Q, ATT_KV, ATT_KV, D_MODEL, D_MODEL)

LANES = 128
VMEM_LIMIT_BYTES = 56 * 1024 * 1024


def _params(*semantics):
    return pltpu.CompilerParams(dimension_semantics=semantics, vmem_limit_bytes=VMEM_LIMIT_BYTES)


def _resident(shape, index_map):
    return pl.BlockSpec(shape, index_map, pipeline_mode=pl.Buffered(1))


def _layer_norm(y, g, b):
    mu = jnp.mean(y, axis=-1, keepdims=True)
    d = y - mu
    var = jnp.mean(d * d, axis=-1, keepdims=True)
    return d * lax.rsqrt(var + LN_EPS) * g + b


def _dot(a, b):
    return jnp.dot(a, b, preferred_element_type=F32)


def _ffn_body(x_ref, wg_ref, wu_ref, wd_ref, g_ref, b_ref, o_ref, xb_ref, *, alpha, n_sub):
    j = pl.program_id(1)
    sub = x_ref.shape[0] // n_sub

    @pl.when(j == 0)
    def _():
        xb_ref[...] = x_ref[...].astype(BF16)
        o_ref[...] = jnp.zeros_like(o_ref)

    for r in range(n_sub):
        rows = pl.ds(r * sub, sub)
        xb = xb_ref[rows, :]
        gate = _dot(xb, wg_ref[...])
        up = _dot(xb, wu_ref[...])
        act = (gate * jax.nn.sigmoid(gate) * up).astype(BF16)
        o_ref[rows, :] += _dot(act, wd_ref[...])

    @pl.when(j == pl.num_programs(1) - 1)
    def _():
        for r in range(n_sub):
            rows = pl.ds(r * sub, sub)
            y = alpha * x_ref[rows, :] + 0.5 * o_ref[rows, :]
            o_ref[rows, :] = _layer_norm(y, g_ref[...], b_ref[...])


def _ffn_ln(x, w_gu, w_down, g, b, *, alpha, tm=1024, tf=512, n_sub=2):
    t, d = x.shape
    f = w_down.shape[0]
    nj = f // tf
    return pl.pallas_call(
        functools.partial(_ffn_body, alpha=alpha, n_sub=n_sub),
        out_shape=jax.ShapeDtypeStruct((t, d), F32),
        grid=(t // tm, nj),
        in_specs=[
            pl.BlockSpec((tm, d), lambda i, j: (i, 0)),
            pl.BlockSpec((d, tf), lambda i, j: (0, j)),
            pl.BlockSpec((d, tf), lambda i, j: (0, j + nj)),
            pl.BlockSpec((tf, d), lambda i, j: (j, 0)),
            pl.BlockSpec((1, d), lambda i, j: (0, 0)),
            pl.BlockSpec((1, d), lambda i, j: (0, 0)),
        ],
        out_specs=pl.BlockSpec((tm, d), lambda i, j: (i, 0)),
        scratch_shapes=[pltpu.VMEM((tm, d), BF16)],
        compiler_params=_params("parallel", "arbitrary"),
        name="ffn_ln",
    )(x, w_gu, w_gu, w_down, g, b)


def _rope_body(pos_ref, inv_ref, cr_ref, sr_ref, ca_ref, s1_ref, s2_ref):
    pos = pos_ref[...].astype(F32)
    ang = pos * inv_ref[...]
    c = jnp.cos(ang)
    s = jnp.sin(ang)
    half_r = RET_DK // 2
    half_a = ROPE_DIMS // 2
    lane = lax.broadcasted_iota(jnp.int32, c.shape, 1)
    low = lane < half_r
    c_swap = pltpu.roll(c, half_r, axis=1)
    s_swap = pltpu.roll(s, half_r, axis=1)
    cr_ref[...] = jnp.where(low, c, c_swap)
    sr_ref[...] = jnp.where(low, -s, s_swap)
    c_att = jnp.where(low, c_swap, c)
    s_att = jnp.where(low, s_swap, s)
    in_head = lane & (ATT_DH - 1)
    first = in_head < half_a
    second = (in_head >= half_a) & (in_head < ROPE_DIMS)
    ca_ref[...] = jnp.where(first, c_att, jnp.where(second, pltpu.roll(c_att, half_a, axis=1), 1.0))
    s1_ref[...] = jnp.where(second, pltpu.roll(s_att, half_a, axis=1), 0.0)
    s2_ref[...] = jnp.where(first, -s_att, 0.0)


def _rope_tables(pos, tm=1024):
    t = pos.shape[0]
    half_r = RET_DK // 2
    half_a = ROPE_DIMS // 2
    inv_r = 1.0 / (RET_THETA ** (jnp.arange(half_r, dtype=F32) / half_r))
    inv_a = 1.0 / (ROPE_THETA ** (jnp.arange(half_a, dtype=F32) / half_a))
    inv = jnp.concatenate([inv_r, inv_a, jnp.zeros((LANES - half_r - half_a,), F32)])[None, :]
    tab = pl.BlockSpec((tm, LANES), lambda i: (i, 0))
    return pl.pallas_call(
        _rope_body,
        out_shape=[jax.ShapeDtypeStruct((t, LANES), F32)] * 5,
        grid=(t // tm,),
        in_specs=[pl.BlockSpec((tm, 1), lambda i: (i, 0)),
                  pl.BlockSpec((1, LANES), lambda i: (0, 0))],
        out_specs=[tab] * 5,
        compiler_params=_params("parallel"),
        name="rope_tables",
    )(pos, inv)


def _proj_ret_qk_body(x_ref, w_ref, cr_ref, sr_ref, kdec_ref, o_ref, *, n_sub):
    sub = x_ref.shape[0] // n_sub
    k_scale = RET_DK ** -0.5
    for r0 in range(0, x_ref.shape[0], sub):
        r = _dot(x_ref[r0:r0 + sub, :].astype(BF16), w_ref[...])
        cr = cr_ref[r0:r0 + sub, :]
        sr = sr_ref[r0:r0 + sub, :]
        for s in range(2 * RET_HEADS):
            x = r[:, s * LANES:(s + 1) * LANES]
            rot = x * cr + pltpu.roll(x, RET_DK // 2, axis=1) * sr
            if s < RET_HEADS:
                o_ref[r0:r0 + sub, s * LANES:(s + 1) * LANES] = rot.astype(BF16)
                continue
            rot = rot * k_scale
            o_ref[r0:r0 + sub, s * LANES:(s + 1) * LANES] = rot.astype(BF16)
            k_decay = kdec_ref[:, (s - RET_HEADS) * LANES:(s - RET_HEADS + 1) * LANES]
            d = s + RET_HEADS
            for c0 in range(0, sub, RET_CHUNK):
                o_ref[r0 + c0:r0 + c0 + RET_CHUNK, d * LANES:(d + 1) * LANES] = (
                    rot[c0:c0 + RET_CHUNK] * k_decay).astype(BF16)


def _proj_att_body(x_ref, w_ref, ca_ref, s1_ref, s2_ref, o_ref, *, n_sub):
    sub = x_ref.shape[0] // n_sub
    half = ROPE_DIMS // 2
    n_rot = (ATT_Q + ATT_KV) // LANES
    for r0 in range(0, x_ref.shape[0], sub):
        r = _dot(x_ref[r0:r0 + sub, :].astype(BF16), w_ref[...])
        ca = ca_ref[r0:r0 + sub, :]
        s1 = s1_ref[r0:r0 + sub, :]
        s2 = s2_ref[r0:r0 + sub, :]
        for s in range(n_rot):
            x = r[:, s * LANES:(s + 1) * LANES]
            rot = (x * ca + pltpu.roll(x, half, axis=1) * s1
                   + pltpu.roll(x, LANES - half, axis=1) * s2)
            o_ref[r0:r0 + sub, s * LANES:(s + 1) * LANES] = rot.astype(BF16)
        o_ref[r0:r0 + sub, n_rot * LANES:] = r[:, n_rot * LANES:].astype(BF16)


def _proj_plain_body(x_ref, w_ref, o_ref):
    o_ref[...] = _dot(x_ref[...].astype(BF16), w_ref[...]).astype(BF16)


def _proj(body, x, w, tables=(), consts=(), *, n_out=None, tm=512, name):
    t, d = x.shape
    n = w.shape[1]
    n_out = n if n_out is None else n_out
    tab = pl.BlockSpec((tm, LANES), lambda i: (i, 0))
    return pl.pallas_call(
        body,
        out_shape=jax.ShapeDtypeStruct((t, n_out), BF16),
        grid=(t // tm,),
        in_specs=[pl.BlockSpec((tm, d), lambda i: (i, 0)), _resident((d, n), lambda i: (0, 0))]
        + [tab] * len(tables) + [_resident(c.shape, lambda i: (0, 0)) for c in consts],
        out_specs=pl.BlockSpec((tm, n_out), lambda i: (i, 0)),
        compiler_params=_params("parallel"),
        name=name,
    )(x, w, *tables, *consts)


def _retention_body(q_ref, k_ref, kd_ref, v_ref, g_ref, inner_ref, qd_ref, cd_ref, gn_ref,
                    o_ref, state_ref, *, n_chunks):
    @pl.when(pl.program_id(2) == 0)
    def _():
        state_ref[...] = jnp.zeros_like(state_ref)

    inner_decay = inner_ref[...]
    q_decay = qd_ref[...]
    chunk_decay = cd_ref[...]
    gn = gn_ref[...]
    for c in range(n_chunks):
        rows = pl.ds(c * RET_CHUNK, RET_CHUNK)
        qn = q_ref[rows, :]
        vn = v_ref[rows, :]
        scores = lax.dot_general(qn, k_ref[rows, :], (((1,), (1,)), ((), ())),
                                 preferred_element_type=F32)
        inner = _dot((scores * inner_decay).astype(BF16), vn)
        state = state_ref[...]
        cross = _dot(qn, state.astype(BF16)) * q_decay
        state_ref[...] = state * chunk_decay + lax.dot_general(
            kd_ref[rows, :], vn, (((0,), (0,)), ((), ())), preferred_element_type=F32)
        o = inner + cross
        mu = jnp.mean(o, axis=-1, keepdims=True)
        d = o - mu
        var = jnp.mean(d * d, axis=-1, keepdims=True)
        y = d * lax.rsqrt(var + GN_EPS) * gn
        gate = g_ref[rows, :].astype(F32)
        o_ref[rows, :] = (gate * jax.nn.sigmoid(gate) * y).astype(BF16)


def _retention_decays():
    c = RET_CHUNK
    log_g = jnp.log1p(-jnp.exp2(-5.0 - jnp.arange(RET_HEADS, dtype=F32)))
    idx = jnp.arange(c, dtype=F32)
    q_decay = jnp.exp((idx[None, :] + 1.0) * log_g[:, None])
    k_decay = jnp.exp((c - 1.0 - idx)[None, :] * log_g[:, None])
    diff = idx[:, None] - idx[None, :]
    inner = jnp.where(diff[None] >= 0,
                      jnp.exp(jnp.maximum(diff, 0.0)[None] * log_g[:, None, None]), 0.0)
    chunk_decay = jnp.exp(c * log_g)
    q_decay = jnp.broadcast_to(q_decay[:, :, None], (RET_HEADS, c, RET_DV))
    k_decay = jnp.broadcast_to(k_decay.T[:, :, None], (c, RET_HEADS, RET_DK)).reshape(c, RET_QK)
    chunk_decay = jnp.broadcast_to(chunk_decay[:, None, None], (RET_HEADS, 1, RET_DV))
    return inner, q_decay, k_decay, chunk_decay


def _retention(qkd, vg, inner, q_decay, chunk_decay, gn_g, batch, seq, *, ts=4096):
    t = qkd.shape[0]
    ns = seq // ts
    gn = gn_g.reshape(RET_HEADS, 1, RET_DV)

    def rows(b, h, s):
        return b * ns + s

    head_tab = lambda shape: pl.BlockSpec((None,) + shape, lambda b, h, s: (h, 0, 0))
    return pl.pallas_call(
        functools.partial(_retention_body, n_chunks=ts // RET_CHUNK),
        out_shape=jax.ShapeDtypeStruct((t, RET_V), BF16),
        grid=(batch, RET_HEADS, ns),
        in_specs=[
            pl.BlockSpec((ts, RET_DK), lambda b, h, s: (rows(b, h, s), h)),
            pl.BlockSpec((ts, RET_DK), lambda b, h, s: (rows(b, h, s), RET_HEADS + h)),
            pl.BlockSpec((ts, RET_DK), lambda b, h, s: (rows(b, h, s), 2 * RET_HEADS + h)),
            pl.BlockSpec((ts, RET_DV), lambda b, h, s: (rows(b, h, s), h)),
            pl.BlockSpec((ts, RET_DV), lambda b, h, s: (rows(b, h, s), RET_HEADS + h)),
            head_tab((RET_CHUNK, RET_CHUNK)),
            head_tab((RET_CHUNK, RET_DV)),
            head_tab((1, RET_DV)),
            head_tab((1, RET_DV)),
        ],
        out_specs=pl.BlockSpec((ts, RET_DV), lambda b, h, s: (rows(b, h, s), h)),
        scratch_shapes=[pltpu.VMEM((RET_DK, RET_DV), F32)],
        compiler_params=_params("parallel", "parallel", "arbitrary"),
        name="retention",
    )(qkd, qkd, qkd, vg, vg, inner, q_decay, chunk_decay, gn)


def _swa_body(sink_ref, q_ref, kvc_ref, kvp_ref, o_ref):
    n = pl.program_id(1)
    c = ATT_BLOCK
    qi = lax.broadcasted_iota(jnp.int32, (c, 2 * c), 0) + c
    kj = lax.broadcasted_iota(jnp.int32, (c, 2 * c), 1)
    rel = qi - kj
    first_key = jnp.where(n > 0, 0, c)
    valid = (rel >= 0) & (rel < WINDOW) & (kj >= first_key)
    scale = ATT_DH ** -0.5

    def heads_of(hk):
        return [hk * ATT_GROUP + g for g in range(ATT_GROUP)]

    def band(hk, offset):
        cols = slice(offset + hk * ATT_DH, offset + (hk + 1) * ATT_DH)
        return jnp.concatenate([kvp_ref[:, cols], kvc_ref[:, cols]], axis=0)

    def scores(hk):
        q4 = jnp.concatenate([q_ref[:, hd * ATT_DH:(hd + 1) * ATT_DH] for hd in heads_of(hk)],
                             axis=0)
        return lax.dot_general(q4 * scale, band(hk, 0), (((1,), (1,)), ((), ())),
                               preferred_element_type=F32)

    def softmax(hk, s4):
        masked = [jnp.where(valid, s4[g * c:(g + 1) * c], -jnp.inf) for g in range(ATT_GROUP)]
        sinks = [sink_ref[hd] for hd in heads_of(hk)]
        maxes = [jnp.maximum(jnp.max(s, axis=-1, keepdims=True), sink)
                 for s, sink in zip(masked, sinks)]
        probs = [jnp.exp(s - m).astype(BF16) for s, m in zip(masked, maxes)]
        sink_terms = [jnp.exp(sink - m) for sink, m in zip(sinks, maxes)]
        return jnp.concatenate(probs, axis=0), sink_terms

    ones = jnp.ones((2 * c, 2 * c - ATT_DH), BF16)

    def weighted_values(hk, e4):
        return _dot(e4, jnp.concatenate([band(hk, ATT_KV), ones], axis=1))

    def store(hk, ov, sink_terms):
        groups = [ov[g * c:(g + 1) * c] for g in range(ATT_GROUP)]
        inv = [1.0 / (og[:, c:c + ATT_DH] + st) for og, st in zip(groups, sink_terms)]
        outs = [og[:, :ATT_DH] * r for og, r in zip(groups, inv)]
        for pair in range(ATT_GROUP // 2):
            col = heads_of(hk)[2 * pair] * ATT_DH
            o_ref[:, col:col + 2 * ATT_DH] = jnp.concatenate(
                [outs[2 * pair], outs[2 * pair + 1]], axis=1).astype(BF16)

    s_next = scores(0)
    pending = None
    for hk in range(ATT_KV_HEADS):
        s_cur = s_next
        if hk + 1 < ATT_KV_HEADS:
            s_next = scores(hk + 1)
        e4, sink_terms = softmax(hk, s_cur)
        if pending is not None:
            store(*pending)
        pending = (hk, weighted_values(hk, e4), sink_terms)
    store(*pending)


def _swa(aqkv, sinks, batch, seq):
    t = aqkv.shape[0]
    nb = seq // ATT_BLOCK
    kv_col = ATT_Q // (2 * ATT_KV)
    return pl.pallas_call(
        _swa_body,
        out_shape=jax.ShapeDtypeStruct((t, ATT_Q), BF16),
        grid=(batch, nb),
        in_specs=[
            pl.BlockSpec(memory_space=pltpu.SMEM),
            pl.BlockSpec((ATT_BLOCK, ATT_Q), lambda b, n: (b * nb + n, 0)),
            pl.BlockSpec((ATT_BLOCK, 2 * ATT_KV), lambda b, n: (b * nb + n, kv_col)),
            pl.BlockSpec((ATT_BLOCK, 2 * ATT_KV),
                         lambda b, n: (b * nb + jnp.maximum(n - 1, 0), kv_col)),
        ],
        out_specs=pl.BlockSpec((ATT_BLOCK, ATT_Q), lambda b, n: (b * nb + n, 0)),
        compiler_params=_params("parallel", "arbitrary"),
        name="swa",
    )(sinks, aqkv, aqkv, aqkv)


def _mix_body(h_ref, ret_ref, att_ref, gr_ref, ga_ref, wr_ref, wa_ref, wm_ref, g_ref, b_ref,
              o_ref, *, alpha):
    ret_branch = _dot(ret_ref[...], wr_ref[...])
    att_branch = _dot(att_ref[...], wa_ref[...])
    merged = (jax.nn.sigmoid(gr_ref[...].astype(F32)) * ret_branch
              + jax.nn.sigmoid(ga_ref[...].astype(F32)) * att_branch)
    mixed = _dot(merged.astype(BF16), wm_ref[...])
    o_ref[...] = _layer_norm(alpha * h_ref[...] + mixed, g_ref[...], b_ref[...])


def _mix_ln(h, ret, att, gates, w_ret, w_att, w_mix, g, b, *, alpha, tm=256):
    t, d = h.shape
    row = pl.BlockSpec((1, d), lambda i: (0, 0))
    return pl.pallas_call(
        functools.partial(_mix_body, alpha=alpha),
        out_shape=jax.ShapeDtypeStruct((t, d), F32),
        grid=(t // tm,),
        in_specs=[
            pl.BlockSpec((tm, d), lambda i: (i, 0)),
            pl.BlockSpec((tm, RET_V), lambda i: (i, 0)),
            pl.BlockSpec((tm, ATT_Q), lambda i: (i, 0)),
            pl.BlockSpec((tm, d), lambda i: (i, 0)),
            pl.BlockSpec((tm, d), lambda i: (i, 1)),
            _resident((RET_V, d), lambda i: (0, 0)),
            _resident((ATT_Q, d), lambda i: (0, 0)),
            _resident((d, d), lambda i: (0, 0)),
            row, row,
        ],
        out_specs=pl.BlockSpec((tm, d), lambda i: (i, 0)),
        compiler_params=_params("parallel"),
        name="mix_ln",
    )(h, ret, att, gates, gates, w_ret, w_att, w_mix, g, b)


def _ple_body(h_ref, p_ref, wg_ref, wp_ref, g_ref, b_ref, o_ref, *, alpha, n_sub):
    sub = h_ref.shape[0] // n_sub
    for r0 in range(0, h_ref.shape[0], sub):
        h = h_ref[r0:r0 + sub, :]
        gate = jax.nn.sigmoid(_dot(h.astype(BF16), wg_ref[...]))
        ple = gate * _dot(p_ref[r0:r0 + sub, :].astype(BF16), wp_ref[...])
        o_ref[r0:r0 + sub, :] = _layer_norm(alpha * h + ple, g_ref[...], b_ref[...])


def _ple_ln(h, p, w_gate, w_proj, g, b, *, alpha, tm=1024, n_sub=2):
    t, d = h.shape
    pd = p.shape[1]
    row = pl.BlockSpec((1, d), lambda i: (0, 0))
    return pl.pallas_call(
        functools.partial(_ple_body, alpha=alpha, n_sub=n_sub),
        out_shape=jax.ShapeDtypeStruct((t, d), F32),
        grid=(t // tm,),
        in_specs=[
            pl.BlockSpec((tm, d), lambda i: (i, 0)),
            pl.BlockSpec((tm, pd), lambda i: (i, 0)),
            _resident((d, d), lambda i: (0, 0)),
            _resident((pd, d), lambda i: (0, 0)),
            row, row,
        ],
        out_specs=pl.BlockSpec((tm, d), lambda i: (i, 0)),
        compiler_params=_params("parallel"),
        name="ple_ln",
    )(h, p, w_gate, w_proj, g, b)


def kernel(x, p, positions, ln_g, ln_b, w_ffn1_gu, w_ffn1_down, w_in, ret_gn_g, att_sinks,
           w_ret_out, w_att_out, w_mix_out, w_ffn2_gu, w_ffn2_down, w_ple_gate, w_ple_proj):
    batch, seq, d = x.shape
    depth = ln_g.shape[0]
    t = batch * seq
    alpha = (2.0 * depth) ** 0.25
    h = x.reshape(t, d)
    cr, sr, ca, s1, s2 = _rope_tables(positions.reshape(t, 1))
    inner_decay, q_decay, k_decay, chunk_decay = _retention_decays()
    bounds = [0]
    for size in IN_SIZES:
        bounds.append(bounds[-1] + size)
    for i in range(depth):
        g = ln_g[i][:, None, :]
        b = ln_b[i][:, None, :]
        w_in_b = w_in[i].astype(BF16)
        w_qk = w_in_b[:, bounds[0]:bounds[2]]
        w_vg = w_in_b[:, bounds[2]:bounds[4]]
        w_att = w_in_b[:, bounds[4]:bounds[7]]
        w_gates = w_in_b[:, bounds[7]:bounds[9]]

        h = _ffn_ln(h, w_ffn1_gu[i].astype(BF16), w_ffn1_down[i].astype(BF16), g[0], b[0], alpha=alpha)

        qkd = _proj(functools.partial(_proj_ret_qk_body, n_sub=2), h, w_qk, (cr, sr), (k_decay,),
                    n_out=3 * RET_QK, tm=1024, name="proj_ret_qk")
        vg = _proj(_proj_plain_body, h, w_vg, name="proj_ret_vg")
        aqkv = _proj(functools.partial(_proj_att_body, n_sub=2), h, w_att, (ca, s1, s2),
                     tm=1024, name="proj_att")
        gates = _proj(_proj_plain_body, h, w_gates, name="proj_gates")

        ret = _retention(qkd, vg, inner_decay, q_decay, chunk_decay, ret_gn_g[i], batch, seq)
        att = _swa(aqkv, att_sinks[i], batch, seq)

        h = _mix_ln(h, ret, att, gates, w_ret_out[i].astype(BF16), w_att_out[i].astype(BF16),
                    w_mix_out[i].astype(BF16), g[1], b[1], alpha=alpha)
        h = _ffn_ln(h, w_ffn2_gu[i].astype(BF16), w_ffn2_down[i].astype(BF16), g[2], b[2], alpha=alpha)
        h = _ple_ln(h, p[i].reshape(t, PLE_DIM), w_ple_gate[i].astype(BF16),
                    w_ple_proj[i].astype(BF16), g[3], b[3], alpha=alpha)
    return h.reshape(batch, seq, d)
```

```python
import functools

import jax
import jax.numpy as jnp
from jax import lax
from jax.experimental import pallas as pl
from jax.experimental.pallas import tpu as pltpu

F32 = jnp.float32
BF16 = jnp.bfloat16

D_MODEL = 2048
PLE_DIM = 256
D_FF = 5632
RET_HEADS = 8
RET_DK = 128
RET_DV = 256
RET_CHUNK = 128
RET_THETA = 10000.0
ATT_HEADS = 16
ATT_KV_HEADS = 4
ATT_DH = 64
ATT_GROUP = ATT_HEADS // ATT_KV_HEADS
WINDOW = 128
ATT_BLOCK = 128
ROPE_THETA = 500000.0
ROPE_DIMS = ATT_DH // 4
LN_EPS = 1e-5
GN_EPS = 1e-5

RET_QK = RET_HEADS * RET_DK
RET_V = RET_HEADS * RET_DV
ATT_Q = ATT_HEADS * ATT_DH
ATT_KV = ATT_KV_HEADS * ATT_DH
IN_SIZES = (RET_QK, RET_QK, RET_V, RET_V, ATT_Q, ATT_KV, ATT_KV, D_MODEL, D_MODEL)

LANES = 128
VMEM_LIMIT_BYTES = 56 * 1024 * 1024


def _params(*semantics):
    return pltpu.CompilerParams(dimension_semantics=semantics, vmem_limit_bytes=VMEM_LIMIT_BYTES)


def _resident(shape, index_map):
    return pl.BlockSpec(shape, index_map, pipeline_mode=pl.Buffered(1))


def _layer_norm(y, g, b):
    mu = jnp.mean(y, axis=-1, keepdims=True)
    d = y - mu
    var = jnp.mean(d * d, axis=-1, keepdims=True)
    return d * lax.rsqrt(var + LN_EPS) * g + b


def _dot(a, b):
    return jnp.dot(a, b, preferred_element_type=F32)


def _ffn_body(x_ref, wgu_hbm, wd_hbm, g_ref, b_ref, o_ref, xb_ref, wg_buf, wu_buf, wd_buf, sem,
              *, alpha, n_sub, tf):
    i = pl.program_id(0)
    sub = x_ref.shape[0] // n_sub
    nj = wd_hbm.shape[0] // tf

    def tile_copies(j, slot):
        return (
            pltpu.make_async_copy(wgu_hbm.at[:, pl.ds(j * tf, tf)], wg_buf.at[slot], sem.at[0, slot]),
            pltpu.make_async_copy(wgu_hbm.at[:, pl.ds((nj + j) * tf, tf)], wu_buf.at[slot],
                                  sem.at[1, slot]),
            pltpu.make_async_copy(wd_hbm.at[pl.ds(j * tf, tf), :], wd_buf.at[slot], sem.at[2, slot]),
        )

    @pl.when(i == 0)
    def _():
        for c in tile_copies(0, 0):
            c.start()

    xb_ref[...] = x_ref[...].astype(BF16)
    o_ref[...] = jnp.zeros_like(o_ref)

    for j in range(nj):
        slot = j % 2
        for c in tile_copies(j, slot):
            c.wait()
        if j + 1 < nj:
            for c in tile_copies(j + 1, 1 - slot):
                c.start()
        for r in range(n_sub):
            rows = pl.ds(r * sub, sub)
            xb = xb_ref[rows, :]
            gate = _dot(xb, wg_buf[slot])
            up = _dot(xb, wu_buf[slot])
            act = (gate * jax.nn.sigmoid(gate) * up).astype(BF16)
            o_ref[rows, :] += _dot(act, wd_buf[slot])

    @pl.when(i + 1 < pl.num_programs(0))
    def _():
        for c in tile_copies(0, 0):
            c.start()

    for r in range(n_sub):
        rows = pl.ds(r * sub, sub)
        y = alpha * x_ref[rows, :] + 0.5 * o_ref[rows, :]
        o_ref[rows, :] = _layer_norm(y, g_ref[...], b_ref[...])


def _ffn_ln(x, w_gu, w_down, g, b, *, alpha, tm=1024, tf=512, n_sub=2):
    t, d = x.shape
    return pl.pallas_call(
        functools.partial(_ffn_body, alpha=alpha, n_sub=n_sub, tf=tf),
        out_shape=jax.ShapeDtypeStruct((t, d), F32),
        grid=(t // tm,),
        in_specs=[
            pl.BlockSpec((tm, d), lambda i: (i, 0)),
            pl.BlockSpec(memory_space=pl.ANY),
            pl.BlockSpec(memory_space=pl.ANY),
            pl.BlockSpec((1, d), lambda i: (0, 0)),
            pl.BlockSpec((1, d), lambda i: (0, 0)),
        ],
        out_specs=pl.BlockSpec((tm, d), lambda i: (i, 0)),
        scratch_shapes=[
            pltpu.VMEM((tm, d), BF16),
            pltpu.VMEM((2, d, tf), BF16),
            pltpu.VMEM((2, d, tf), BF16),
            pltpu.VMEM((2, tf, d), BF16),
            pltpu.SemaphoreType.DMA((3, 2)),
        ],
        compiler_params=_params("arbitrary"),
        name="ffn_ln",
    )(x, w_gu, w_down, g, b)


def _rope_body(pos_ref, inv_ref, cr_ref, sr_ref, ca_ref, s1_ref, s2_ref):
    pos = pos_ref[...].astype(F32)
    ang = pos * inv_ref[...]
    c = jnp.cos(ang)
    s = jnp.sin(ang)
    half_r = RET_DK // 2
    half_a = ROPE_DIMS // 2
    lane = lax.broadcasted_iota(jnp.int32, c.shape, 1)
    low = lane < half_r
    c_swap = pltpu.roll(c, half_r, axis=1)
    s_swap = pltpu.roll(s, half_r, axis=1)
    cr_ref[...] = jnp.where(low, c, c_swap)
    sr_ref[...] = jnp.where(low, -s, s_swap)
    c_att = jnp.where(low, c_swap, c)
    s_att = jnp.where(low, s_swap, s)
    in_head = lane & (ATT_DH - 1)
    first = in_head < half_a
    second = (in_head >= half_a) & (in_head < ROPE_DIMS)
    ca_ref[...] = jnp.where(first, c_att, jnp.where(second, pltpu.roll(c_att, half_a, axis=1), 1.0))
    s1_ref[...] = jnp.where(second, pltpu.roll(s_att, half_a, axis=1), 0.0)
    s2_ref[...] = jnp.where(first, -s_att, 0.0)


def _rope_tables(pos, tm=1024):
    t = pos.shape[0]
    half_r = RET_DK // 2
    half_a = ROPE_DIMS // 2
    inv_r = 1.0 / (RET_THETA ** (jnp.arange(half_r, dtype=F32) / half_r))
    inv_a = 1.0 / (ROPE_THETA ** (jnp.arange(half_a, dtype=F32) / half_a))
    inv = jnp.concatenate([inv_r, inv_a, jnp.zeros((LANES - half_r - half_a,), F32)])[None, :]
    tab = pl.BlockSpec((tm, LANES), lambda i: (i, 0))
    return pl.pallas_call(
        _rope_body,
        out_shape=[jax.ShapeDtypeStruct((t, LANES), F32)] * 5,
        grid=(t // tm,),
        in_specs=[pl.BlockSpec((tm, 1), lambda i: (i, 0)),
                  pl.BlockSpec((1, LANES), lambda i: (0, 0))],
        out_specs=[tab] * 5,
        compiler_params=_params("parallel"),
        name="rope_tables",
    )(pos, inv)


def _proj_ret_qk_body(x_ref, w_ref, cr_ref, sr_ref, kdec_ref, o_ref, *, n_sub):
    sub = x_ref.shape[0] // n_sub
    k_scale = RET_DK ** -0.5
    for r0 in range(0, x_ref.shape[0], sub):
        r = _dot(x_ref[r0:r0 + sub, :].astype(BF16), w_ref[...])
        cr = cr_ref[r0:r0 + sub, :]
        sr = sr_ref[r0:r0 + sub, :]
        for s in range(2 * RET_HEADS):
            x = r[:, s * LANES:(s + 1) * LANES]
            rot = x * cr + pltpu.roll(x, RET_DK // 2, axis=1) * sr
            if s < RET_HEADS:
                o_ref[r0:r0 + sub, s * LANES:(s + 1) * LANES] = rot.astype(BF16)
                continue
            rot = rot * k_scale
            o_ref[r0:r0 + sub, s * LANES:(s + 1) * LANES] = rot.astype(BF16)
            k_decay = kdec_ref[:, (s - RET_HEADS) * LANES:(s - RET_HEADS + 1) * LANES]
            d = s + RET_HEADS
            for c0 in range(0, sub, RET_CHUNK):
                o_ref[r0 + c0:r0 + c0 + RET_CHUNK, d * LANES:(d + 1) * LANES] = (
                    rot[c0:c0 + RET_CHUNK] * k_decay).astype(BF16)


def _proj_att_body(x_ref, w_ref, ca_ref, s1_ref, s2_ref, o_ref, *, n_sub):
    sub = x_ref.shape[0] // n_sub
    half = ROPE_DIMS // 2
    n_rot = (ATT_Q + ATT_KV) // LANES
    for r0 in range(0, x_ref.shape[0], sub):
        r = _dot(x_ref[r0:r0 + sub, :].astype(BF16), w_ref[...])
        ca = ca_ref[r0:r0 + sub, :]
        s1 = s1_ref[r0:r0 + sub, :]
        s2 = s2_ref[r0:r0 + sub, :]
        for s in range(n_rot):
            x = r[:, s * LANES:(s + 1) * LANES]
            rot = (x * ca + pltpu.roll(x, half, axis=1) * s1
                   + pltpu.roll(x, LANES - half, axis=1) * s2)
            o_ref[r0:r0 + sub, s * LANES:(s + 1) * LANES] = rot.astype(BF16)
        o_ref[r0:r0 + sub, n_rot * LANES:] = r[:, n_rot * LANES:].astype(BF16)


def _proj_plain_body(x_ref, w_ref, o_ref):
    o_ref[...] = _dot(x_ref[...].astype(BF16), w_ref[...]).astype(BF16)


def _proj(body, x, w, tables=(), consts=(), *, n_out=None, tm=512, name):
    t, d = x.shape
    n = w.shape[1]
    n_out = n if n_out is None else n_out
    tab = pl.BlockSpec((tm, LANES), lambda i: (i, 0))
    return pl.pallas_call(
        body,
        out_shape=jax.ShapeDtypeStruct((t, n_out), BF16),
        grid=(t // tm,),
        in_specs=[pl.BlockSpec((tm, d), lambda i: (i, 0)), _resident((d, n), lambda i: (0, 0))]
        + [tab] * len(tables) + [_resident(c.shape, lambda i: (0, 0)) for c in consts],
        out_specs=pl.BlockSpec((tm, n_out), lambda i: (i, 0)),
        compiler_params=_params("parallel"),
        name=name,
    )(x, w, *tables, *consts)


def _retention_body(q_ref, k_ref, kd_ref, v_ref, g_ref, inner_ref, qd_ref, cd_ref, gn_ref,
                    o_ref, state_ref, *, n_chunks):
    @pl.when(pl.program_id(2) == 0)
    def _():
        state_ref[...] = jnp.zeros_like(state_ref)

    inner_decay = inner_ref[...]
    q_decay = qd_ref[...]
    chunk_decay = cd_ref[...]
    gn = gn_ref[...]
    for c in range(n_chunks):
        rows = pl.ds(c * RET_CHUNK, RET_CHUNK)
        qn = q_ref[rows, :]
        vn = v_ref[rows, :]
        scores = lax.dot_general(qn, k_ref[rows, :], (((1,), (1,)), ((), ())),
                                 preferred_element_type=F32)
        inner = _dot((scores * inner_decay).astype(BF16), vn)
        state = state_ref[...]
        cross = _dot(qn, state.astype(BF16)) * q_decay
        state_ref[...] = state * chunk_decay + lax.dot_general(
            kd_ref[rows, :], vn, (((0,), (0,)), ((), ())), preferred_element_type=F32)
        o = inner + cross
        mu = jnp.mean(o, axis=-1, keepdims=True)
        d = o - mu
        var = jnp.mean(d * d, axis=-1, keepdims=True)
        y = d * lax.rsqrt(var + GN_EPS) * gn
        gate = g_ref[rows, :].astype(F32)
        o_ref[rows, :] = (gate * jax.nn.sigmoid(gate) * y).astype(BF16)


def _retention_decays():
    c = RET_CHUNK
    log_g = jnp.log1p(-jnp.exp2(-5.0 - jnp.arange(RET_HEADS, dtype=F32)))
    idx = jnp.arange(c, dtype=F32)
    q_decay = jnp.exp((idx[None, :] + 1.0) * log_g[:, None])
    k_decay = jnp.exp((c - 1.0 - idx)[None, :] * log_g[:, None])
    diff = idx[:, None] - idx[None, :]
    inner = jnp.where(diff[None] >= 0,
                      jnp.exp(jnp.maximum(diff, 0.0)[None] * log_g[:, None, None]), 0.0)
    chunk_decay = jnp.exp(c * log_g)
    q_decay = jnp.broadcast_to(q_decay[:, :, None], (RET_HEADS, c, RET_DV))
    k_decay = jnp.broadcast_to(k_decay.T[:, :, None], (c, RET_HEADS, RET_DK)).reshape(c, RET_QK)
    chunk_decay = jnp.broadcast_to(chunk_decay[:, None, None], (RET_HEADS, 1, RET_DV))
    return inner, q_decay, k_decay, chunk_decay


def _retention(qkd, vg, inner, q_decay, chunk_decay, gn_g, batch, seq, *, ts=4096):
    t = qkd.shape[0]
    ns = seq // ts
    gn = gn_g.reshape(RET_HEADS, 1, RET_DV)

    def rows(b, h, s):
        return b * ns + s

    head_tab = lambda shape: pl.BlockSpec((None,) + shape, lambda b, h, s: (h, 0, 0))
    return pl.pallas_call(
        functools.partial(_retention_body, n_chunks=ts // RET_CHUNK),
        out_shape=jax.ShapeDtypeStruct((t, RET_V), BF16),
        grid=(batch, RET_HEADS, ns),
        in_specs=[
            pl.BlockSpec((ts, RET_DK), lambda b, h, s: (rows(b, h, s), h)),
            pl.BlockSpec((ts, RET_DK), lambda b, h, s: (rows(b, h, s), RET_HEADS + h)),
            pl.BlockSpec((ts, RET_DK), lambda b, h, s: (rows(b, h, s), 2 * RET_HEADS + h)),
            pl.BlockSpec((ts, RET_DV), lambda b, h, s: (rows(b, h, s), h)),
            pl.BlockSpec((ts, RET_DV), lambda b, h, s: (rows(b, h, s), RET_HEADS + h)),
            head_tab((RET_CHUNK, RET_CHUNK)),
            head_tab((RET_CHUNK, RET_DV)),
            head_tab((1, RET_DV)),
            head_tab((1, RET_DV)),
        ],
        out_specs=pl.BlockSpec((ts, RET_DV), lambda b, h, s: (rows(b, h, s), h)),
        scratch_shapes=[pltpu.VMEM((RET_DK, RET_DV), F32)],
        compiler_params=_params("parallel", "parallel", "arbitrary"),
        name="retention",
    )(qkd, qkd, qkd, vg, vg, inner, q_decay, chunk_decay, gn)


def _swa_body(sink_ref, q_ref, kvc_ref, kvp_ref, o_ref):
    n = pl.program_id(1)
    c = ATT_BLOCK
    qi = lax.broadcasted_iota(jnp.int32, (c, 2 * c), 0) + c
    kj = lax.broadcasted_iota(jnp.int32, (c, 2 * c), 1)
    rel = qi - kj
    first_key = jnp.where(n > 0, 0, c)
    valid = (rel >= 0) & (rel < WINDOW) & (kj >= first_key)
    scale = ATT_DH ** -0.5

    def heads_of(hk):
        return [hk * ATT_GROUP + g for g in range(ATT_GROUP)]

    def band(hk, offset):
        cols = slice(offset + hk * ATT_DH, offset + (hk + 1) * ATT_DH)
        return jnp.concatenate([kvp_ref[:, cols], kvc_ref[:, cols]], axis=0)

    def scores(hk):
        q4 = jnp.concatenate([q_ref[:, hd * ATT_DH:(hd + 1) * ATT_DH] for hd in heads_of(hk)],
                             axis=0)
        return lax.dot_general(q4 * scale, band(hk, 0), (((1,), (1,)), ((), ())),
                               preferred_element_type=F32)

    def softmax(hk, s4):
        masked = [jnp.where(valid, s4[g * c:(g + 1) * c], -jnp.inf) for g in range(ATT_GROUP)]
        sinks = [sink_ref[hd] for hd in heads_of(hk)]
        maxes = [jnp.maximum(jnp.max(s, axis=-1, keepdims=True), sink)
                 for s, sink in zip(masked, sinks)]
        probs = [jnp.exp(s - m).astype(BF16) for s, m in zip(masked, maxes)]
        sink_terms = [jnp.exp(sink - m) for sink, m in zip(sinks, maxes)]
        return jnp.concatenate(probs, axis=0), sink_terms

    ones = jnp.ones((2 * c, 2 * c - ATT_DH), BF16)

    def weighted_values(hk, e4):
        return _dot(e4, jnp.concatenate([band(hk, ATT_KV), ones], axis=1))

    def store(hk, ov, sink_terms):
        groups = [ov[g * c:(g + 1) * c] for g in range(ATT_GROUP)]
        inv = [1.0 / (og[:, c:c + ATT_DH] + st) for og, st in zip(groups, sink_terms)]
        outs = [og[:, :ATT_DH] * r for og, r in zip(groups, inv)]
        for pair in range(ATT_GROUP // 2):
            col = heads_of(hk)[2 * pair] * ATT_DH
            o_ref[:, col:col + 2 * ATT_DH] = jnp.concatenate(
                [outs[2 * pair], outs[2 * pair + 1]], axis=1).astype(BF16)

    s_next = scores(0)
    pending = None
    for hk in range(ATT_KV_HEADS):
        s_cur = s_next
        if hk + 1 < ATT_KV_HEADS:
            s_next = scores(hk + 1)
        e4, sink_terms = softmax(hk, s_cur)
        if pending is not None:
            store(*pending)
        pending = (hk, weighted_values(hk, e4), sink_terms)
    store(*pending)


def _swa(aqkv, sinks, batch, seq):
    t = aqkv.shape[0]
    nb = seq // ATT_BLOCK
    kv_col = ATT_Q // (2 * ATT_KV)
    return pl.pallas_call(
        _swa_body,
        out_shape=jax.ShapeDtypeStruct((t, ATT_Q), BF16),
        grid=(batch, nb),
        in_specs=[
            pl.BlockSpec(memory_space=pltpu.SMEM),
            pl.BlockSpec((ATT_BLOCK, ATT_Q), lambda b, n: (b * nb + n, 0)),
            pl.BlockSpec((ATT_BLOCK, 2 * ATT_KV), lambda b, n: (b * nb + n, kv_col)),
            pl.BlockSpec((ATT_BLOCK, 2 * ATT_KV),
                         lambda b, n: (b * nb + jnp.maximum(n - 1, 0), kv_col)),
        ],
        out_specs=pl.BlockSpec((ATT_BLOCK, ATT_Q), lambda b, n: (b * nb + n, 0)),
        compiler_params=_params("parallel", "arbitrary"),
        name="swa",
    )(sinks, aqkv, aqkv, aqkv)


def _mix_body(h_ref, ret_ref, att_ref, gr_ref, ga_ref, wr_ref, wa_ref, wm_ref, g_ref, b_ref,
              o_ref, *, alpha):
    ret_branch = _dot(ret_ref[...], wr_ref[...])
    att_branch = _dot(att_ref[...], wa_ref[...])
    merged = (jax.nn.sigmoid(gr_ref[...].astype(F32)) * ret_branch
              + jax.nn.sigmoid(ga_ref[...].astype(F32)) * att_branch)
    mixed = _dot(merged.astype(BF16), wm_ref[...])
    o_ref[...] = _layer_norm(alpha * h_ref[...] + mixed, g_ref[...], b_ref[...])


def _mix_ln(h, ret, att, gates, w_ret, w_att, w_mix, g, b, *, alpha, tm=256):
    t, d = h.shape
    row = pl.BlockSpec((1, d), lambda i: (0, 0))
    return pl.pallas_call(
        functools.partial(_mix_body, alpha=alpha),
        out_shape=jax.ShapeDtypeStruct((t, d), F32),
        grid=(t // tm,),
        in_specs=[
            pl.BlockSpec((tm, d), lambda i: (i, 0)),
            pl.BlockSpec((tm, RET_V), lambda i: (i, 0)),
            pl.BlockSpec((tm, ATT_Q), lambda i: (i, 0)),
            pl.BlockSpec((tm, d), lambda i: (i, 0)),
            pl.BlockSpec((tm, d), lambda i: (i, 1)),
            _resident((RET_V, d), lambda i: (0, 0)),
            _resident((ATT_Q, d), lambda i: (0, 0)),
            _resident((d, d), lambda i: (0, 0)),
            row, row,
        ],
        out_specs=pl.BlockSpec((tm, d), lambda i: (i, 0)),
        compiler_params=_params("parallel"),
        name="mix_ln",
    )(h, ret, att, gates, gates, w_ret, w_att, w_mix, g, b)


def _ple_body(h_ref, p_ref, wg_ref, wp_ref, g_ref, b_ref, o_ref, *, alpha, n_sub):
    sub = h_ref.shape[0] // n_sub
    for r0 in range(0, h_ref.shape[0], sub):
        h = h_ref[r0:r0 + sub, :]
        gate = jax.nn.sigmoid(_dot(h.astype(BF16), wg_ref[...]))
        ple = gate * _dot(p_ref[r0:r0 + sub, :].astype(BF16), wp_ref[...])
        o_ref[r0:r0 + sub, :] = _layer_norm(alpha * h + ple, g_ref[...], b_ref[...])


def _ple_ln(h, p, w_gate, w_proj, g, b, *, alpha, tm=1024, n_sub=2):
    t, d = h.shape
    pd = p.shape[1]
    row = pl.BlockSpec((1, d), lambda i: (0, 0))
    return pl.pallas_call(
        functools.partial(_ple_body, alpha=alpha, n_sub=n_sub),
        out_shape=jax.ShapeDtypeStruct((t, d), F32),
        grid=(t // tm,),
        in_specs=[
            pl.BlockSpec((tm, d), lambda i: (i, 0)),
            pl.BlockSpec((tm, pd), lambda i: (i, 0)),
            _resident((d, d), lambda i: (0, 0)),
            _resident((pd, d), lambda i: (0, 0)),
            row, row,
        ],
        out_specs=pl.BlockSpec((tm, d), lambda i: (i, 0)),
        compiler_params=_params("parallel"),
        name="ple_ln",
    )(h, p, w_gate, w_proj, g, b)


def kernel(x, p, positions, ln_g, ln_b, w_ffn1_gu, w_ffn1_down, w_in, ret_gn_g, att_sinks,
           w_ret_out, w_att_out, w_mix_out, w_ffn2_gu, w_ffn2_down, w_ple_gate, w_ple_proj):
    batch, seq, d = x.shape
    depth = ln_g.shape[0]
    t = batch * seq
    alpha = (2.0 * depth) ** 0.25
    h = x.reshape(t, d)
    cr, sr, ca, s1, s2 = _rope_tables(positions.reshape(t, 1))
    inner_decay, q_decay, k_decay, chunk_decay = _retention_decays()
    bounds = [0]
    for size in IN_SIZES:
        bounds.append(bounds[-1] + size)
    for i in range(depth):
        g = ln_g[i][:, None, :]
        b = ln_b[i][:, None, :]
        w_in_b = w_in[i].astype(BF16)
        w_qk = w_in_b[:, bounds[0]:bounds[2]]
        w_vg = w_in_b[:, bounds[2]:bounds[4]]
        w_att = w_in_b[:, bounds[4]:bounds[7]]
        w_gates = w_in_b[:, bounds[7]:bounds[9]]

        h = _ffn_ln(h, w_ffn1_gu[i].astype(BF16), w_ffn1_down[i].astype(BF16), g[0], b[0], alpha=alpha)

        qkd = _proj(functools.partial(_proj_ret_qk_body, n_sub=2), h, w_qk, (cr, sr), (k_decay,),
                    n_out=3 * RET_QK, tm=1024, name="proj_ret_qk")
        vg = _proj(_proj_plain_body, h, w_vg, name="proj_ret_vg")
        aqkv = _proj(functools.partial(_proj_att_body, n_sub=2), h, w_att, (ca, s1, s2),
                     tm=1024, name="proj_att")
        gates = _proj(_proj_plain_body, h, w_gates, name="proj_gates")

        ret = _retention(qkd, vg, inner_decay, q_decay, chunk_decay, ret_gn_g[i], batch, seq)
        att = _swa(aqkv, att_sinks[i], batch, seq)

        h = _mix_ln(h, ret, att, gates, w_ret_out[i].astype(BF16), w_att_out[i].astype(BF16),
                    w_mix_out[i].astype(BF16), g[1], b[1], alpha=alpha)
        h = _ffn_ln(h, w_ffn2_gu[i].astype(BF16), w_ffn2_down[i].astype(BF16), g[2], b[2], alpha=alpha)
        h = _ple_ln(h, p[i].reshape(t, PLE_DIM), w_ple_gate[i].astype(BF16),
                    w_ple_proj[i].astype(BF16), g[3], b[3], alpha=alpha)
    return h.reshape(batch, seq, d)
```

```python
import functools

import jax
import jax.numpy as jnp
from jax import lax
from jax.experimental import pallas as pl
from jax.experimental.pallas import tpu as pltpu

F32 = jnp.float32
BF16 = jnp.bfloat16

D_MODEL = 2048
PLE_DIM = 256
D_FF = 5632
RET_HEADS = 8
RET_DK = 128
RET_DV = 256
RET_CHUNK = 128
RET_THETA = 10000.0
ATT_HEADS = 16
ATT_KV_HEADS = 4
ATT_DH = 64
ATT_GROUP = ATT_HEADS // ATT_KV_HEADS
WINDOW = 128
ATT_BLOCK = 128
ROPE_THETA = 500000.0
ROPE_DIMS = ATT_DH // 4
LN_EPS = 1e-5
GN_EPS = 1e-5

RET_QK = RET_HEADS * RET_DK
RET_V = RET_HEADS * RET_DV
ATT_Q = ATT_HEADS * ATT_DH
ATT_KV = ATT_KV_HEADS * ATT_DH
IN_SIZES = (RET_QK, RET_QK, RET_V, RET_V, ATT_Q, ATT_KV, ATT_KV, D_MODEL, D_MODEL)

LANES = 128
VMEM_LIMIT_BYTES = 60 * 1024 * 1024


def _params(*semantics):
    return pltpu.CompilerParams(dimension_semantics=semantics, vmem_limit_bytes=VMEM_LIMIT_BYTES)


def _resident(shape, index_map):
    return pl.BlockSpec(shape, index_map, pipeline_mode=pl.Buffered(1))


def _layer_norm(y, g, b):
    mu = jnp.mean(y, axis=-1, keepdims=True)
    d = y - mu
    var = jnp.mean(d * d, axis=-1, keepdims=True)
    return d * lax.rsqrt(var + LN_EPS) * g + b


def _dot(a, b):
    return jnp.dot(a, b, preferred_element_type=F32)


def _ffn_body(x_ref, wg_ref, wu_ref, wd_ref, g_ref, b_ref, o_ref, xb_ref, *, alpha, n_sub):
    j = pl.program_id(1)
    sub = x_ref.shape[0] // n_sub

    @pl.when(j == 0)
    def _():
        xb_ref[...] = x_ref[...].astype(BF16)
        o_ref[...] = jnp.zeros_like(o_ref)

    for r in range(n_sub):
        rows = pl.ds(r * sub, sub)
        xb = xb_ref[rows, :]
        gate = _dot(xb, wg_ref[...])
        up = _dot(xb, wu_ref[...])
        act = (gate * jax.nn.sigmoid(gate) * up).astype(BF16)
        o_ref[rows, :] += _dot(act, wd_ref[...])

    @pl.when(j == pl.num_programs(1) - 1)
    def _():
        for r in range(n_sub):
            rows = pl.ds(r * sub, sub)
            y = alpha * x_ref[rows, :] + 0.5 * o_ref[rows, :]
            o_ref[rows, :] = _layer_norm(y, g_ref[...], b_ref[...])


def _ffn_ln(x, w_gu, w_down, g, b, *, alpha, tm=1024, tf=512, n_sub=2):
    t, d = x.shape
    f = w_down.shape[0]
    nj = f // tf
    return pl.pallas_call(
        functools.partial(_ffn_body, alpha=alpha, n_sub=n_sub),
        out_shape=jax.ShapeDtypeStruct((t, d), F32),
        grid=(t // tm, nj),
        in_specs=[
            pl.BlockSpec((tm, d), lambda i, j: (i, 0)),
            pl.BlockSpec((d, tf), lambda i, j: (0, j)),
            pl.BlockSpec((d, tf), lambda i, j: (0, j + nj)),
            pl.BlockSpec((tf, d), lambda i, j: (j, 0)),
            pl.BlockSpec((1, d), lambda i, j: (0, 0)),
            pl.BlockSpec((1, d), lambda i, j: (0, 0)),
        ],
        out_specs=pl.BlockSpec((tm, d), lambda i, j: (i, 0)),
        scratch_shapes=[pltpu.VMEM((tm, d), BF16)],
        compiler_params=_params("parallel", "arbitrary"),
        name="ffn_ln",
    )(x, w_gu, w_gu, w_down, g, b)


def _rope_body(pos_ref, inv_ref, cr_ref, sr_ref, ca_ref, s1_ref, s2_ref):
    pos = pos_ref[...].astype(F32)
    ang = pos * inv_ref[...]
    c = jnp.cos(ang)
    s = jnp.sin(ang)
    half_r = RET_DK // 2
    half_a = ROPE_DIMS // 2
    lane = lax.broadcasted_iota(jnp.int32, c.shape, 1)
    low = lane < half_r
    c_swap = pltpu.roll(c, half_r, axis=1)
    s_swap = pltpu.roll(s, half_r, axis=1)
    cr_ref[...] = jnp.where(low, c, c_swap)
    sr_ref[...] = jnp.where(low, -s, s_swap)
    c_att = jnp.where(low, c_swap, c)
    s_att = jnp.where(low, s_swap, s)
    in_head = lane & (ATT_DH - 1)
    first = in_head < half_a
    second = (in_head >= half_a) & (in_head < ROPE_DIMS)
    ca_ref[...] = jnp.where(first, c_att, jnp.where(second, pltpu.roll(c_att, half_a, axis=1), 1.0))
    s1_ref[...] = jnp.where(second, pltpu.roll(s_att, half_a, axis=1), 0.0)
    s2_ref[...] = jnp.where(first, -s_att, 0.0)


def _rope_tables(pos, tm=1024):
    t = pos.shape[0]
    half_r = RET_DK // 2
    half_a = ROPE_DIMS // 2
    inv_r = 1.0 / (RET_THETA ** (jnp.arange(half_r, dtype=F32) / half_r))
    inv_a = 1.0 / (ROPE_THETA ** (jnp.arange(half_a, dtype=F32) / half_a))
    inv = jnp.concatenate([inv_r, inv_a, jnp.zeros((LANES - half_r - half_a,), F32)])[None, :]
    tab = pl.BlockSpec((tm, LANES), lambda i: (i, 0))
    return pl.pallas_call(
        _rope_body,
        out_shape=[jax.ShapeDtypeStruct((t, LANES), F32)] * 5,
        grid=(t // tm,),
        in_specs=[pl.BlockSpec((tm, 1), lambda i: (i, 0)),
                  pl.BlockSpec((1, LANES), lambda i: (0, 0))],
        out_specs=[tab] * 5,
        compiler_params=_params("parallel"),
        name="rope_tables",
    )(pos, inv)


def _proj_ret_qk_body(x_ref, w_ref, cr_ref, sr_ref, kdec_ref, o_ref, *, n_sub):
    sub = x_ref.shape[0] // n_sub
    k_scale = RET_DK ** -0.5
    for r0 in range(0, x_ref.shape[0], sub):
        r = _dot(x_ref[r0:r0 + sub, :].astype(BF16), w_ref[...])
        cr = cr_ref[r0:r0 + sub, :]
        sr = sr_ref[r0:r0 + sub, :]
        for s in range(2 * RET_HEADS):
            x = r[:, s * LANES:(s + 1) * LANES]
            rot = x * cr + pltpu.roll(x, RET_DK // 2, axis=1) * sr
            if s < RET_HEADS:
                o_ref[r0:r0 + sub, s * LANES:(s + 1) * LANES] = rot.astype(BF16)
                continue
            rot = rot * k_scale
            o_ref[r0:r0 + sub, s * LANES:(s + 1) * LANES] = rot.astype(BF16)
            k_decay = kdec_ref[:, (s - RET_HEADS) * LANES:(s - RET_HEADS + 1) * LANES]
            d = s + RET_HEADS
            for c0 in range(0, sub, RET_CHUNK):
                o_ref[r0 + c0:r0 + c0 + RET_CHUNK, d * LANES:(d + 1) * LANES] = (
                    rot[c0:c0 + RET_CHUNK] * k_decay).astype(BF16)


def _proj_att_body(x_ref, w_ref, ca_ref, s1_ref, s2_ref, o_ref, *, n_sub):
    sub = x_ref.shape[0] // n_sub
    half = ROPE_DIMS // 2
    n_rot = (ATT_Q + ATT_KV) // LANES
    for r0 in range(0, x_ref.shape[0], sub):
        r = _dot(x_ref[r0:r0 + sub, :].astype(BF16), w_ref[...])
        ca = ca_ref[r0:r0 + sub, :]
        s1 = s1_ref[r0:r0 + sub, :]
        s2 = s2_ref[r0:r0 + sub, :]
        for s in range(n_rot):
            x = r[:, s * LANES:(s + 1) * LANES]
            rot = (x * ca + pltpu.roll(x, half, axis=1) * s1
                   + pltpu.roll(x, LANES - half, axis=1) * s2)
            o_ref[r0:r0 + sub, s * LANES:(s + 1) * LANES] = rot.astype(BF16)
        o_ref[r0:r0 + sub, n_rot * LANES:] = r[:, n_rot * LANES:].astype(BF16)


def _proj_plain_body(x_ref, w_ref, o_ref):
    o_ref[...] = _dot(x_ref[...].astype(BF16), w_ref[...]).astype(BF16)


def _proj(body, x, w, tables=(), consts=(), *, n_out=None, tm=512, name):
    t, d = x.shape
    n = w.shape[1]
    n_out = n if n_out is None else n_out
    tab = pl.BlockSpec((tm, LANES), lambda i: (i, 0))
    return pl.pallas_call(
        body,
        out_shape=jax.ShapeDtypeStruct((t, n_out), BF16),
        grid=(t // tm,),
        in_specs=[pl.BlockSpec((tm, d), lambda i: (i, 0)), _resident((d, n), lambda i: (0, 0))]
        + [tab] * len(tables) + [_resident(c.shape, lambda i: (0, 0)) for c in consts],
        out_specs=pl.BlockSpec((tm, n_out), lambda i: (i, 0)),
        compiler_params=_params("parallel"),
        name=name,
    )(x, w, *tables, *consts)


def _retention_body(q_ref, k_ref, kd_ref, v_ref, g_ref, inner_ref, qd_ref, cd_ref, gn_ref,
                    o_ref, state_ref, *, n_chunks):
    @pl.when(pl.program_id(2) == 0)
    def _():
        state_ref[...] = jnp.zeros_like(state_ref)

    inner_decay = inner_ref[...]
    q_decay = qd_ref[...]
    chunk_decay = cd_ref[...]
    gn = gn_ref[...]
    for c in range(n_chunks):
        rows = pl.ds(c * RET_CHUNK, RET_CHUNK)
        qn = q_ref[rows, :]
        vn = v_ref[rows, :]
        scores = lax.dot_general(qn, k_ref[rows, :], (((1,), (1,)), ((), ())),
                                 preferred_element_type=F32)
        inner = _dot((scores * inner_decay).astype(BF16), vn)
        state = state_ref[...]
        cross = _dot(qn, state.astype(BF16)) * q_decay
        state_ref[...] = state * chunk_decay + lax.dot_general(
            kd_ref[rows, :], vn, (((0,), (0,)), ((), ())), preferred_element_type=F32)
        o = inner + cross
        mu = jnp.mean(o, axis=-1, keepdims=True)
        d = o - mu
        var = jnp.mean(d * d, axis=-1, keepdims=True)
        y = d * lax.rsqrt(var + GN_EPS) * gn
        gate = g_ref[rows, :].astype(F32)
        o_ref[rows, :] = (gate * jax.nn.sigmoid(gate) * y).astype(BF16)


def _retention_decays():
    c = RET_CHUNK
    log_g = jnp.log1p(-jnp.exp2(-5.0 - jnp.arange(RET_HEADS, dtype=F32)))
    idx = jnp.arange(c, dtype=F32)
    q_decay = jnp.exp((idx[None, :] + 1.0) * log_g[:, None])
    k_decay = jnp.exp((c - 1.0 - idx)[None, :] * log_g[:, None])
    diff = idx[:, None] - idx[None, :]
    inner = jnp.where(diff[None] >= 0,
                      jnp.exp(jnp.maximum(diff, 0.0)[None] * log_g[:, None, None]), 0.0)
    chunk_decay = jnp.exp(c * log_g)
    q_decay = jnp.broadcast_to(q_decay[:, :, None], (RET_HEADS, c, RET_DV))
    k_decay = jnp.broadcast_to(k_decay.T[:, :, None], (c, RET_HEADS, RET_DK)).reshape(c, RET_QK)
    chunk_decay = jnp.broadcast_to(chunk_decay[:, None, None], (RET_HEADS, 1, RET_DV))
    return inner, q_decay, k_decay, chunk_decay


def _retention(qkd, vg, inner, q_decay, chunk_decay, gn_g, batch, seq, *, ts=4096):
    t = qkd.shape[0]
    ns = seq // ts
    gn = gn_g.reshape(RET_HEADS, 1, RET_DV)

    def rows(b, h, s):
        return b * ns + s

    head_tab = lambda shape: pl.BlockSpec((None,) + shape, lambda b, h, s: (h, 0, 0))
    return pl.pallas_call(
        functools.partial(_retention_body, n_chunks=ts // RET_CHUNK),
        out_shape=jax.ShapeDtypeStruct((t, RET_V), BF16),
        grid=(batch, RET_HEADS, ns),
        in_specs=[
            pl.BlockSpec((ts, RET_DK), lambda b, h, s: (rows(b, h, s), h)),
            pl.BlockSpec((ts, RET_DK), lambda b, h, s: (rows(b, h, s), RET_HEADS + h)),
            pl.BlockSpec((ts, RET_DK), lambda b, h, s: (rows(b, h, s), 2 * RET_HEADS + h)),
            pl.BlockSpec((ts, RET_DV), lambda b, h, s: (rows(b, h, s), h)),
            pl.BlockSpec((ts, RET_DV), lambda b, h, s: (rows(b, h, s), RET_HEADS + h)),
            head_tab((RET_CHUNK, RET_CHUNK)),
            head_tab((RET_CHUNK, RET_DV)),
            head_tab((1, RET_DV)),
            head_tab((1, RET_DV)),
        ],
        out_specs=pl.BlockSpec((ts, RET_DV), lambda b, h, s: (rows(b, h, s), h)),
        scratch_shapes=[pltpu.VMEM((RET_DK, RET_DV), F32)],
        compiler_params=_params("parallel", "parallel", "arbitrary"),
        name="retention",
    )(qkd, qkd, qkd, vg, vg, inner, q_decay, chunk_decay, gn)


def _swa_body(sink_ref, q_ref, kvc_ref, kvp_ref, o_ref, *, n_blocks):
    n = pl.program_id(1)
    c = ATT_BLOCK
    qi = lax.broadcasted_iota(jnp.int32, (c, 2 * c), 0) + c
    kj = lax.broadcasted_iota(jnp.int32, (c, 2 * c), 1)
    rel = qi - kj
    in_window = (rel >= 0) & (rel < WINDOW)
    valid_first = in_window & (kj >= jnp.where(n > 0, 0, c))
    scale = ATT_DH ** -0.5

    def heads_of(hk):
        return [hk * ATT_GROUP + g for g in range(ATT_GROUP)]

    def band(qb, hk, offset):
        cols = slice(offset + hk * ATT_DH, offset + (hk + 1) * ATT_DH)
        prev = kvp_ref[:, cols] if qb == 0 else kvc_ref[(qb - 1) * c:qb * c, cols]
        return jnp.concatenate([prev, kvc_ref[qb * c:(qb + 1) * c, cols]], axis=0)

    def scores(qb, hk):
        q4 = jnp.concatenate([q_ref[qb * c:(qb + 1) * c, hd * ATT_DH:(hd + 1) * ATT_DH]
                              for hd in heads_of(hk)], axis=0)
        return lax.dot_general(q4 * scale, band(qb, hk, 0), (((1,), (1,)), ((), ())),
                               preferred_element_type=F32)

    def softmax(qb, hk, s4):
        valid = valid_first if qb == 0 else in_window
        masked = [jnp.where(valid, s4[g * c:(g + 1) * c], -jnp.inf) for g in range(ATT_GROUP)]
        sinks = [sink_ref[hd] for hd in heads_of(hk)]
        maxes = [jnp.maximum(jnp.max(s, axis=-1, keepdims=True), sink)
                 for s, sink in zip(masked, sinks)]
        probs = [jnp.exp(s - m).astype(BF16) for s, m in zip(masked, maxes)]
        sink_terms = [jnp.exp(sink - m) for sink, m in zip(sinks, maxes)]
        return jnp.concatenate(probs, axis=0), sink_terms

    ones = jnp.ones((2 * c, 2 * c - ATT_DH), BF16)

    def weighted_values(qb, hk, e4):
        return _dot(e4, jnp.concatenate([band(qb, hk, ATT_KV), ones], axis=1))

    def store(qb, hk, ov, sink_terms):
        groups = [ov[g * c:(g + 1) * c] for g in range(ATT_GROUP)]
        inv = [1.0 / (og[:, c:c + ATT_DH] + st) for og, st in zip(groups, sink_terms)]
        outs = [og[:, :ATT_DH] * r for og, r in zip(groups, inv)]
        for pair in range(ATT_GROUP // 2):
            col = heads_of(hk)[2 * pair] * ATT_DH
            o_ref[qb * c:(qb + 1) * c, col:col + 2 * ATT_DH] = jnp.concatenate(
                [outs[2 * pair], outs[2 * pair + 1]], axis=1).astype(BF16)

    units = [(qb, hk) for hk in range(ATT_KV_HEADS) for qb in range(n_blocks)]
    s_next = scores(*units[0])
    pending = None
    for u, unit in enumerate(units):
        s_cur = s_next
        if u + 1 < len(units):
            s_next = scores(*units[u + 1])
        e4, sink_terms = softmax(*unit, s_cur)
        if pending is not None:
            store(*pending)
        pending = (*unit, weighted_values(*unit, e4), sink_terms)
    store(*pending)


def _swa(aqkv, sinks, batch, seq, *, n_blocks=2):
    t = aqkv.shape[0]
    tq = n_blocks * ATT_BLOCK
    nq = seq // tq
    kv_col = ATT_Q // (2 * ATT_KV)
    return pl.pallas_call(
        functools.partial(_swa_body, n_blocks=n_blocks),
        out_shape=jax.ShapeDtypeStruct((t, ATT_Q), BF16),
        grid=(batch, nq),
        in_specs=[
            pl.BlockSpec(memory_space=pltpu.SMEM),
            pl.BlockSpec((tq, ATT_Q), lambda b, n: (b * nq + n, 0)),
            pl.BlockSpec((tq, 2 * ATT_KV), lambda b, n: (b * nq + n, kv_col)),
            pl.BlockSpec((ATT_BLOCK, 2 * ATT_KV),
                         lambda b, n: ((b * nq + n) * n_blocks - jnp.minimum(n, 1), kv_col)),
        ],
        out_specs=pl.BlockSpec((tq, ATT_Q), lambda b, n: (b * nq + n, 0)),
        compiler_params=_params("parallel", "arbitrary"),
        name="swa",
    )(sinks, aqkv, aqkv, aqkv)


def _mix_body(h_ref, ret_ref, att_ref, gr_ref, ga_ref, wr_ref, wa_ref, wm_ref, g_ref, b_ref,
              o_ref, *, alpha):
    ret_branch = _dot(ret_ref[...], wr_ref[...])
    att_branch = _dot(att_ref[...], wa_ref[...])
    merged = (jax.nn.sigmoid(gr_ref[...].astype(F32)) * ret_branch
              + jax.nn.sigmoid(ga_ref[...].astype(F32)) * att_branch)
    mixed = _dot(merged.astype(BF16), wm_ref[...])
    o_ref[...] = _layer_norm(alpha * h_ref[...] + mixed, g_ref[...], b_ref[...])


def _mix_ln(h, ret, att, gates, w_ret, w_att, w_mix, g, b, *, alpha, tm=512):
    t, d = h.shape
    row = pl.BlockSpec((1, d), lambda i: (0, 0))
    return pl.pallas_call(
        functools.partial(_mix_body, alpha=alpha),
        out_shape=jax.ShapeDtypeStruct((t, d), F32),
        grid=(t // tm,),
        in_specs=[
            pl.BlockSpec((tm, d), lambda i: (i, 0)),
            pl.BlockSpec((tm, RET_V), lambda i: (i, 0)),
            pl.BlockSpec((tm, ATT_Q), lambda i: (i, 0)),
            pl.BlockSpec((tm, d), lambda i: (i, 0)),
            pl.BlockSpec((tm, d), lambda i: (i, 1)),
            _resident((RET_V, d), lambda i: (0, 0)),
            _resident((ATT_Q, d), lambda i: (0, 0)),
            _resident((d, d), lambda i: (0, 0)),
            row, row,
        ],
        out_specs=pl.BlockSpec((tm, d), lambda i: (i, 0)),
        compiler_params=_params("parallel"),
        name="mix_ln",
    )(h, ret, att, gates, gates, w_ret, w_att, w_mix, g, b)


def _ple_body(h_ref, p_ref, wg_ref, wp_ref, g_ref, b_ref, o_ref, *, alpha, n_sub):
    sub = h_ref.shape[0] // n_sub
    for r0 in range(0, h_ref.shape[0], sub):
        h = h_ref[r0:r0 + sub, :]
        gate = jax.nn.sigmoid(_dot(h.astype(BF16), wg_ref[...]))
        ple = gate * _dot(p_ref[r0:r0 + sub, :].astype(BF16), wp_ref[...])
        o_ref[r0:r0 + sub, :] = _layer_norm(alpha * h + ple, g_ref[...], b_ref[...])


def _ple_ln(h, p, w_gate, w_proj, g, b, *, alpha, tm=1024, n_sub=2):
    t, d = h.shape
    pd = p.shape[1]
    row = pl.BlockSpec((1, d), lambda i: (0, 0))
    return pl.pallas_call(
        functools.partial(_ple_body, alpha=alpha, n_sub=n_sub),
        out_shape=jax.ShapeDtypeStruct((t, d), F32),
        grid=(t // tm,),
        in_specs=[
            pl.BlockSpec((tm, d), lambda i: (i, 0)),
            pl.BlockSpec((tm, pd), lambda i: (i, 0)),
            _resident((d, d), lambda i: (0, 0)),
            _resident((pd, d), lambda i: (0, 0)),
            row, row,
        ],
        out_specs=pl.BlockSpec((tm, d), lambda i: (i, 0)),
        compiler_params=_params("parallel"),
        name="ple_ln",
    )(h, p, w_gate, w_proj, g, b)


def kernel(x, p, positions, ln_g, ln_b, w_ffn1_gu, w_ffn1_down, w_in, ret_gn_g, att_sinks,
           w_ret_out, w_att_out, w_mix_out, w_ffn2_gu, w_ffn2_down, w_ple_gate, w_ple_proj):
    batch, seq, d = x.shape
    depth = ln_g.shape[0]
    t = batch * seq
    alpha = (2.0 * depth) ** 0.25
    h = x.reshape(t, d)
    cr, sr, ca, s1, s2 = _rope_tables(positions.reshape(t, 1))
    inner_decay, q_decay, k_decay, chunk_decay = _retention_decays()
    bounds = [0]
    for size in IN_SIZES:
        bounds.append(bounds[-1] + size)
    for i in range(depth):
        g = ln_g[i][:, None, :]
        b = ln_b[i][:, None, :]
        w_qk = w_in[i][:, bounds[0]:bounds[2]].astype(BF16)
        w_vg = w_in[i][:, bounds[2]:bounds[4]].astype(BF16)
        w_att = w_in[i][:, bounds[4]:bounds[7]].astype(BF16)
        w_gates = w_in[i][:, bounds[7]:bounds[9]].astype(BF16)

        h = _ffn_ln(h, w_ffn1_gu[i].astype(BF16), w_ffn1_down[i].astype(BF16), g[0], b[0], alpha=alpha)

        qkd = _proj(functools.partial(_proj_ret_qk_body, n_sub=2), h, w_qk, (cr, sr), (k_decay,),
                    n_out=3 * RET_QK, tm=1024, name="proj_ret_qk")
        vg = _proj(_proj_plain_body, h, w_vg, name="proj_ret_vg")
        aqkv = _proj(functools.partial(_proj_att_body, n_sub=2), h, w_att, (ca, s1, s2),
                     tm=1024, name="proj_att")
        gates = _proj(_proj_plain_body, h, w_gates, name="proj_gates")

        ret = _retention(qkd, vg, inner_decay, q_decay, chunk_decay, ret_gn_g[i], batch, seq)
        att = _swa(aqkv, att_sinks[i], batch, seq)

        h = _mix_ln(h, ret, att, gates, w_ret_out[i].astype(BF16), w_att_out[i].astype(BF16),
                    w_mix_out[i].astype(BF16), g[1], b[1], alpha=alpha)
        h = _ffn_ln(h, w_ffn2_gu[i].astype(BF16), w_ffn2_down[i].astype(BF16), g[2], b[2], alpha=alpha)
        h = _ple_ln(h, p[i].reshape(t, PLE_DIM), w_ple_gate[i].astype(BF16),
                    w_ple_proj[i].astype(BF16), g[3], b[3], alpha=alpha)
    return h.reshape(batch, seq, d)
```

```python
import functools

import jax
import jax.numpy as jnp
from jax import lax
from jax.experimental import pallas as pl
from jax.experimental.pallas import tpu as pltpu

F32 = jnp.float32
BF16 = jnp.bfloat16

D_MODEL = 2048
PLE_DIM = 256
D_FF = 5632
RET_HEADS = 8
RET_DK = 128
RET_DV = 256
RET_CHUNK = 128
RET_THETA = 10000.0
ATT_HEADS = 16
ATT_KV_HEADS = 4
ATT_DH = 64
ATT_GROUP = ATT_HEADS // ATT_KV_HEADS
WINDOW = 128
ATT_BLOCK = 128
ROPE_THETA = 500000.0
ROPE_DIMS = ATT_DH // 4
LN_EPS = 1e-5
GN_EPS = 1e-5

RET_QK = RET_HEADS * RET_DK
RET_V = RET_HEADS * RET_DV
ATT_Q = ATT_HEADS * ATT_DH
ATT_KV = ATT_KV_HEADS * ATT_DH
IN_SIZES = (RET_QK, RET_QK, RET_V, RET_V, ATT_Q, ATT_KV, ATT_KV, D_MODEL, D_MODEL)

LANES = 128
VMEM_LIMIT_BYTES = 60 * 1024 * 1024


def _params(*semantics):
    return pltpu.CompilerParams(dimension_semantics=semantics, vmem_limit_bytes=VMEM_LIMIT_BYTES)


def _resident(shape, index_map):
    return pl.BlockSpec(shape, index_map, pipeline_mode=pl.Buffered(1))


def _layer_norm(y, g, b):
    mu = jnp.mean(y, axis=-1, keepdims=True)
    d = y - mu
    var = jnp.mean(d * d, axis=-1, keepdims=True)
    return d * lax.rsqrt(var + LN_EPS) * g + b


def _dot(a, b):
    return jnp.dot(a, b, preferred_element_type=F32)


def _ffn_body(x_ref, wg_ref, wu_ref, wd_ref, g_ref, b_ref, o_ref, xb_ref, *, alpha, n_sub):
    j = pl.program_id(1)
    sub = x_ref.shape[0] // n_sub

    @pl.when(j == 0)
    def _():
        xb_ref[...] = x_ref[...].astype(BF16)
        o_ref[...] = jnp.zeros_like(o_ref)

    for r in range(n_sub):
        rows = pl.ds(r * sub, sub)
        xb = xb_ref[rows, :]
        gate = _dot(xb, wg_ref[...])
        up = _dot(xb, wu_ref[...])
        act = (gate * jax.nn.sigmoid(gate) * up).astype(BF16)
        o_ref[rows, :] += _dot(act, wd_ref[...])

    @pl.when(j == pl.num_programs(1) - 1)
    def _():
        for r in range(n_sub):
            rows = pl.ds(r * sub, sub)
            y = alpha * x_ref[rows, :] + 0.5 * o_ref[rows, :]
            o_ref[rows, :] = _layer_norm(y, g_ref[...], b_ref[...])


def _ffn_ln(x, w_gu, w_down, g, b, *, alpha, tm=1024, tf=512, n_sub=2):
    t, d = x.shape
    f = w_down.shape[0]
    nj = f // tf
    return pl.pallas_call(
        functools.partial(_ffn_body, alpha=alpha, n_sub=n_sub),
        out_shape=jax.ShapeDtypeStruct((t, d), F32),
        grid=(t // tm, nj),
        in_specs=[
            pl.BlockSpec((tm, d), lambda i, j: (i, 0)),
            pl.BlockSpec((d, tf), lambda i, j: (0, j)),
            pl.BlockSpec((d, tf), lambda i, j: (0, j + nj)),
            pl.BlockSpec((tf, d), lambda i, j: (j, 0)),
            pl.BlockSpec((1, d), lambda i, j: (0, 0)),
            pl.BlockSpec((1, d), lambda i, j: (0, 0)),
        ],
        out_specs=pl.BlockSpec((tm, d), lambda i, j: (i, 0)),
        scratch_shapes=[pltpu.VMEM((tm, d), BF16)],
        compiler_params=_params("parallel", "arbitrary"),
        name="ffn_ln",
    )(x, w_gu, w_gu, w_down, g, b)


def _rope_body(pos_ref, inv_ref, cr_ref, sr_ref, ca_ref, s1_ref, s2_ref):
    pos = pos_ref[...].astype(F32)
    ang = pos * inv_ref[...]
    c = jnp.cos(ang)
    s = jnp.sin(ang)
    half_r = RET_DK // 2
    half_a = ROPE_DIMS // 2
    lane = lax.broadcasted_iota(jnp.int32, c.shape, 1)
    low = lane < half_r
    c_swap = pltpu.roll(c, half_r, axis=1)
    s_swap = pltpu.roll(s, half_r, axis=1)
    cr_ref[...] = jnp.where(low, c, c_swap)
    sr_ref[...] = jnp.where(low, -s, s_swap)
    c_att = jnp.where(low, c_swap, c)
    s_att = jnp.where(low, s_swap, s)
    in_head = lane & (ATT_DH - 1)
    first = in_head < half_a
    second = (in_head >= half_a) & (in_head < ROPE_DIMS)
    ca_ref[...] = jnp.where(first, c_att, jnp.where(second, pltpu.roll(c_att, half_a, axis=1), 1.0))
    s1_ref[...] = jnp.where(second, pltpu.roll(s_att, half_a, axis=1), 0.0)
    s2_ref[...] = jnp.where(first, -s_att, 0.0)


def _rope_tables(pos, tm=1024):
    t = pos.shape[0]
    half_r = RET_DK // 2
    half_a = ROPE_DIMS // 2
    inv_r = 1.0 / (RET_THETA ** (jnp.arange(half_r, dtype=F32) / half_r))
    inv_a = 1.0 / (ROPE_THETA ** (jnp.arange(half_a, dtype=F32) / half_a))
    inv = jnp.concatenate([inv_r, inv_a, jnp.zeros((LANES - half_r - half_a,), F32)])[None, :]
    tab = pl.BlockSpec((tm, LANES), lambda i: (i, 0))
    return pl.pallas_call(
        _rope_body,
        out_shape=[jax.ShapeDtypeStruct((t, LANES), F32)] * 5,
        grid=(t // tm,),
        in_specs=[pl.BlockSpec((tm, 1), lambda i: (i, 0)),
                  pl.BlockSpec((1, LANES), lambda i: (0, 0))],
        out_specs=[tab] * 5,
        compiler_params=_params("parallel"),
        name="rope_tables",
    )(pos, inv)


def _proj_ret_qk_body(x_ref, w_ref, cr_ref, sr_ref, kdec_ref, o_ref, *, n_sub):
    sub = x_ref.shape[0] // n_sub
    k_scale = RET_DK ** -0.5
    for r0 in range(0, x_ref.shape[0], sub):
        r = _dot(x_ref[r0:r0 + sub, :].astype(BF16), w_ref[...])
        cr = cr_ref[r0:r0 + sub, :]
        sr = sr_ref[r0:r0 + sub, :]
        for s in range(2 * RET_HEADS):
            x = r[:, s * LANES:(s + 1) * LANES]
            rot = x * cr + pltpu.roll(x, RET_DK // 2, axis=1) * sr
            if s < RET_HEADS:
                o_ref[r0:r0 + sub, s * LANES:(s + 1) * LANES] = rot.astype(BF16)
                continue
            rot = rot * k_scale
            o_ref[r0:r0 + sub, s * LANES:(s + 1) * LANES] = rot.astype(BF16)
            k_decay = kdec_ref[:, (s - RET_HEADS) * LANES:(s - RET_HEADS + 1) * LANES]
            d = s + RET_HEADS
            for c0 in range(0, sub, RET_CHUNK):
                o_ref[r0 + c0:r0 + c0 + RET_CHUNK, d * LANES:(d + 1) * LANES] = (
                    rot[c0:c0 + RET_CHUNK] * k_decay).astype(BF16)


def _proj_att_body(x_ref, w_ref, ca_ref, s1_ref, s2_ref, o_ref, *, n_sub):
    sub = x_ref.shape[0] // n_sub
    half = ROPE_DIMS // 2
    n_rot = (ATT_Q + ATT_KV) // LANES
    for r0 in range(0, x_ref.shape[0], sub):
        r = _dot(x_ref[r0:r0 + sub, :].astype(BF16), w_ref[...])
        ca = ca_ref[r0:r0 + sub, :]
        s1 = s1_ref[r0:r0 + sub, :]
        s2 = s2_ref[r0:r0 + sub, :]
        for s in range(n_rot):
            x = r[:, s * LANES:(s + 1) * LANES]
            rot = (x * ca + pltpu.roll(x, half, axis=1) * s1
                   + pltpu.roll(x, LANES - half, axis=1) * s2)
            o_ref[r0:r0 + sub, s * LANES:(s + 1) * LANES] = rot.astype(BF16)
        o_ref[r0:r0 + sub, n_rot * LANES:] = r[:, n_rot * LANES:].astype(BF16)


def _proj(body, x, w, tables=(), consts=(), *, n_out=None, tm=512, name):
    t, d = x.shape
    n = w.shape[1]
    n_out = n if n_out is None else n_out
    tab = pl.BlockSpec((tm, LANES), lambda i: (i, 0))
    return pl.pallas_call(
        body,
        out_shape=jax.ShapeDtypeStruct((t, n_out), BF16),
        grid=(t // tm,),
        in_specs=[pl.BlockSpec((tm, d), lambda i: (i, 0)), _resident((d, n), lambda i: (0, 0))]
        + [tab] * len(tables) + [_resident(c.shape, lambda i: (0, 0)) for c in consts],
        out_specs=pl.BlockSpec((tm, n_out), lambda i: (i, 0)),
        compiler_params=_params("parallel"),
        name=name,
    )(x, w, *tables, *consts)


def _retention_body(q_ref, k_ref, kd_ref, v_ref, g_ref, inner_ref, qd_ref, cd_ref, gn_ref,
                    h_ref, w_ref, o_ref, proj_ref, state_ref, *, n_chunks, proj_every):
    @pl.when(pl.program_id(2) == 0)
    def _():
        state_ref[...] = jnp.zeros_like(state_ref)

    inner_decay = inner_ref[...]
    q_decay = qd_ref[...]
    chunk_decay = cd_ref[...]
    gn = gn_ref[...]
    hb = h_ref[...].astype(BF16)
    proj_cols = proj_ref.shape[1] * proj_every // n_chunks
    for c in range(n_chunks):
        if c % proj_every == 0:
            cols = pl.ds((c // proj_every) * proj_cols, proj_cols)
            proj_ref[:, cols] = _dot(hb, w_ref[:, cols]).astype(BF16)
        rows = pl.ds(c * RET_CHUNK, RET_CHUNK)
        qn = q_ref[rows, :]
        vn = v_ref[rows, :]
        scores = lax.dot_general(qn, k_ref[rows, :], (((1,), (1,)), ((), ())),
                                 preferred_element_type=F32)
        inner = _dot((scores * inner_decay).astype(BF16), vn)
        state = state_ref[...]
        cross = _dot(qn, state.astype(BF16)) * q_decay
        state_ref[...] = state * chunk_decay + lax.dot_general(
            kd_ref[rows, :], vn, (((0,), (0,)), ((), ())), preferred_element_type=F32)
        o = inner + cross
        mu = jnp.mean(o, axis=-1, keepdims=True)
        d = o - mu
        var = jnp.mean(d * d, axis=-1, keepdims=True)
        y = d * lax.rsqrt(var + GN_EPS) * gn
        gate = g_ref[rows, :].astype(F32)
        o_ref[rows, :] = (gate * jax.nn.sigmoid(gate) * y).astype(BF16)


def _retention_decays():
    c = RET_CHUNK
    log_g = jnp.log1p(-jnp.exp2(-5.0 - jnp.arange(RET_HEADS, dtype=F32)))
    idx = jnp.arange(c, dtype=F32)
    q_decay = jnp.exp((idx[None, :] + 1.0) * log_g[:, None])
    k_decay = jnp.exp((c - 1.0 - idx)[None, :] * log_g[:, None])
    diff = idx[:, None] - idx[None, :]
    inner = jnp.where(diff[None] >= 0,
                      jnp.exp(jnp.maximum(diff, 0.0)[None] * log_g[:, None, None]), 0.0)
    chunk_decay = jnp.exp(c * log_g)
    q_decay = jnp.broadcast_to(q_decay[:, :, None], (RET_HEADS, c, RET_DV))
    k_decay = jnp.broadcast_to(k_decay.T[:, :, None], (c, RET_HEADS, RET_DK)).reshape(c, RET_QK)
    chunk_decay = jnp.broadcast_to(chunk_decay[:, None, None], (RET_HEADS, 1, RET_DV))
    return inner, q_decay, k_decay, chunk_decay


def _retention_and_proj(qkd, vg, inner, q_decay, chunk_decay, gn_g, h, w, batch, seq, *,
                        ts=4096, proj_every=2):
    t, d = h.shape
    n_out = w.shape[1]
    ns = seq // ts
    tp = t // (batch * RET_HEADS * ns)
    gn = gn_g.reshape(RET_HEADS, 1, RET_DV)

    def rows(b, hd, s):
        return b * ns + s

    def tile(b, hd, s):
        return (b * RET_HEADS + hd) * ns + s

    head_tab = lambda shape: pl.BlockSpec((None,) + shape, lambda b, hd, s: (hd, 0, 0))
    return pl.pallas_call(
        functools.partial(_retention_body, n_chunks=ts // RET_CHUNK, proj_every=proj_every),
        out_shape=[jax.ShapeDtypeStruct((t, RET_V), BF16), jax.ShapeDtypeStruct((t, n_out), BF16)],
        grid=(batch, RET_HEADS, ns),
        in_specs=[
            pl.BlockSpec((ts, RET_DK), lambda b, hd, s: (rows(b, hd, s), hd)),
            pl.BlockSpec((ts, RET_DK), lambda b, hd, s: (rows(b, hd, s), RET_HEADS + hd)),
            pl.BlockSpec((ts, RET_DK), lambda b, hd, s: (rows(b, hd, s), 2 * RET_HEADS + hd)),
            pl.BlockSpec((ts, RET_DV), lambda b, hd, s: (rows(b, hd, s), hd)),
            pl.BlockSpec((ts, RET_DV), lambda b, hd, s: (rows(b, hd, s), RET_HEADS + hd)),
            head_tab((RET_CHUNK, RET_CHUNK)),
            head_tab((RET_CHUNK, RET_DV)),
            head_tab((1, RET_DV)),
            head_tab((1, RET_DV)),
            pl.BlockSpec((tp, d), lambda b, hd, s: (tile(b, hd, s), 0)),
            _resident((d, n_out), lambda b, hd, s: (0, 0)),
        ],
        out_specs=[pl.BlockSpec((ts, RET_DV), lambda b, hd, s: (rows(b, hd, s), hd)),
                   pl.BlockSpec((tp, n_out), lambda b, hd, s: (tile(b, hd, s), 0))],
        scratch_shapes=[pltpu.VMEM((RET_DK, RET_DV), F32)],
        compiler_params=_params("parallel", "parallel", "arbitrary"),
        name="retention_proj",
    )(qkd, qkd, qkd, vg, vg, inner, q_decay, chunk_decay, gn, h, w)


def _swa_body(sink_ref, q_ref, kvc_ref, kvp_ref, h_ref, w_ref, o_ref, proj_ref, *, n_blocks):
    n = pl.program_id(1)
    c = ATT_BLOCK
    qi = lax.broadcasted_iota(jnp.int32, (c, 2 * c), 0) + c
    kj = lax.broadcasted_iota(jnp.int32, (c, 2 * c), 1)
    rel = qi - kj
    in_window = (rel >= 0) & (rel < WINDOW)
    valid_first = in_window & (kj >= jnp.where(n > 0, 0, c))
    scale = ATT_DH ** -0.5

    def heads_of(hk):
        return [hk * ATT_GROUP + g for g in range(ATT_GROUP)]

    def band(qb, hk, offset):
        cols = slice(offset + hk * ATT_DH, offset + (hk + 1) * ATT_DH)
        prev = kvp_ref[:, cols] if qb == 0 else kvc_ref[(qb - 1) * c:qb * c, cols]
        return jnp.concatenate([prev, kvc_ref[qb * c:(qb + 1) * c, cols]], axis=0)

    def scores(qb, hk):
        q4 = jnp.concatenate([q_ref[qb * c:(qb + 1) * c, hd * ATT_DH:(hd + 1) * ATT_DH]
                              for hd in heads_of(hk)], axis=0)
        return lax.dot_general(q4 * scale, band(qb, hk, 0), (((1,), (1,)), ((), ())),
                               preferred_element_type=F32)

    def softmax(qb, hk, s4):
        valid = valid_first if qb == 0 else in_window
        masked = [jnp.where(valid, s4[g * c:(g + 1) * c], -jnp.inf) for g in range(ATT_GROUP)]
        sinks = [sink_ref[hd] for hd in heads_of(hk)]
        maxes = [jnp.maximum(jnp.max(s, axis=-1, keepdims=True), sink)
                 for s, sink in zip(masked, sinks)]
        probs = [jnp.exp(s - m).astype(BF16) for s, m in zip(masked, maxes)]
        sink_terms = [jnp.exp(sink - m) for sink, m in zip(sinks, maxes)]
        return jnp.concatenate(probs, axis=0), sink_terms

    ones = jnp.ones((2 * c, 2 * c - ATT_DH), BF16)

    def weighted_values(qb, hk, e4):
        return _dot(e4, jnp.concatenate([band(qb, hk, ATT_KV), ones], axis=1))

    def store(qb, hk, ov, sink_terms):
        groups = [ov[g * c:(g + 1) * c] for g in range(ATT_GROUP)]
        inv = [1.0 / (og[:, c:c + ATT_DH] + st) for og, st in zip(groups, sink_terms)]
        outs = [og[:, :ATT_DH] * r for og, r in zip(groups, inv)]
        for pair in range(ATT_GROUP // 2):
            col = heads_of(hk)[2 * pair] * ATT_DH
            o_ref[qb * c:(qb + 1) * c, col:col + 2 * ATT_DH] = jnp.concatenate(
                [outs[2 * pair], outs[2 * pair + 1]], axis=1).astype(BF16)

    units = [(qb, hk) for hk in range(ATT_KV_HEADS) for qb in range(n_blocks)]
    hb = h_ref[...].astype(BF16)
    chunk = proj_ref.shape[1] // len(units)

    def project(u):
        cols = slice(u * chunk, (u + 1) * chunk)
        proj_ref[:, cols] = _dot(hb, w_ref[:, cols]).astype(BF16)

    s_next = scores(*units[0])
    pending = None
    for u, unit in enumerate(units):
        s_cur = s_next
        if u + 1 < len(units):
            s_next = scores(*units[u + 1])
        project(u)
        e4, sink_terms = softmax(*unit, s_cur)
        if pending is not None:
            store(*pending)
        pending = (*unit, weighted_values(*unit, e4), sink_terms)
    store(*pending)


def _swa_and_proj(aqkv, sinks, h, w, batch, seq, *, n_blocks=4):
    t, d = h.shape
    n_out = w.shape[1]
    tq = n_blocks * ATT_BLOCK
    nq = seq // tq
    kv_col = ATT_Q // (2 * ATT_KV)
    return pl.pallas_call(
        functools.partial(_swa_body, n_blocks=n_blocks),
        out_shape=[jax.ShapeDtypeStruct((t, ATT_Q), BF16), jax.ShapeDtypeStruct((t, n_out), BF16)],
        grid=(batch, nq),
        in_specs=[
            pl.BlockSpec(memory_space=pltpu.SMEM),
            pl.BlockSpec((tq, ATT_Q), lambda b, n: (b * nq + n, 0)),
            pl.BlockSpec((tq, 2 * ATT_KV), lambda b, n: (b * nq + n, kv_col)),
            pl.BlockSpec((ATT_BLOCK, 2 * ATT_KV),
                         lambda b, n: ((b * nq + n) * n_blocks - jnp.minimum(n, 1), kv_col)),
            pl.BlockSpec((tq, d), lambda b, n: (b * nq + n, 0)),
            _resident((d, n_out), lambda b, n: (0, 0)),
        ],
        out_specs=[pl.BlockSpec((tq, ATT_Q), lambda b, n: (b * nq + n, 0)),
                   pl.BlockSpec((tq, n_out), lambda b, n: (b * nq + n, 0))],
        compiler_params=_params("parallel", "arbitrary"),
        name="swa_proj",
    )(sinks, aqkv, aqkv, aqkv, h, w)


def _mix_body(h_ref, ret_ref, att_ref, gr_ref, ga_ref, wr_ref, wa_ref, wm_ref, g_ref, b_ref,
              o_ref, *, alpha):
    ret_branch = _dot(ret_ref[...], wr_ref[...])
    att_branch = _dot(att_ref[...], wa_ref[...])
    merged = (jax.nn.sigmoid(gr_ref[...].astype(F32)) * ret_branch
              + jax.nn.sigmoid(ga_ref[...].astype(F32)) * att_branch)
    mixed = _dot(merged.astype(BF16), wm_ref[...])
    o_ref[...] = _layer_norm(alpha * h_ref[...] + mixed, g_ref[...], b_ref[...])


def _mix_ln(h, ret, att, gates, w_ret, w_att, w_mix, g, b, *, alpha, tm=512):
    t, d = h.shape
    row = pl.BlockSpec((1, d), lambda i: (0, 0))
    return pl.pallas_call(
        functools.partial(_mix_body, alpha=alpha),
        out_shape=jax.ShapeDtypeStruct((t, d), F32),
        grid=(t // tm,),
        in_specs=[
            pl.BlockSpec((tm, d), lambda i: (i, 0)),
            pl.BlockSpec((tm, RET_V), lambda i: (i, 0)),
            pl.BlockSpec((tm, ATT_Q), lambda i: (i, 0)),
            pl.BlockSpec((tm, d), lambda i: (i, 0)),
            pl.BlockSpec((tm, d), lambda i: (i, 1)),
            _resident((RET_V, d), lambda i: (0, 0)),
            _resident((ATT_Q, d), lambda i: (0, 0)),
            _resident((d, d), lambda i: (0, 0)),
            row, row,
        ],
        out_specs=pl.BlockSpec((tm, d), lambda i: (i, 0)),
        compiler_params=_params("parallel"),
        name="mix_ln",
    )(h, ret, att, gates, gates, w_ret, w_att, w_mix, g, b)


def _ple_body(h_ref, p_ref, wg_ref, wp_ref, g_ref, b_ref, o_ref, *, alpha, n_sub):
    sub = h_ref.shape[0] // n_sub
    for r0 in range(0, h_ref.shape[0], sub):
        h = h_ref[r0:r0 + sub, :]
        gate = jax.nn.sigmoid(_dot(h.astype(BF16), wg_ref[...]))
        ple = gate * _dot(p_ref[r0:r0 + sub, :].astype(BF16), wp_ref[...])
        o_ref[r0:r0 + sub, :] = _layer_norm(alpha * h + ple, g_ref[...], b_ref[...])


def _ple_ln(h, p, w_gate, w_proj, g, b, *, alpha, tm=1024, n_sub=2):
    t, d = h.shape
    pd = p.shape[1]
    row = pl.BlockSpec((1, d), lambda i: (0, 0))
    return pl.pallas_call(
        functools.partial(_ple_body, alpha=alpha, n_sub=n_sub),
        out_shape=jax.ShapeDtypeStruct((t, d), F32),
        grid=(t // tm,),
        in_specs=[
            pl.BlockSpec((tm, d), lambda i: (i, 0)),
            pl.BlockSpec((tm, pd), lambda i: (i, 0)),
            _resident((d, d), lambda i: (0, 0)),
            _resident((pd, d), lambda i: (0, 0)),
            row, row,
        ],
        out_specs=pl.BlockSpec((tm, d), lambda i: (i, 0)),
        compiler_params=_params("parallel"),
        name="ple_ln",
    )(h, p, w_gate, w_proj, g, b)


def kernel(x, p, positions, ln_g, ln_b, w_ffn1_gu, w_ffn1_down, w_in, ret_gn_g, att_sinks,
           w_ret_out, w_att_out, w_mix_out, w_ffn2_gu, w_ffn2_down, w_ple_gate, w_ple_proj):
    batch, seq, d = x.shape
    depth = ln_g.shape[0]
    t = batch * seq
    alpha = (2.0 * depth) ** 0.25
    h = x.reshape(t, d)
    cr, sr, ca, s1, s2 = _rope_tables(positions.reshape(t, 1))
    inner_decay, q_decay, k_decay, chunk_decay = _retention_decays()
    bounds = [0]
    for size in IN_SIZES:
        bounds.append(bounds[-1] + size)
    for i in range(depth):
        g = ln_g[i][:, None, :]
        b = ln_b[i][:, None, :]
        w_qk = w_in[i][:, bounds[0]:bounds[2]].astype(BF16)
        w_vg = w_in[i][:, bounds[2]:bounds[4]].astype(BF16)
        w_att = w_in[i][:, bounds[4]:bounds[7]].astype(BF16)
        w_gates = w_in[i][:, bounds[7]:bounds[9]].astype(BF16)

        h = _ffn_ln(h, w_ffn1_gu[i].astype(BF16), w_ffn1_down[i].astype(BF16), g[0], b[0], alpha=alpha)

        qkd = _proj(functools.partial(_proj_ret_qk_body, n_sub=2), h, w_qk, (cr, sr), (k_decay,),
                    n_out=3 * RET_QK, tm=1024, name="proj_ret_qk")
        aqkv = _proj(functools.partial(_proj_att_body, n_sub=2), h, w_att, (ca, s1, s2),
                     tm=1024, name="proj_att")
        att, vg = _swa_and_proj(aqkv, att_sinks[i], h, w_vg, batch, seq)
        ret, gates = _retention_and_proj(qkd, vg, inner_decay, q_decay, chunk_decay, ret_gn_g[i],
                                         h, w_gates, batch, seq)

        h = _mix_ln(h, ret, att, gates, w_ret_out[i].astype(BF16), w_att_out[i].astype(BF16),
                    w_mix_out[i].astype(BF16), g[1], b[1], alpha=alpha)
        h = _ffn_ln(h, w_ffn2_gu[i].astype(BF16), w_ffn2_down[i].astype(BF16), g[2], b[2], alpha=alpha)
        h = _ple_ln(h, p[i].reshape(t, PLE_DIM), w_ple_gate[i].astype(BF16),
                    w_ple_proj[i].astype(BF16), g[3], b[3], alpha=alpha)
    return h.reshape(batch, seq, d)
```

```python
import functools

import jax
import jax.numpy as jnp
from jax import lax
from jax.experimental import pallas as pl
from jax.experimental.pallas import tpu as pltpu

F32 = jnp.float32
BF16 = jnp.bfloat16

D_MODEL = 2048
PLE_DIM = 256
D_FF = 5632
RET_HEADS = 8
RET_DK = 128
RET_DV = 256
RET_CHUNK = 128
RET_THETA = 10000.0
ATT_HEADS = 16
ATT_KV_HEADS = 4
ATT_DH = 64
ATT_GROUP = ATT_HEADS // ATT_KV_HEADS
WINDOW = 128
ATT_BLOCK = 128
ROPE_THETA = 500000.0
ROPE_DIMS = ATT_DH // 4
LN_EPS = 1e-5
GN_EPS = 1e-5

RET_QK = RET_HEADS * RET_DK
RET_V = RET_HEADS * RET_DV
ATT_Q = ATT_HEADS * ATT_DH
ATT_KV = ATT_KV_HEADS * ATT_DH
IN_SIZES = (RET_QK, RET_QK, RET_V, RET_V, ATT_Q, ATT_KV, ATT_KV, D_MODEL, D_MODEL)

LANES = 128
VMEM_LIMIT_BYTES = 60 * 1024 * 1024


def _params(*semantics):
    return pltpu.CompilerParams(dimension_semantics=semantics, vmem_limit_bytes=VMEM_LIMIT_BYTES)


def _resident(shape, index_map):
    return pl.BlockSpec(shape, index_map, pipeline_mode=pl.Buffered(1))


def _side_cast_specs(arrays, n_steps, step_of):
    specs = []
    for a in arrays:
        r, c = a.shape
        gc = next(k for k in (1, 2, 4, 8) if (r * k) % (16 * n_steps) == 0 and c % (k * LANES) == 0)
        gr = n_steps // gc
        specs.append(pl.BlockSpec(
            (r // gr, c // gc), lambda *idx, gc=gc: (step_of(*idx) // gc, step_of(*idx) % gc)))
    return specs


def _side_casts(refs, n_fixed_out):
    n = (len(refs) - n_fixed_out) // 2
    for src, dst in zip(refs[:n], refs[n + n_fixed_out:]):
        dst[...] = src[...].astype(BF16)
    return refs[n:n + n_fixed_out]


def _layer_norm(y, g, b):
    mu = jnp.mean(y, axis=-1, keepdims=True)
    d = y - mu
    var = jnp.mean(d * d, axis=-1, keepdims=True)
    return d * lax.rsqrt(var + LN_EPS) * g + b


def _dot(a, b):
    return jnp.dot(a, b, preferred_element_type=F32)


def _ffn_body(x_ref, wg_ref, wu_ref, wd_ref, g_ref, b_ref, o_ref, xb_ref, *, alpha, n_sub):
    j = pl.program_id(1)
    sub = x_ref.shape[0] // n_sub

    @pl.when(j == 0)
    def _():
        xb_ref[...] = x_ref[...].astype(BF16)
        o_ref[...] = jnp.zeros_like(o_ref)

    for r in range(n_sub):
        rows = pl.ds(r * sub, sub)
        xb = xb_ref[rows, :]
        gate = _dot(xb, wg_ref[...])
        up = _dot(xb, wu_ref[...])
        act = (gate * jax.nn.sigmoid(gate) * up).astype(BF16)
        o_ref[rows, :] += _dot(act, wd_ref[...])

    @pl.when(j == pl.num_programs(1) - 1)
    def _():
        for r in range(n_sub):
            rows = pl.ds(r * sub, sub)
            y = alpha * x_ref[rows, :] + 0.5 * o_ref[rows, :]
            o_ref[rows, :] = _layer_norm(y, g_ref[...], b_ref[...])


def _ffn_ln(x, w_gu, w_down, g, b, *, alpha, tm=1024, tf=512, n_sub=2):
    t, d = x.shape
    f = w_down.shape[0]
    nj = f // tf
    return pl.pallas_call(
        functools.partial(_ffn_body, alpha=alpha, n_sub=n_sub),
        out_shape=jax.ShapeDtypeStruct((t, d), F32),
        grid=(t // tm, nj),
        in_specs=[
            pl.BlockSpec((tm, d), lambda i, j: (i, 0)),
            pl.BlockSpec((d, tf), lambda i, j: (0, j)),
            pl.BlockSpec((d, tf), lambda i, j: (0, j + nj)),
            pl.BlockSpec((tf, d), lambda i, j: (j, 0)),
            pl.BlockSpec((1, d), lambda i, j: (0, 0)),
            pl.BlockSpec((1, d), lambda i, j: (0, 0)),
        ],
        out_specs=pl.BlockSpec((tm, d), lambda i, j: (i, 0)),
        scratch_shapes=[pltpu.VMEM((tm, d), BF16)],
        compiler_params=_params("parallel", "arbitrary"),
        name="ffn_ln",
    )(x, w_gu, w_gu, w_down, g, b)


def _rope_body(pos_ref, inv_ref, cr_ref, sr_ref, ca_ref, s1_ref, s2_ref):
    pos = pos_ref[...].astype(F32)
    ang = pos * inv_ref[...]
    c = jnp.cos(ang)
    s = jnp.sin(ang)
    half_r = RET_DK // 2
    half_a = ROPE_DIMS // 2
    lane = lax.broadcasted_iota(jnp.int32, c.shape, 1)
    low = lane < half_r
    c_swap = pltpu.roll(c, half_r, axis=1)
    s_swap = pltpu.roll(s, half_r, axis=1)
    cr_ref[...] = jnp.where(low, c, c_swap)
    sr_ref[...] = jnp.where(low, -s, s_swap)
    c_att = jnp.where(low, c_swap, c)
    s_att = jnp.where(low, s_swap, s)
    in_head = lane & (ATT_DH - 1)
    first = in_head < half_a
    second = (in_head >= half_a) & (in_head < ROPE_DIMS)
    ca_ref[...] = jnp.where(first, c_att, jnp.where(second, pltpu.roll(c_att, half_a, axis=1), 1.0))
    s1_ref[...] = jnp.where(second, pltpu.roll(s_att, half_a, axis=1), 0.0)
    s2_ref[...] = jnp.where(first, -s_att, 0.0)


def _rope_tables(pos, tm=1024):
    t = pos.shape[0]
    half_r = RET_DK // 2
    half_a = ROPE_DIMS // 2
    inv_r = 1.0 / (RET_THETA ** (jnp.arange(half_r, dtype=F32) / half_r))
    inv_a = 1.0 / (ROPE_THETA ** (jnp.arange(half_a, dtype=F32) / half_a))
    inv = jnp.concatenate([inv_r, inv_a, jnp.zeros((LANES - half_r - half_a,), F32)])[None, :]
    tab = pl.BlockSpec((tm, LANES), lambda i: (i, 0))
    return pl.pallas_call(
        _rope_body,
        out_shape=[jax.ShapeDtypeStruct((t, LANES), F32)] * 5,
        grid=(t // tm,),
        in_specs=[pl.BlockSpec((tm, 1), lambda i: (i, 0)),
                  pl.BlockSpec((1, LANES), lambda i: (0, 0))],
        out_specs=[tab] * 5,
        compiler_params=_params("parallel"),
        name="rope_tables",
    )(pos, inv)


def _proj_ret_qk_body(x_ref, w_ref, cr_ref, sr_ref, kdec_ref, o_ref, *, n_sub):
    sub = x_ref.shape[0] // n_sub
    k_scale = RET_DK ** -0.5
    for r0 in range(0, x_ref.shape[0], sub):
        r = _dot(x_ref[r0:r0 + sub, :].astype(BF16), w_ref[...])
        cr = cr_ref[r0:r0 + sub, :]
        sr = sr_ref[r0:r0 + sub, :]
        for s in range(2 * RET_HEADS):
            x = r[:, s * LANES:(s + 1) * LANES]
            rot = x * cr + pltpu.roll(x, RET_DK // 2, axis=1) * sr
            if s < RET_HEADS:
                o_ref[r0:r0 + sub, s * LANES:(s + 1) * LANES] = rot.astype(BF16)
                continue
            rot = rot * k_scale
            o_ref[r0:r0 + sub, s * LANES:(s + 1) * LANES] = rot.astype(BF16)
            k_decay = kdec_ref[:, (s - RET_HEADS) * LANES:(s - RET_HEADS + 1) * LANES]
            d = s + RET_HEADS
            for c0 in range(0, sub, RET_CHUNK):
                o_ref[r0 + c0:r0 + c0 + RET_CHUNK, d * LANES:(d + 1) * LANES] = (
                    rot[c0:c0 + RET_CHUNK] * k_decay).astype(BF16)


def _proj_att_body(x_ref, w_ref, ca_ref, s1_ref, s2_ref, o_ref, *, n_sub):
    sub = x_ref.shape[0] // n_sub
    half = ROPE_DIMS // 2
    n_rot = (ATT_Q + ATT_KV) // LANES
    for r0 in range(0, x_ref.shape[0], sub):
        r = _dot(x_ref[r0:r0 + sub, :].astype(BF16), w_ref[...])
        ca = ca_ref[r0:r0 + sub, :]
        s1 = s1_ref[r0:r0 + sub, :]
        s2 = s2_ref[r0:r0 + sub, :]
        for s in range(n_rot):
            x = r[:, s * LANES:(s + 1) * LANES]
            rot = (x * ca + pltpu.roll(x, half, axis=1) * s1
                   + pltpu.roll(x, LANES - half, axis=1) * s2)
            o_ref[r0:r0 + sub, s * LANES:(s + 1) * LANES] = rot.astype(BF16)
        o_ref[r0:r0 + sub, n_rot * LANES:] = r[:, n_rot * LANES:].astype(BF16)


def _proj(body, x, w, tables=(), consts=(), *, n_out=None, tm=512, name):
    t, d = x.shape
    n = w.shape[1]
    n_out = n if n_out is None else n_out
    tab = pl.BlockSpec((tm, LANES), lambda i: (i, 0))
    return pl.pallas_call(
        body,
        out_shape=jax.ShapeDtypeStruct((t, n_out), BF16),
        grid=(t // tm,),
        in_specs=[pl.BlockSpec((tm, d), lambda i: (i, 0)), _resident((d, n), lambda i: (0, 0))]
        + [tab] * len(tables) + [_resident(c.shape, lambda i: (0, 0)) for c in consts],
        out_specs=pl.BlockSpec((tm, n_out), lambda i: (i, 0)),
        compiler_params=_params("parallel"),
        name=name,
    )(x, w, *tables, *consts)


def _retention_body(q_ref, k_ref, kd_ref, v_ref, g_ref, inner_ref, qd_ref, cd_ref, gn_ref,
                    h_ref, w_ref, *refs, n_chunks, proj_every):
    state_ref = refs[-1]
    o_ref, proj_ref = _side_casts(refs[:-1], 2)

    @pl.when(pl.program_id(2) == 0)
    def _():
        state_ref[...] = jnp.zeros_like(state_ref)

    inner_decay = inner_ref[...]
    q_decay = qd_ref[...]
    chunk_decay = cd_ref[...]
    gn = gn_ref[...]
    hb = h_ref[...].astype(BF16)
    proj_cols = proj_ref.shape[1] * proj_every // n_chunks
    for c in range(n_chunks):
        if c % proj_every == 0:
            cols = pl.ds((c // proj_every) * proj_cols, proj_cols)
            proj_ref[:, cols] = _dot(hb, w_ref[:, cols]).astype(BF16)
        rows = pl.ds(c * RET_CHUNK, RET_CHUNK)
        qn = q_ref[rows, :]
        vn = v_ref[rows, :]
        scores = lax.dot_general(qn, k_ref[rows, :], (((1,), (1,)), ((), ())),
                                 preferred_element_type=F32)
        inner = _dot((scores * inner_decay).astype(BF16), vn)
        state = state_ref[...]
        cross = _dot(qn, state.astype(BF16)) * q_decay
        state_ref[...] = state * chunk_decay + lax.dot_general(
            kd_ref[rows, :], vn, (((0,), (0,)), ((), ())), preferred_element_type=F32)
        o = inner + cross
        mu = jnp.mean(o, axis=-1, keepdims=True)
        d = o - mu
        var = jnp.mean(d * d, axis=-1, keepdims=True)
        y = d * lax.rsqrt(var + GN_EPS) * gn
        gate = g_ref[rows, :].astype(F32)
        o_ref[rows, :] = (gate * jax.nn.sigmoid(gate) * y).astype(BF16)


def _retention_decays():
    c = RET_CHUNK
    log_g = jnp.log1p(-jnp.exp2(-5.0 - jnp.arange(RET_HEADS, dtype=F32)))
    idx = jnp.arange(c, dtype=F32)
    q_decay = jnp.exp((idx[None, :] + 1.0) * log_g[:, None])
    k_decay = jnp.exp((c - 1.0 - idx)[None, :] * log_g[:, None])
    diff = idx[:, None] - idx[None, :]
    inner = jnp.where(diff[None] >= 0,
                      jnp.exp(jnp.maximum(diff, 0.0)[None] * log_g[:, None, None]), 0.0)
    chunk_decay = jnp.exp(c * log_g)
    q_decay = jnp.broadcast_to(q_decay[:, :, None], (RET_HEADS, c, RET_DV))
    k_decay = jnp.broadcast_to(k_decay.T[:, :, None], (c, RET_HEADS, RET_DK)).reshape(c, RET_QK)
    chunk_decay = jnp.broadcast_to(chunk_decay[:, None, None], (RET_HEADS, 1, RET_DV))
    return inner, q_decay, k_decay, chunk_decay


def _retention_and_proj(qkd, vg, inner, q_decay, chunk_decay, gn_g, h, w, batch, seq, casts=(), *,
                        ts=4096, proj_every=2):
    t, d = h.shape
    n_out = w.shape[1]
    ns = seq // ts
    tp = t // (batch * RET_HEADS * ns)
    gn = gn_g.reshape(RET_HEADS, 1, RET_DV)

    def rows(b, hd, s):
        return b * ns + s

    def tile(b, hd, s):
        return (b * RET_HEADS + hd) * ns + s

    head_tab = lambda shape: pl.BlockSpec((None,) + shape, lambda b, hd, s: (hd, 0, 0))
    cast_specs = _side_cast_specs(casts, batch * RET_HEADS * ns, tile)
    return pl.pallas_call(
        functools.partial(_retention_body, n_chunks=ts // RET_CHUNK, proj_every=proj_every),
        out_shape=[jax.ShapeDtypeStruct((t, RET_V), BF16), jax.ShapeDtypeStruct((t, n_out), BF16)]
        + [jax.ShapeDtypeStruct(a.shape, BF16) for a in casts],
        grid=(batch, RET_HEADS, ns),
        in_specs=[
            pl.BlockSpec((ts, RET_DK), lambda b, hd, s: (rows(b, hd, s), hd)),
            pl.BlockSpec((ts, RET_DK), lambda b, hd, s: (rows(b, hd, s), RET_HEADS + hd)),
            pl.BlockSpec((ts, RET_DK), lambda b, hd, s: (rows(b, hd, s), 2 * RET_HEADS + hd)),
            pl.BlockSpec((ts, RET_DV), lambda b, hd, s: (rows(b, hd, s), hd)),
            pl.BlockSpec((ts, RET_DV), lambda b, hd, s: (rows(b, hd, s), RET_HEADS + hd)),
            head_tab((RET_CHUNK, RET_CHUNK)),
            head_tab((RET_CHUNK, RET_DV)),
            head_tab((1, RET_DV)),
            head_tab((1, RET_DV)),
            pl.BlockSpec((tp, d), lambda b, hd, s: (tile(b, hd, s), 0)),
            _resident((d, n_out), lambda b, hd, s: (0, 0)),
        ] + cast_specs,
        out_specs=[pl.BlockSpec((ts, RET_DV), lambda b, hd, s: (rows(b, hd, s), hd)),
                   pl.BlockSpec((tp, n_out), lambda b, hd, s: (tile(b, hd, s), 0))] + cast_specs,
        scratch_shapes=[pltpu.VMEM((RET_DK, RET_DV), F32)],
        compiler_params=_params("parallel", "parallel", "arbitrary"),
        name="retention_proj",
    )(qkd, qkd, qkd, vg, vg, inner, q_decay, chunk_decay, gn, h, w, *casts)


def _swa_body(sink_ref, q_ref, kvc_ref, kvp_ref, h_ref, w_ref, *refs, n_blocks):
    o_ref, proj_ref = _side_casts(refs, 2)
    n = pl.program_id(1)
    c = ATT_BLOCK
    qi = lax.broadcasted_iota(jnp.int32, (c, 2 * c), 0) + c
    kj = lax.broadcasted_iota(jnp.int32, (c, 2 * c), 1)
    rel = qi - kj
    in_window = (rel >= 0) & (rel < WINDOW)
    valid_first = in_window & (kj >= jnp.where(n > 0, 0, c))
    scale = ATT_DH ** -0.5

    def heads_of(hk):
        return [hk * ATT_GROUP + g for g in range(ATT_GROUP)]

    def band(qb, hk, offset):
        cols = slice(offset + hk * ATT_DH, offset + (hk + 1) * ATT_DH)
        prev = kvp_ref[:, cols] if qb == 0 else kvc_ref[(qb - 1) * c:qb * c, cols]
        return jnp.concatenate([prev, kvc_ref[qb * c:(qb + 1) * c, cols]], axis=0)

    def scores(qb, hk):
        q4 = jnp.concatenate([q_ref[qb * c:(qb + 1) * c, hd * ATT_DH:(hd + 1) * ATT_DH]
                              for hd in heads_of(hk)], axis=0)
        return lax.dot_general(q4 * scale, band(qb, hk, 0), (((1,), (1,)), ((), ())),
                               preferred_element_type=F32)

    def softmax(qb, hk, s4):
        valid = valid_first if qb == 0 else in_window
        masked = [jnp.where(valid, s4[g * c:(g + 1) * c], -jnp.inf) for g in range(ATT_GROUP)]
        sinks = [sink_ref[hd] for hd in heads_of(hk)]
        maxes = [jnp.maximum(jnp.max(s, axis=-1, keepdims=True), sink)
                 for s, sink in zip(masked, sinks)]
        probs = [jnp.exp(s - m).astype(BF16) for s, m in zip(masked, maxes)]
        sink_terms = [jnp.exp(sink - m) for sink, m in zip(sinks, maxes)]
        return jnp.concatenate(probs, axis=0), sink_terms

    ones = jnp.ones((2 * c, 2 * c - ATT_DH), BF16)

    def weighted_values(qb, hk, e4):
        return _dot(e4, jnp.concatenate([band(qb, hk, ATT_KV), ones], axis=1))

    def store(qb, hk, ov, sink_terms):
        groups = [ov[g * c:(g + 1) * c] for g in range(ATT_GROUP)]
        inv = [1.0 / (og[:, c:c + ATT_DH] + st) for og, st in zip(groups, sink_terms)]
        outs = [og[:, :ATT_DH] * r for og, r in zip(groups, inv)]
        for pair in range(ATT_GROUP // 2):
            col = heads_of(hk)[2 * pair] * ATT_DH
            o_ref[qb * c:(qb + 1) * c, col:col + 2 * ATT_DH] = jnp.concatenate(
                [outs[2 * pair], outs[2 * pair + 1]], axis=1).astype(BF16)

    units = [(qb, hk) for hk in range(ATT_KV_HEADS) for qb in range(n_blocks)]
    hb = h_ref[...].astype(BF16)
    chunk = proj_ref.shape[1] // len(units)

    def project(u):
        cols = slice(u * chunk, (u + 1) * chunk)
        proj_ref[:, cols] = _dot(hb, w_ref[:, cols]).astype(BF16)

    s_next = scores(*units[0])
    pending = None
    for u, unit in enumerate(units):
        s_cur = s_next
        if u + 1 < len(units):
            s_next = scores(*units[u + 1])
        project(u)
        e4, sink_terms = softmax(*unit, s_cur)
        if pending is not None:
            store(*pending)
        pending = (*unit, weighted_values(*unit, e4), sink_terms)
    store(*pending)


def _swa_and_proj(aqkv, sinks, h, w, batch, seq, casts=(), *, n_blocks=4):
    t, d = h.shape
    n_out = w.shape[1]
    tq = n_blocks * ATT_BLOCK
    nq = seq // tq
    kv_col = ATT_Q // (2 * ATT_KV)
    cast_specs = _side_cast_specs(casts, batch * nq, lambda b, n: b * nq + n)
    return pl.pallas_call(
        functools.partial(_swa_body, n_blocks=n_blocks),
        out_shape=[jax.ShapeDtypeStruct((t, ATT_Q), BF16), jax.ShapeDtypeStruct((t, n_out), BF16)]
        + [jax.ShapeDtypeStruct(a.shape, BF16) for a in casts],
        grid=(batch, nq),
        in_specs=[
            pl.BlockSpec(memory_space=pltpu.SMEM),
            pl.BlockSpec((tq, ATT_Q), lambda b, n: (b * nq + n, 0)),
            pl.BlockSpec((tq, 2 * ATT_KV), lambda b, n: (b * nq + n, kv_col)),
            pl.BlockSpec((ATT_BLOCK, 2 * ATT_KV),
                         lambda b, n: ((b * nq + n) * n_blocks - jnp.minimum(n, 1), kv_col)),
            pl.BlockSpec((tq, d), lambda b, n: (b * nq + n, 0)),
            _resident((d, n_out), lambda b, n: (0, 0)),
        ] + cast_specs,
        out_specs=[pl.BlockSpec((tq, ATT_Q), lambda b, n: (b * nq + n, 0)),
                   pl.BlockSpec((tq, n_out), lambda b, n: (b * nq + n, 0))] + cast_specs,
        compiler_params=_params("parallel", "arbitrary"),
        name="swa_proj",
    )(sinks, aqkv, aqkv, aqkv, h, w, *casts)


def _mix_body(h_ref, ret_ref, att_ref, gr_ref, ga_ref, wr_ref, wa_ref, wm_ref, g_ref, b_ref,
              o_ref, *, alpha):
    ret_branch = _dot(ret_ref[...], wr_ref[...])
    att_branch = _dot(att_ref[...], wa_ref[...])
    merged = (jax.nn.sigmoid(gr_ref[...].astype(F32)) * ret_branch
              + jax.nn.sigmoid(ga_ref[...].astype(F32)) * att_branch)
    mixed = _dot(merged.astype(BF16), wm_ref[...])
    o_ref[...] = _layer_norm(alpha * h_ref[...] + mixed, g_ref[...], b_ref[...])


def _mix_ln(h, ret, att, gates, w_ret, w_att, w_mix, g, b, *, alpha, tm=512):
    t, d = h.shape
    row = pl.BlockSpec((1, d), lambda i: (0, 0))
    return pl.pallas_call(
        functools.partial(_mix_body, alpha=alpha),
        out_shape=jax.ShapeDtypeStruct((t, d), F32),
        grid=(t // tm,),
        in_specs=[
            pl.BlockSpec((tm, d), lambda i: (i, 0)),
            pl.BlockSpec((tm, RET_V), lambda i: (i, 0)),
            pl.BlockSpec((tm, ATT_Q), lambda i: (i, 0)),
            pl.BlockSpec((tm, d), lambda i: (i, 0)),
            pl.BlockSpec((tm, d), lambda i: (i, 1)),
            _resident((RET_V, d), lambda i: (0, 0)),
            _resident((ATT_Q, d), lambda i: (0, 0)),
            _resident((d, d), lambda i: (0, 0)),
            row, row,
        ],
        out_specs=pl.BlockSpec((tm, d), lambda i: (i, 0)),
        compiler_params=_params("parallel"),
        name="mix_ln",
    )(h, ret, att, gates, gates, w_ret, w_att, w_mix, g, b)


def _ple_body(h_ref, p_ref, wg_ref, wp_ref, g_ref, b_ref, o_ref, *, alpha, n_sub):
    sub = h_ref.shape[0] // n_sub
    for r0 in range(0, h_ref.shape[0], sub):
        h = h_ref[r0:r0 + sub, :]
        gate = jax.nn.sigmoid(_dot(h.astype(BF16), wg_ref[...]))
        ple = gate * _dot(p_ref[r0:r0 + sub, :].astype(BF16), wp_ref[...])
        o_ref[r0:r0 + sub, :] = _layer_norm(alpha * h + ple, g_ref[...], b_ref[...])


def _ple_ln(h, p, w_gate, w_proj, g, b, *, alpha, tm=1024, n_sub=2):
    t, d = h.shape
    pd = p.shape[1]
    row = pl.BlockSpec((1, d), lambda i: (0, 0))
    return pl.pallas_call(
        functools.partial(_ple_body, alpha=alpha, n_sub=n_sub),
        out_shape=jax.ShapeDtypeStruct((t, d), F32),
        grid=(t // tm,),
        in_specs=[
            pl.BlockSpec((tm, d), lambda i: (i, 0)),
            pl.BlockSpec((tm, pd), lambda i: (i, 0)),
            _resident((d, d), lambda i: (0, 0)),
            _resident((pd, d), lambda i: (0, 0)),
            row, row,
        ],
        out_specs=pl.BlockSpec((tm, d), lambda i: (i, 0)),
        compiler_params=_params("parallel"),
        name="ple_ln",
    )(h, p, w_gate, w_proj, g, b)


def kernel(x, p, positions, ln_g, ln_b, w_ffn1_gu, w_ffn1_down, w_in, ret_gn_g, att_sinks,
           w_ret_out, w_att_out, w_mix_out, w_ffn2_gu, w_ffn2_down, w_ple_gate, w_ple_proj):
    batch, seq, d = x.shape
    depth = ln_g.shape[0]
    t = batch * seq
    alpha = (2.0 * depth) ** 0.25
    h = x.reshape(t, d)
    cr, sr, ca, s1, s2 = _rope_tables(positions.reshape(t, 1))
    inner_decay, q_decay, k_decay, chunk_decay = _retention_decays()
    bounds = [0]
    for size in IN_SIZES:
        bounds.append(bounds[-1] + size)
    for i in range(depth):
        g = ln_g[i][:, None, :]
        b = ln_b[i][:, None, :]
        w_qk = w_in[i][:, bounds[0]:bounds[2]].astype(BF16)
        w_vg = w_in[i][:, bounds[2]:bounds[4]].astype(BF16)
        w_att = w_in[i][:, bounds[4]:bounds[7]].astype(BF16)
        w_gates = w_in[i][:, bounds[7]:bounds[9]].astype(BF16)

        h = _ffn_ln(h, w_ffn1_gu[i].astype(BF16), w_ffn1_down[i].astype(BF16), g[0], b[0], alpha=alpha)

        qkd = _proj(functools.partial(_proj_ret_qk_body, n_sub=2), h, w_qk, (cr, sr), (k_decay,),
                    n_out=3 * RET_QK, tm=1024, name="proj_ret_qk")
        aqkv = _proj(functools.partial(_proj_att_body, n_sub=2), h, w_att, (ca, s1, s2),
                     tm=1024, name="proj_att")
        att, vg, w2_gu, w2_down = _swa_and_proj(
            aqkv, att_sinks[i], h, w_vg, batch, seq, (w_ffn2_gu[i], w_ffn2_down[i]))
        ret, gates, w_ro, w_ao, w_mo, w_pg = _retention_and_proj(
            qkd, vg, inner_decay, q_decay, chunk_decay, ret_gn_g[i], h, w_gates, batch, seq,
            (w_ret_out[i], w_att_out[i], w_mix_out[i], w_ple_gate[i]))

        h = _mix_ln(h, ret, att, gates, w_ro, w_ao, w_mo, g[1], b[1], alpha=alpha)
        h = _ffn_ln(h, w2_gu, w2_down, g[2], b[2], alpha=alpha)
        h = _ple_ln(h, p[i].reshape(t, PLE_DIM), w_pg, w_ple_proj[i].astype(BF16), g[3], b[3],
                    alpha=alpha)
    return h.reshape(batch, seq, d)
```

```python
import functools

import jax
import jax.numpy as jnp
from jax import lax
from jax.experimental import pallas as pl
from jax.experimental.pallas import tpu as pltpu

F32 = jnp.float32
BF16 = jnp.bfloat16

D_MODEL = 2048
PLE_DIM = 256
D_FF = 5632
RET_HEADS = 8
RET_DK = 128
RET_DV = 256
RET_CHUNK = 128
RET_THETA = 10000.0
ATT_HEADS = 16
ATT_KV_HEADS = 4
ATT_DH = 64
ATT_GROUP = ATT_HEADS // ATT_KV_HEADS
WINDOW = 128
ATT_BLOCK = 128
ROPE_THETA = 500000.0
ROPE_DIMS = ATT_DH // 4
LN_EPS = 1e-5
GN_EPS = 1e-5

RET_QK = RET_HEADS * RET_DK
RET_V = RET_HEADS * RET_DV
ATT_Q = ATT_HEADS * ATT_DH
ATT_KV = ATT_KV_HEADS * ATT_DH
IN_SIZES = (RET_QK, RET_QK, RET_V, RET_V, ATT_Q, ATT_KV, ATT_KV, D_MODEL, D_MODEL)

LANES = 128
VMEM_LIMIT_BYTES = 60 * 1024 * 1024


def _params(*semantics):
    return pltpu.CompilerParams(dimension_semantics=semantics, vmem_limit_bytes=VMEM_LIMIT_BYTES)


def _resident(shape, index_map):
    return pl.BlockSpec(shape, index_map, pipeline_mode=pl.Buffered(1))


def _side_cast_specs(arrays, n_steps, step_of):
    specs = []
    for a in arrays:
        r, c = a.shape
        gc = next(k for k in (1, 2, 4, 8) if (r * k) % (16 * n_steps) == 0 and c % (k * LANES) == 0)
        gr = n_steps // gc
        specs.append(pl.BlockSpec(
            (r // gr, c // gc), lambda *idx, gc=gc: (step_of(*idx) // gc, step_of(*idx) % gc)))
    return specs


def _side_casts(refs, n_fixed_out):
    n = (len(refs) - n_fixed_out) // 2
    for src, dst in zip(refs[:n], refs[n + n_fixed_out:]):
        dst[...] = src[...].astype(BF16)
    return refs[n:n + n_fixed_out]


def _layer_norm(y, g, b):
    mu = jnp.mean(y, axis=-1, keepdims=True)
    d = y - mu
    var = jnp.mean(d * d, axis=-1, keepdims=True)
    return d * lax.rsqrt(var + LN_EPS) * g + b


def _dot(a, b):
    return jnp.dot(a, b, preferred_element_type=F32)


def _ffn_body(x_ref, wg_ref, wu_ref, wd_ref, g_ref, b_ref, o_ref, xb_ref, *, alpha, n_sub):
    j = pl.program_id(1)
    sub = x_ref.shape[0] // n_sub

    @pl.when(j == 0)
    def _():
        xb_ref[...] = x_ref[...].astype(BF16)
        o_ref[...] = jnp.zeros_like(o_ref)

    for r in range(n_sub):
        rows = pl.ds(r * sub, sub)
        xb = xb_ref[rows, :]
        gate = _dot(xb, wg_ref[...])
        up = _dot(xb, wu_ref[...])
        act = (gate * jax.nn.sigmoid(gate) * up).astype(BF16)
        o_ref[rows, :] += _dot(act, wd_ref[...])

    @pl.when(j == pl.num_programs(1) - 1)
    def _():
        for r in range(n_sub):
            rows = pl.ds(r * sub, sub)
            y = alpha * x_ref[rows, :] + 0.5 * o_ref[rows, :]
            o_ref[rows, :] = _layer_norm(y, g_ref[...], b_ref[...])


def _ffn_ln(x, w_gu, w_down, g, b, *, alpha, tm=1024, tf=512, n_sub=2):
    t, d = x.shape
    f = w_down.shape[0]
    nj = f // tf
    return pl.pallas_call(
        functools.partial(_ffn_body, alpha=alpha, n_sub=n_sub),
        out_shape=jax.ShapeDtypeStruct((t, d), F32),
        grid=(t // tm, nj),
        in_specs=[
            pl.BlockSpec((tm, d), lambda i, j: (i, 0)),
            pl.BlockSpec((d, tf), lambda i, j: (0, j)),
            pl.BlockSpec((d, tf), lambda i, j: (0, j + nj)),
            pl.BlockSpec((tf, d), lambda i, j: (j, 0)),
            pl.BlockSpec((1, d), lambda i, j: (0, 0)),
            pl.BlockSpec((1, d), lambda i, j: (0, 0)),
        ],
        out_specs=pl.BlockSpec((tm, d), lambda i, j: (i, 0)),
        scratch_shapes=[pltpu.VMEM((tm, d), BF16)],
        compiler_params=_params("parallel", "arbitrary"),
        name="ffn_ln",
    )(x, w_gu, w_gu, w_down, g, b)


def _rope_body(pos_ref, inv_ref, w_in_ref, *refs, in_bounds):
    n_seg = len(in_bounds) - 1
    cr_ref, sr_ref, ca_ref, s1_ref, s2_ref, *seg_refs = _side_casts(refs, 5 + n_seg)
    for seg_ref, lo, hi in zip(seg_refs, in_bounds[:-1], in_bounds[1:]):
        seg_ref[...] = w_in_ref[:, lo:hi].astype(BF16)
    pos = pos_ref[...].astype(F32)
    ang = pos * inv_ref[...]
    c = jnp.cos(ang)
    s = jnp.sin(ang)
    half_r = RET_DK // 2
    half_a = ROPE_DIMS // 2
    lane = lax.broadcasted_iota(jnp.int32, c.shape, 1)
    low = lane < half_r
    c_swap = pltpu.roll(c, half_r, axis=1)
    s_swap = pltpu.roll(s, half_r, axis=1)
    cr_ref[...] = jnp.where(low, c, c_swap)
    sr_ref[...] = jnp.where(low, -s, s_swap)
    c_att = jnp.where(low, c_swap, c)
    s_att = jnp.where(low, s_swap, s)
    in_head = lane & (ATT_DH - 1)
    first = in_head < half_a
    second = (in_head >= half_a) & (in_head < ROPE_DIMS)
    ca_ref[...] = jnp.where(first, c_att, jnp.where(second, pltpu.roll(c_att, half_a, axis=1), 1.0))
    s1_ref[...] = jnp.where(second, pltpu.roll(s_att, half_a, axis=1), 0.0)
    s2_ref[...] = jnp.where(first, -s_att, 0.0)


def _rope_tables(pos, w_in, in_bounds, casts=(), tm=1024):
    t = pos.shape[0]
    n_steps = t // tm
    d_in, n_in = w_in.shape
    w_rows = d_in // n_steps
    widths = [hi - lo for lo, hi in zip(in_bounds[:-1], in_bounds[1:])]
    cast_specs = _side_cast_specs(casts, n_steps, lambda i: i)
    half_r = RET_DK // 2
    half_a = ROPE_DIMS // 2
    inv_r = 1.0 / (RET_THETA ** (jnp.arange(half_r, dtype=F32) / half_r))
    inv_a = 1.0 / (ROPE_THETA ** (jnp.arange(half_a, dtype=F32) / half_a))
    inv = jnp.concatenate([inv_r, inv_a, jnp.zeros((LANES - half_r - half_a,), F32)])[None, :]
    tab = pl.BlockSpec((tm, LANES), lambda i: (i, 0))
    return pl.pallas_call(
        functools.partial(_rope_body, in_bounds=tuple(in_bounds)),
        out_shape=[jax.ShapeDtypeStruct((t, LANES), F32)] * 5
        + [jax.ShapeDtypeStruct((d_in, w), BF16) for w in widths]
        + [jax.ShapeDtypeStruct(a.shape, BF16) for a in casts],
        grid=(n_steps,),
        in_specs=[pl.BlockSpec((tm, 1), lambda i: (i, 0)),
                  pl.BlockSpec((1, LANES), lambda i: (0, 0)),
                  pl.BlockSpec((w_rows, n_in), lambda i: (i, 0))] + cast_specs,
        out_specs=[tab] * 5 + [pl.BlockSpec((w_rows, w), lambda i: (i, 0)) for w in widths]
        + cast_specs,
        compiler_params=_params("parallel"),
        name="rope_tables",
    )(pos, inv, w_in, *casts)


def _proj_ret_qk_body(x_ref, w_ref, cr_ref, sr_ref, kdec_ref, o_ref, *, n_sub):
    sub = x_ref.shape[0] // n_sub
    k_scale = RET_DK ** -0.5
    for r0 in range(0, x_ref.shape[0], sub):
        r = _dot(x_ref[r0:r0 + sub, :].astype(BF16), w_ref[...])
        cr = cr_ref[r0:r0 + sub, :]
        sr = sr_ref[r0:r0 + sub, :]
        for s in range(2 * RET_HEADS):
            x = r[:, s * LANES:(s + 1) * LANES]
            rot = x * cr + pltpu.roll(x, RET_DK // 2, axis=1) * sr
            if s < RET_HEADS:
                o_ref[r0:r0 + sub, s * LANES:(s + 1) * LANES] = rot.astype(BF16)
                continue
            rot = rot * k_scale
            o_ref[r0:r0 + sub, s * LANES:(s + 1) * LANES] = rot.astype(BF16)
            k_decay = kdec_ref[:, (s - RET_HEADS) * LANES:(s - RET_HEADS + 1) * LANES]
            d = s + RET_HEADS
            for c0 in range(0, sub, RET_CHUNK):
                o_ref[r0 + c0:r0 + c0 + RET_CHUNK, d * LANES:(d + 1) * LANES] = (
                    rot[c0:c0 + RET_CHUNK] * k_decay).astype(BF16)


def _proj_att_body(x_ref, w_ref, ca_ref, s1_ref, s2_ref, o_ref, *, n_sub):
    sub = x_ref.shape[0] // n_sub
    half = ROPE_DIMS // 2
    n_rot = (ATT_Q + ATT_KV) // LANES
    for r0 in range(0, x_ref.shape[0], sub):
        r = _dot(x_ref[r0:r0 + sub, :].astype(BF16), w_ref[...])
        ca = ca_ref[r0:r0 + sub, :]
        s1 = s1_ref[r0:r0 + sub, :]
        s2 = s2_ref[r0:r0 + sub, :]
        for s in range(n_rot):
            x = r[:, s * LANES:(s + 1) * LANES]
            rot = (x * ca + pltpu.roll(x, half, axis=1) * s1
                   + pltpu.roll(x, LANES - half, axis=1) * s2)
            o_ref[r0:r0 + sub, s * LANES:(s + 1) * LANES] = rot.astype(BF16)
        o_ref[r0:r0 + sub, n_rot * LANES:] = r[:, n_rot * LANES:].astype(BF16)


def _proj(body, x, w, tables=(), consts=(), *, n_out=None, tm=512, name):
    t, d = x.shape
    n = w.shape[1]
    n_out = n if n_out is None else n_out
    tab = pl.BlockSpec((tm, LANES), lambda i: (i, 0))
    return pl.pallas_call(
        body,
        out_shape=jax.ShapeDtypeStruct((t, n_out), BF16),
        grid=(t // tm,),
        in_specs=[pl.BlockSpec((tm, d), lambda i: (i, 0)), _resident((d, n), lambda i: (0, 0))]
        + [tab] * len(tables) + [_resident(c.shape, lambda i: (0, 0)) for c in consts],
        out_specs=pl.BlockSpec((tm, n_out), lambda i: (i, 0)),
        compiler_params=_params("parallel"),
        name=name,
    )(x, w, *tables, *consts)


def _retention_body(q_ref, k_ref, kd_ref, v_ref, g_ref, inner_ref, qd_ref, cd_ref, gn_ref,
                    h_ref, w_ref, *refs, n_chunks, proj_every):
    state_ref = refs[-1]
    o_ref, proj_ref = _side_casts(refs[:-1], 2)

    @pl.when(pl.program_id(2) == 0)
    def _():
        state_ref[...] = jnp.zeros_like(state_ref)

    inner_decay = inner_ref[...]
    q_decay = qd_ref[...]
    chunk_decay = cd_ref[...]
    gn = gn_ref[...]
    hb = h_ref[...].astype(BF16)
    proj_cols = proj_ref.shape[1] * proj_every // n_chunks
    for c in range(n_chunks):
        if c % proj_every == 0:
            cols = pl.ds((c // proj_every) * proj_cols, proj_cols)
            proj_ref[:, cols] = _dot(hb, w_ref[:, cols]).astype(BF16)
        rows = pl.ds(c * RET_CHUNK, RET_CHUNK)
        qn = q_ref[rows, :]
        vn = v_ref[rows, :]
        scores = lax.dot_general(qn, k_ref[rows, :], (((1,), (1,)), ((), ())),
                                 preferred_element_type=F32)
        inner = _dot((scores * inner_decay).astype(BF16), vn)
        state = state_ref[...]
        cross = _dot(qn, state.astype(BF16)) * q_decay
        state_ref[...] = state * chunk_decay + lax.dot_general(
            kd_ref[rows, :], vn, (((0,), (0,)), ((), ())), preferred_element_type=F32)
        o = inner + cross
        mu = jnp.mean(o, axis=-1, keepdims=True)
        d = o - mu
        var = jnp.mean(d * d, axis=-1, keepdims=True)
        y = d * lax.rsqrt(var + GN_EPS) * gn
        gate = g_ref[rows, :].astype(F32)
        o_ref[rows, :] = (gate * jax.nn.sigmoid(gate) * y).astype(BF16)


def _retention_decays():
    c = RET_CHUNK
    log_g = jnp.log1p(-jnp.exp2(-5.0 - jnp.arange(RET_HEADS, dtype=F32)))
    idx = jnp.arange(c, dtype=F32)
    q_decay = jnp.exp((idx[None, :] + 1.0) * log_g[:, None])
    k_decay = jnp.exp((c - 1.0 - idx)[None, :] * log_g[:, None])
    diff = idx[:, None] - idx[None, :]
    inner = jnp.where(diff[None] >= 0,
                      jnp.exp(jnp.maximum(diff, 0.0)[None] * log_g[:, None, None]), 0.0)
    chunk_decay = jnp.exp(c * log_g)
    q_decay = jnp.broadcast_to(q_decay[:, :, None], (RET_HEADS, c, RET_DV))
    k_decay = jnp.broadcast_to(k_decay.T[:, :, None], (c, RET_HEADS, RET_DK)).reshape(c, RET_QK)
    chunk_decay = jnp.broadcast_to(chunk_decay[:, None, None], (RET_HEADS, 1, RET_DV))
    return inner, q_decay, k_decay, chunk_decay


def _retention_and_proj(qkd, vg, inner, q_decay, chunk_decay, gn_g, h, w, batch, seq, casts=(), *,
                        ts=4096, proj_every=2):
    t, d = h.shape
    n_out = w.shape[1]
    ns = seq // ts
    tp = t // (batch * RET_HEADS * ns)
    gn = gn_g.reshape(RET_HEADS, 1, RET_DV)

    def rows(b, hd, s):
        return b * ns + s

    def tile(b, hd, s):
        return (b * RET_HEADS + hd) * ns + s

    head_tab = lambda shape: pl.BlockSpec((None,) + shape, lambda b, hd, s: (hd, 0, 0))
    cast_specs = _side_cast_specs(casts, batch * RET_HEADS * ns, tile)
    return pl.pallas_call(
        functools.partial(_retention_body, n_chunks=ts // RET_CHUNK, proj_every=proj_every),
        out_shape=[jax.ShapeDtypeStruct((t, RET_V), BF16), jax.ShapeDtypeStruct((t, n_out), BF16)]
        + [jax.ShapeDtypeStruct(a.shape, BF16) for a in casts],
        grid=(batch, RET_HEADS, ns),
        in_specs=[
            pl.BlockSpec((ts, RET_DK), lambda b, hd, s: (rows(b, hd, s), hd)),
            pl.BlockSpec((ts, RET_DK), lambda b, hd, s: (rows(b, hd, s), RET_HEADS + hd)),
            pl.BlockSpec((ts, RET_DK), lambda b, hd, s: (rows(b, hd, s), 2 * RET_HEADS + hd)),
            pl.BlockSpec((ts, RET_DV), lambda b, hd, s: (rows(b, hd, s), hd)),
            pl.BlockSpec((ts, RET_DV), lambda b, hd, s: (rows(b, hd, s), RET_HEADS + hd)),
            head_tab((RET_CHUNK, RET_CHUNK)),
            head_tab((RET_CHUNK, RET_DV)),
            head_tab((1, RET_DV)),
            head_tab((1, RET_DV)),
            pl.BlockSpec((tp, d), lambda b, hd, s: (tile(b, hd, s), 0)),
            _resident((d, n_out), lambda b, hd, s: (0, 0)),
        ] + cast_specs,
        out_specs=[pl.BlockSpec((ts, RET_DV), lambda b, hd, s: (rows(b, hd, s), hd)),
                   pl.BlockSpec((tp, n_out), lambda b, hd, s: (tile(b, hd, s), 0))] + cast_specs,
        scratch_shapes=[pltpu.VMEM((RET_DK, RET_DV), F32)],
        compiler_params=_params("parallel", "parallel", "arbitrary"),
        name="retention_proj",
    )(qkd, qkd, qkd, vg, vg, inner, q_decay, chunk_decay, gn, h, w, *casts)


def _swa_body(sink_ref, q_ref, kvc_ref, kvp_ref, h_ref, w_ref, *refs, n_blocks):
    o_ref, proj_ref = _side_casts(refs, 2)
    n = pl.program_id(1)
    c = ATT_BLOCK
    qi = lax.broadcasted_iota(jnp.int32, (c, 2 * c), 0) + c
    kj = lax.broadcasted_iota(jnp.int32, (c, 2 * c), 1)
    rel = qi - kj
    in_window = (rel >= 0) & (rel < WINDOW)
    valid_first = in_window & (kj >= jnp.where(n > 0, 0, c))
    scale = ATT_DH ** -0.5

    def heads_of(hk):
        return [hk * ATT_GROUP + g for g in range(ATT_GROUP)]

    def band(qb, hk, offset):
        cols = slice(offset + hk * ATT_DH, offset + (hk + 1) * ATT_DH)
        prev = kvp_ref[:, cols] if qb == 0 else kvc_ref[(qb - 1) * c:qb * c, cols]
        return jnp.concatenate([prev, kvc_ref[qb * c:(qb + 1) * c, cols]], axis=0)

    def scores(qb, hk):
        q4 = jnp.concatenate([q_ref[qb * c:(qb + 1) * c, hd * ATT_DH:(hd + 1) * ATT_DH]
                              for hd in heads_of(hk)], axis=0)
        return lax.dot_general(q4 * scale, band(qb, hk, 0), (((1,), (1,)), ((), ())),
                               preferred_element_type=F32)

    def softmax(qb, hk, s4):
        valid = valid_first if qb == 0 else in_window
        masked = [jnp.where(valid, s4[g * c:(g + 1) * c], -jnp.inf) for g in range(ATT_GROUP)]
        sinks = [sink_ref[hd] for hd in heads_of(hk)]
        maxes = [jnp.maximum(jnp.max(s, axis=-1, keepdims=True), sink)
                 for s, sink in zip(masked, sinks)]
        probs = [jnp.exp(s - m).astype(BF16) for s, m in zip(masked, maxes)]
        sink_terms = [jnp.exp(sink - m) for sink, m in zip(sinks, maxes)]
        return jnp.concatenate(probs, axis=0), sink_terms

    ones = jnp.ones((2 * c, 2 * c - ATT_DH), BF16)

    def weighted_values(qb, hk, e4):
        return _dot(e4, jnp.concatenate([band(qb, hk, ATT_KV), ones], axis=1))

    def store(qb, hk, ov, sink_terms):
        groups = [ov[g * c:(g + 1) * c] for g in range(ATT_GROUP)]
        inv = [1.0 / (og[:, c:c + ATT_DH] + st) for og, st in zip(groups, sink_terms)]
        outs = [og[:, :ATT_DH] * r for og, r in zip(groups, inv)]
        for pair in range(ATT_GROUP // 2):
            col = heads_of(hk)[2 * pair] * ATT_DH
            o_ref[qb * c:(qb + 1) * c, col:col + 2 * ATT_DH] = jnp.concatenate(
                [outs[2 * pair], outs[2 * pair + 1]], axis=1).astype(BF16)

    units = [(qb, hk) for hk in range(ATT_KV_HEADS) for qb in range(n_blocks)]
    hb = h_ref[...].astype(BF16)
    chunk = proj_ref.shape[1] // len(units)

    def project(u):
        cols = slice(u * chunk, (u + 1) * chunk)
        proj_ref[:, cols] = _dot(hb, w_ref[:, cols]).astype(BF16)

    s_next = scores(*units[0])
    pending = None
    for u, unit in enumerate(units):
        s_cur = s_next
        if u + 1 < len(units):
            s_next = scores(*units[u + 1])
        project(u)
        e4, sink_terms = softmax(*unit, s_cur)
        if pending is not None:
            store(*pending)
        pending = (*unit, weighted_values(*unit, e4), sink_terms)
    store(*pending)


def _swa_and_proj(aqkv, sinks, h, w, batch, seq, casts=(), *, n_blocks=4):
    t, d = h.shape
    n_out = w.shape[1]
    tq = n_blocks * ATT_BLOCK
    nq = seq // tq
    kv_col = ATT_Q // (2 * ATT_KV)
    cast_specs = _side_cast_specs(casts, batch * nq, lambda b, n: b * nq + n)
    return pl.pallas_call(
        functools.partial(_swa_body, n_blocks=n_blocks),
        out_shape=[jax.ShapeDtypeStruct((t, ATT_Q), BF16), jax.ShapeDtypeStruct((t, n_out), BF16)]
        + [jax.ShapeDtypeStruct(a.shape, BF16) for a in casts],
        grid=(batch, nq),
        in_specs=[
            pl.BlockSpec(memory_space=pltpu.SMEM),
            pl.BlockSpec((tq, ATT_Q), lambda b, n: (b * nq + n, 0)),
            pl.BlockSpec((tq, 2 * ATT_KV), lambda b, n: (b * nq + n, kv_col)),
            pl.BlockSpec((ATT_BLOCK, 2 * ATT_KV),
                         lambda b, n: ((b * nq + n) * n_blocks - jnp.minimum(n, 1), kv_col)),
            pl.BlockSpec((tq, d), lambda b, n: (b * nq + n, 0)),
            _resident((d, n_out), lambda b, n: (0, 0)),
        ] + cast_specs,
        out_specs=[pl.BlockSpec((tq, ATT_Q), lambda b, n: (b * nq + n, 0)),
                   pl.BlockSpec((tq, n_out), lambda b, n: (b * nq + n, 0))] + cast_specs,
        compiler_params=_params("parallel", "arbitrary"),
        name="swa_proj",
    )(sinks, aqkv, aqkv, aqkv, h, w, *casts)


def _mix_body(h_ref, ret_ref, att_ref, gr_ref, ga_ref, wr_ref, wa_ref, wm_ref, g_ref, b_ref,
              o_ref, *, alpha):
    ret_branch = _dot(ret_ref[...], wr_ref[...])
    att_branch = _dot(att_ref[...], wa_ref[...])
    merged = (jax.nn.sigmoid(gr_ref[...].astype(F32)) * ret_branch
              + jax.nn.sigmoid(ga_ref[...].astype(F32)) * att_branch)
    mixed = _dot(merged.astype(BF16), wm_ref[...])
    o_ref[...] = _layer_norm(alpha * h_ref[...] + mixed, g_ref[...], b_ref[...])


def _mix_ln(h, ret, att, gates, w_ret, w_att, w_mix, g, b, *, alpha, tm=512):
    t, d = h.shape
    row = pl.BlockSpec((1, d), lambda i: (0, 0))
    return pl.pallas_call(
        functools.partial(_mix_body, alpha=alpha),
        out_shape=jax.ShapeDtypeStruct((t, d), F32),
        grid=(t // tm,),
        in_specs=[
            pl.BlockSpec((tm, d), lambda i: (i, 0)),
            pl.BlockSpec((tm, RET_V), lambda i: (i, 0)),
            pl.BlockSpec((tm, ATT_Q), lambda i: (i, 0)),
            pl.BlockSpec((tm, d), lambda i: (i, 0)),
            pl.BlockSpec((tm, d), lambda i: (i, 1)),
            _resident((RET_V, d), lambda i: (0, 0)),
            _resident((ATT_Q, d), lambda i: (0, 0)),
            _resident((d, d), lambda i: (0, 0)),
            row, row,
        ],
        out_specs=pl.BlockSpec((tm, d), lambda i: (i, 0)),
        compiler_params=_params("parallel"),
        name="mix_ln",
    )(h, ret, att, gates, gates, w_ret, w_att, w_mix, g, b)


def _ple_body(h_ref, p_ref, wg_ref, wp_ref, g_ref, b_ref, o_ref, *, alpha, n_sub):
    sub = h_ref.shape[0] // n_sub
    for r0 in range(0, h_ref.shape[0], sub):
        h = h_ref[r0:r0 + sub, :]
        gate = jax.nn.sigmoid(_dot(h.astype(BF16), wg_ref[...]))
        ple = gate * _dot(p_ref[r0:r0 + sub, :].astype(BF16), wp_ref[...])
        o_ref[r0:r0 + sub, :] = _layer_norm(alpha * h + ple, g_ref[...], b_ref[...])


def _ple_ln(h, p, w_gate, w_proj, g, b, *, alpha, tm=1024, n_sub=2):
    t, d = h.shape
    pd = p.shape[1]
    row = pl.BlockSpec((1, d), lambda i: (0, 0))
    return pl.pallas_call(
        functools.partial(_ple_body, alpha=alpha, n_sub=n_sub),
        out_shape=jax.ShapeDtypeStruct((t, d), F32),
        grid=(t // tm,),
        in_specs=[
            pl.BlockSpec((tm, d), lambda i: (i, 0)),
            pl.BlockSpec((tm, pd), lambda i: (i, 0)),
            _resident((d, d), lambda i: (0, 0)),
            _resident((pd, d), lambda i: (0, 0)),
            row, row,
        ],
        out_specs=pl.BlockSpec((tm, d), lambda i: (i, 0)),
        compiler_params=_params("parallel"),
        name="ple_ln",
    )(h, p, w_gate, w_proj, g, b)


def kernel(x, p, positions, ln_g, ln_b, w_ffn1_gu, w_ffn1_down, w_in, ret_gn_g, att_sinks,
           w_ret_out, w_att_out, w_mix_out, w_ffn2_gu, w_ffn2_down, w_ple_gate, w_ple_proj):
    batch, seq, d = x.shape
    depth = ln_g.shape[0]
    t = batch * seq
    alpha = (2.0 * depth) ** 0.25
    h = x.reshape(t, d)
    inner_decay, q_decay, k_decay, chunk_decay = _retention_decays()
    bounds = [0]
    for size in IN_SIZES:
        bounds.append(bounds[-1] + size)
    group_bounds = [bounds[0], bounds[2], bounds[4], bounds[7], bounds[9]]
    for i in range(depth):
        g = ln_g[i][:, None, :]
        b = ln_b[i][:, None, :]
        cr, sr, ca, s1, s2, w_qk, w_vg, w_att, w_gates, w1_gu, w1_down = _rope_tables(
            positions.reshape(t, 1), w_in[i], group_bounds, (w_ffn1_gu[i], w_ffn1_down[i]))

        h = _ffn_ln(h, w1_gu, w1_down, g[0], b[0], alpha=alpha)

        qkd = _proj(functools.partial(_proj_ret_qk_body, n_sub=2), h, w_qk, (cr, sr), (k_decay,),
                    n_out=3 * RET_QK, tm=1024, name="proj_ret_qk")
        aqkv = _proj(functools.partial(_proj_att_body, n_sub=2), h, w_att, (ca, s1, s2),
                     tm=1024, name="proj_att")
        att, vg, w2_gu, w2_down = _swa_and_proj(
            aqkv, att_sinks[i], h, w_vg, batch, seq, (w_ffn2_gu[i], w_ffn2_down[i]))
        ret, gates, w_ro, w_ao, w_mo, w_pg = _retention_and_proj(
            qkd, vg, inner_decay, q_decay, chunk_decay, ret_gn_g[i], h, w_gates, batch, seq,
            (w_ret_out[i], w_att_out[i], w_mix_out[i], w_ple_gate[i]))

        h = _mix_ln(h, ret, att, gates, w_ro, w_ao, w_mo, g[1], b[1], alpha=alpha)
        h = _ffn_ln(h, w2_gu, w2_down, g[2], b[2], alpha=alpha)
        h = _ple_ln(h, p[i].reshape(t, PLE_DIM), w_pg, w_ple_proj[i].astype(BF16), g[3], b[3],
                    alpha=alpha)
    return h.reshape(batch, seq, d)
```

```python
import functools

import jax
import jax.numpy as jnp
from jax import lax
from jax.experimental import pallas as pl
from jax.experimental.pallas import tpu as pltpu

F32 = jnp.float32
BF16 = jnp.bfloat16

D_MODEL = 2048
PLE_DIM = 256
D_FF = 5632
RET_HEADS = 8
RET_DK = 128
RET_DV = 256
RET_CHUNK = 256
RET_THETA = 10000.0
ATT_HEADS = 16
ATT_KV_HEADS = 4
ATT_DH = 64
ATT_GROUP = ATT_HEADS // ATT_KV_HEADS
WINDOW = 128
ATT_BLOCK = 128
ROPE_THETA = 500000.0
ROPE_DIMS = ATT_DH // 4
LN_EPS = 1e-5
GN_EPS = 1e-5

RET_QK = RET_HEADS * RET_DK
RET_V = RET_HEADS * RET_DV
ATT_Q = ATT_HEADS * ATT_DH
ATT_KV = ATT_KV_HEADS * ATT_DH
IN_SIZES = (RET_QK, RET_QK, RET_V, RET_V, ATT_Q, ATT_KV, ATT_KV, D_MODEL, D_MODEL)

LANES = 128
VMEM_LIMIT_BYTES = 60 * 1024 * 1024


def _params(*semantics):
    return pltpu.CompilerParams(dimension_semantics=semantics, vmem_limit_bytes=VMEM_LIMIT_BYTES)


def _resident(shape, index_map):
    return pl.BlockSpec(shape, index_map, pipeline_mode=pl.Buffered(1))


def _side_cast_specs(arrays, n_steps, step_of):
    specs = []
    for a in arrays:
        r, c = a.shape
        gc = next(k for k in (1, 2, 4, 8) if (r * k) % (16 * n_steps) == 0 and c % (k * LANES) == 0)
        gr = n_steps // gc
        specs.append(pl.BlockSpec(
            (r // gr, c // gc), lambda *idx, gc=gc: (step_of(*idx) // gc, step_of(*idx) % gc)))
    return specs


def _side_casts(refs, n_fixed_out):
    n = (len(refs) - n_fixed_out) // 2
    for src, dst in zip(refs[:n], refs[n + n_fixed_out:]):
        dst[...] = src[...].astype(BF16)
    return refs[n:n + n_fixed_out]


def _layer_norm(y, g, b):
    mu = jnp.mean(y, axis=-1, keepdims=True)
    d = y - mu
    var = jnp.mean(d * d, axis=-1, keepdims=True)
    return d * lax.rsqrt(var + LN_EPS) * g + b


def _dot(a, b):
    return jnp.dot(a, b, preferred_element_type=F32)


def _ffn_body(x_ref, wg_ref, wu_ref, wd_ref, g_ref, b_ref, o_ref, xb_ref, *, alpha, n_sub):
    j = pl.program_id(1)
    sub = x_ref.shape[0] // n_sub

    @pl.when(j == 0)
    def _():
        xb_ref[...] = x_ref[...].astype(BF16)
        o_ref[...] = jnp.zeros_like(o_ref)

    for r in range(n_sub):
        rows = pl.ds(r * sub, sub)
        xb = xb_ref[rows, :]
        gate = _dot(xb, wg_ref[...])
        up = _dot(xb, wu_ref[...])
        act = (gate * jax.nn.sigmoid(gate) * up).astype(BF16)
        o_ref[rows, :] += _dot(act, wd_ref[...])

    @pl.when(j == pl.num_programs(1) - 1)
    def _():
        for r in range(n_sub):
            rows = pl.ds(r * sub, sub)
            y = alpha * x_ref[rows, :] + 0.5 * o_ref[rows, :]
            o_ref[rows, :] = _layer_norm(y, g_ref[...], b_ref[...])


def _ffn_ln(x, w_gu, w_down, g, b, *, alpha, tm=1024, tf=512, n_sub=2):
    t, d = x.shape
    f = w_down.shape[0]
    nj = f // tf
    return pl.pallas_call(
        functools.partial(_ffn_body, alpha=alpha, n_sub=n_sub),
        out_shape=jax.ShapeDtypeStruct((t, d), F32),
        grid=(t // tm, nj),
        in_specs=[
            pl.BlockSpec((tm, d), lambda i, j: (i, 0)),
            pl.BlockSpec((d, tf), lambda i, j: (0, j)),
            pl.BlockSpec((d, tf), lambda i, j: (0, j + nj)),
            pl.BlockSpec((tf, d), lambda i, j: (j, 0)),
            pl.BlockSpec((1, d), lambda i, j: (0, 0)),
            pl.BlockSpec((1, d), lambda i, j: (0, 0)),
        ],
        out_specs=pl.BlockSpec((tm, d), lambda i, j: (i, 0)),
        scratch_shapes=[pltpu.VMEM((tm, d), BF16)],
        compiler_params=_params("parallel", "arbitrary"),
        name="ffn_ln",
    )(x, w_gu, w_gu, w_down, g, b)


def _rope_body(pos_ref, inv_ref, w_in_ref, *refs, in_bounds):
    n_seg = len(in_bounds) - 1
    cr_ref, sr_ref, ca_ref, s1_ref, s2_ref, *seg_refs = _side_casts(refs, 5 + n_seg)
    for seg_ref, lo, hi in zip(seg_refs, in_bounds[:-1], in_bounds[1:]):
        seg_ref[...] = w_in_ref[:, lo:hi].astype(BF16)
    pos = pos_ref[...].astype(F32)
    ang = pos * inv_ref[...]
    c = jnp.cos(ang)
    s = jnp.sin(ang)
    half_r = RET_DK // 2
    half_a = ROPE_DIMS // 2
    lane = lax.broadcasted_iota(jnp.int32, c.shape, 1)
    low = lane < half_r
    c_swap = pltpu.roll(c, half_r, axis=1)
    s_swap = pltpu.roll(s, half_r, axis=1)
    cr_ref[...] = jnp.where(low, c, c_swap)
    sr_ref[...] = jnp.where(low, -s, s_swap)
    c_att = jnp.where(low, c_swap, c)
    s_att = jnp.where(low, s_swap, s)
    in_head = lane & (ATT_DH - 1)
    first = in_head < half_a
    second = (in_head >= half_a) & (in_head < ROPE_DIMS)
    ca_ref[...] = jnp.where(first, c_att, jnp.where(second, pltpu.roll(c_att, half_a, axis=1), 1.0))
    s1_ref[...] = jnp.where(second, pltpu.roll(s_att, half_a, axis=1), 0.0)
    s2_ref[...] = jnp.where(first, -s_att, 0.0)


def _rope_tables(pos, w_in, in_bounds, casts=(), tm=1024):
    t = pos.shape[0]
    n_steps = t // tm
    d_in, n_in = w_in.shape
    w_rows = d_in // n_steps
    widths = [hi - lo for lo, hi in zip(in_bounds[:-1], in_bounds[1:])]
    cast_specs = _side_cast_specs(casts, n_steps, lambda i: i)
    half_r = RET_DK // 2
    half_a = ROPE_DIMS // 2
    inv_r = 1.0 / (RET_THETA ** (jnp.arange(half_r, dtype=F32) / half_r))
    inv_a = 1.0 / (ROPE_THETA ** (jnp.arange(half_a, dtype=F32) / half_a))
    inv = jnp.concatenate([inv_r, inv_a, jnp.zeros((LANES - half_r - half_a,), F32)])[None, :]
    tab = pl.BlockSpec((tm, LANES), lambda i: (i, 0))
    return pl.pallas_call(
        functools.partial(_rope_body, in_bounds=tuple(in_bounds)),
        out_shape=[jax.ShapeDtypeStruct((t, LANES), F32)] * 5
        + [jax.ShapeDtypeStruct((d_in, w), BF16) for w in widths]
        + [jax.ShapeDtypeStruct(a.shape, BF16) for a in casts],
        grid=(n_steps,),
        in_specs=[pl.BlockSpec((tm, 1), lambda i: (i, 0)),
                  pl.BlockSpec((1, LANES), lambda i: (0, 0)),
                  pl.BlockSpec((w_rows, n_in), lambda i: (i, 0))] + cast_specs,
        out_specs=[tab] * 5 + [pl.BlockSpec((w_rows, w), lambda i: (i, 0)) for w in widths]
        + cast_specs,
        compiler_params=_params("parallel"),
        name="rope_tables",
    )(pos, inv, w_in, *casts)


def _proj_ret_qk_body(x_ref, w_ref, cr_ref, sr_ref, kdec_ref, o_ref, *, n_sub):
    sub = x_ref.shape[0] // n_sub
    k_scale = RET_DK ** -0.5
    for r0 in range(0, x_ref.shape[0], sub):
        r = _dot(x_ref[r0:r0 + sub, :].astype(BF16), w_ref[...])
        cr = cr_ref[r0:r0 + sub, :]
        sr = sr_ref[r0:r0 + sub, :]
        for s in range(2 * RET_HEADS):
            x = r[:, s * LANES:(s + 1) * LANES]
            rot = x * cr + pltpu.roll(x, RET_DK // 2, axis=1) * sr
            if s < RET_HEADS:
                o_ref[r0:r0 + sub, s * LANES:(s + 1) * LANES] = rot.astype(BF16)
                continue
            rot = rot * k_scale
            o_ref[r0:r0 + sub, s * LANES:(s + 1) * LANES] = rot.astype(BF16)
            k_decay = kdec_ref[:, (s - RET_HEADS) * LANES:(s - RET_HEADS + 1) * LANES]
            d = s + RET_HEADS
            for c0 in range(0, sub, RET_CHUNK):
                o_ref[r0 + c0:r0 + c0 + RET_CHUNK, d * LANES:(d + 1) * LANES] = (
                    rot[c0:c0 + RET_CHUNK] * k_decay).astype(BF16)


def _proj_att_body(x_ref, w_ref, ca_ref, s1_ref, s2_ref, o_ref, *, n_sub):
    sub = x_ref.shape[0] // n_sub
    half = ROPE_DIMS // 2
    n_rot = (ATT_Q + ATT_KV) // LANES
    for r0 in range(0, x_ref.shape[0], sub):
        r = _dot(x_ref[r0:r0 + sub, :].astype(BF16), w_ref[...])
        ca = ca_ref[r0:r0 + sub, :]
        s1 = s1_ref[r0:r0 + sub, :]
        s2 = s2_ref[r0:r0 + sub, :]
        for s in range(n_rot):
            x = r[:, s * LANES:(s + 1) * LANES]
            rot = (x * ca + pltpu.roll(x, half, axis=1) * s1
                   + pltpu.roll(x, LANES - half, axis=1) * s2)
            o_ref[r0:r0 + sub, s * LANES:(s + 1) * LANES] = rot.astype(BF16)
        o_ref[r0:r0 + sub, n_rot * LANES:] = r[:, n_rot * LANES:].astype(BF16)


def _proj(body, x, w, tables=(), consts=(), *, n_out=None, tm=512, name):
    t, d = x.shape
    n = w.shape[1]
    n_out = n if n_out is None else n_out
    tab = pl.BlockSpec((tm, LANES), lambda i: (i, 0))
    return pl.pallas_call(
        body,
        out_shape=jax.ShapeDtypeStruct((t, n_out), BF16),
        grid=(t // tm,),
        in_specs=[pl.BlockSpec((tm, d), lambda i: (i, 0)), _resident((d, n), lambda i: (0, 0))]
        + [tab] * len(tables) + [_resident(c.shape, lambda i: (0, 0)) for c in consts],
        out_specs=pl.BlockSpec((tm, n_out), lambda i: (i, 0)),
        compiler_params=_params("parallel"),
        name=name,
    )(x, w, *tables, *consts)


def _retention_body(q_ref, k_ref, kd_ref, v_ref, g_ref, inner_ref, qd_ref, cd_ref, gn_ref,
                    h_ref, w_ref, *refs, n_chunks, proj_every):
    state_ref = refs[-1]
    o_ref, proj_ref = _side_casts(refs[:-1], 2)

    @pl.when(pl.program_id(2) == 0)
    def _():
        state_ref[...] = jnp.zeros_like(state_ref)

    inner_decay = inner_ref[...]
    q_decay = qd_ref[...]
    chunk_decay = cd_ref[...]
    gn = gn_ref[...]
    hb = h_ref[...].astype(BF16)
    proj_cols = proj_ref.shape[1] * proj_every // n_chunks
    for c in range(n_chunks):
        if c % proj_every == 0:
            cols = pl.ds((c // proj_every) * proj_cols, proj_cols)
            proj_ref[:, cols] = _dot(hb, w_ref[:, cols]).astype(BF16)
        rows = pl.ds(c * RET_CHUNK, RET_CHUNK)
        qn = q_ref[rows, :]
        vn = v_ref[rows, :]
        scores = lax.dot_general(qn, k_ref[rows, :], (((1,), (1,)), ((), ())),
                                 preferred_element_type=F32)
        inner = _dot((scores * inner_decay).astype(BF16), vn)
        state = state_ref[...]
        cross = _dot(qn, state.astype(BF16)) * q_decay
        state_ref[...] = state * chunk_decay + lax.dot_general(
            kd_ref[rows, :], vn, (((0,), (0,)), ((), ())), preferred_element_type=F32)
        o = inner + cross
        mu = jnp.mean(o, axis=-1, keepdims=True)
        d = o - mu
        var = jnp.mean(d * d, axis=-1, keepdims=True)
        y = d * lax.rsqrt(var + GN_EPS) * gn
        gate = g_ref[rows, :].astype(F32)
        o_ref[rows, :] = (gate * jax.nn.sigmoid(gate) * y).astype(BF16)


def _retention_decays():
    c = RET_CHUNK
    log_g = jnp.log1p(-jnp.exp2(-5.0 - jnp.arange(RET_HEADS, dtype=F32)))
    idx = jnp.arange(c, dtype=F32)
    q_decay = jnp.exp((idx[None, :] + 1.0) * log_g[:, None])
    k_decay = jnp.exp((c - 1.0 - idx)[None, :] * log_g[:, None])
    diff = idx[:, None] - idx[None, :]
    inner = jnp.where(diff[None] >= 0,
                      jnp.exp(jnp.maximum(diff, 0.0)[None] * log_g[:, None, None]), 0.0)
    chunk_decay = jnp.exp(c * log_g)
    q_decay = jnp.broadcast_to(q_decay[:, :, None], (RET_HEADS, c, RET_DV))
    k_decay = jnp.broadcast_to(k_decay.T[:, :, None], (c, RET_HEADS, RET_DK)).reshape(c, RET_QK)
    chunk_decay = jnp.broadcast_to(chunk_decay[:, None, None], (RET_HEADS, 1, RET_DV))
    return inner, q_decay, k_decay, chunk_decay


def _retention_and_proj(qkd, vg, inner, q_decay, chunk_decay, gn_g, h, w, batch, seq, casts=(), *,
                        ts=4096, proj_every=1):
    t, d = h.shape
    n_out = w.shape[1]
    ns = seq // ts
    tp = t // (batch * RET_HEADS * ns)
    gn = gn_g.reshape(RET_HEADS, 1, RET_DV)

    def rows(b, hd, s):
        return b * ns + s

    def tile(b, hd, s):
        return (b * RET_HEADS + hd) * ns + s

    head_tab = lambda shape: pl.BlockSpec((None,) + shape, lambda b, hd, s: (hd, 0, 0))
    cast_specs = _side_cast_specs(casts, batch * RET_HEADS * ns, tile)
    return pl.pallas_call(
        functools.partial(_retention_body, n_chunks=ts // RET_CHUNK, proj_every=proj_every),
        out_shape=[jax.ShapeDtypeStruct((t, RET_V), BF16), jax.ShapeDtypeStruct((t, n_out), BF16)]
        + [jax.ShapeDtypeStruct(a.shape, BF16) for a in casts],
        grid=(batch, RET_HEADS, ns),
        in_specs=[
            pl.BlockSpec((ts, RET_DK), lambda b, hd, s: (rows(b, hd, s), hd)),
            pl.BlockSpec((ts, RET_DK), lambda b, hd, s: (rows(b, hd, s), RET_HEADS + hd)),
            pl.BlockSpec((ts, RET_DK), lambda b, hd, s: (rows(b, hd, s), 2 * RET_HEADS + hd)),
            pl.BlockSpec((ts, RET_DV), lambda b, hd, s: (rows(b, hd, s), hd)),
            pl.BlockSpec((ts, RET_DV), lambda b, hd, s: (rows(b, hd, s), RET_HEADS + hd)),
            head_tab((RET_CHUNK, RET_CHUNK)),
            head_tab((RET_CHUNK, RET_DV)),
            head_tab((1, RET_DV)),
            head_tab((1, RET_DV)),
            pl.BlockSpec((tp, d), lambda b, hd, s: (tile(b, hd, s), 0)),
            _resident((d, n_out), lambda b, hd, s: (0, 0)),
        ] + cast_specs,
        out_specs=[pl.BlockSpec((ts, RET_DV), lambda b, hd, s: (rows(b, hd, s), hd)),
                   pl.BlockSpec((tp, n_out), lambda b, hd, s: (tile(b, hd, s), 0))] + cast_specs,
        scratch_shapes=[pltpu.VMEM((RET_DK, RET_DV), F32)],
        compiler_params=_params("parallel", "parallel", "arbitrary"),
        name="retention_proj",
    )(qkd, qkd, qkd, vg, vg, inner, q_decay, chunk_decay, gn, h, w, *casts)


def _swa_body(sink_ref, q_ref, kvc_ref, kvp_ref, h_ref, w_ref, *refs, n_blocks):
    o_ref, proj_ref = _side_casts(refs, 2)
    n = pl.program_id(1)
    c = ATT_BLOCK
    qi = lax.broadcasted_iota(jnp.int32, (c, 2 * c), 0) + c
    kj = lax.broadcasted_iota(jnp.int32, (c, 2 * c), 1)
    rel = qi - kj
    in_window = (rel >= 0) & (rel < WINDOW)
    valid_first = in_window & (kj >= jnp.where(n > 0, 0, c))
    scale = ATT_DH ** -0.5

    def heads_of(hk):
        return [hk * ATT_GROUP + g for g in range(ATT_GROUP)]

    def band(qb, hk, offset):
        cols = slice(offset + hk * ATT_DH, offset + (hk + 1) * ATT_DH)
        prev = kvp_ref[:, cols] if qb == 0 else kvc_ref[(qb - 1) * c:qb * c, cols]
        return jnp.concatenate([prev, kvc_ref[qb * c:(qb + 1) * c, cols]], axis=0)

    def scores(qb, hk):
        q4 = jnp.concatenate([q_ref[qb * c:(qb + 1) * c, hd * ATT_DH:(hd + 1) * ATT_DH]
                              for hd in heads_of(hk)], axis=0)
        return lax.dot_general(q4 * scale, band(qb, hk, 0), (((1,), (1,)), ((), ())),
                               preferred_element_type=F32)

    def softmax(qb, hk, s4):
        valid = valid_first if qb == 0 else in_window
        masked = [jnp.where(valid, s4[g * c:(g + 1) * c], -jnp.inf) for g in range(ATT_GROUP)]
        sinks = [sink_ref[hd] for hd in heads_of(hk)]
        maxes = [jnp.maximum(jnp.max(s, axis=-1, keepdims=True), sink)
                 for s, sink in zip(masked, sinks)]
        probs = [jnp.exp(s - m).astype(BF16) for s, m in zip(masked, maxes)]
        sink_terms = [jnp.exp(sink - m) for sink, m in zip(sinks, maxes)]
        return jnp.concatenate(probs, axis=0), sink_terms

    ones = jnp.ones((2 * c, 2 * c - ATT_DH), BF16)

    def weighted_values(qb, hk, e4):
        return _dot(e4, jnp.concatenate([band(qb, hk, ATT_KV), ones], axis=1))

    def store(qb, hk, ov, sink_terms):
        groups = [ov[g * c:(g + 1) * c] for g in range(ATT_GROUP)]
        inv = [1.0 / (og[:, c:c + ATT_DH] + st) for og, st in zip(groups, sink_terms)]
        outs = [og[:, :ATT_DH] * r for og, r in zip(groups, inv)]
        for pair in range(ATT_GROUP // 2):
            col = heads_of(hk)[2 * pair] * ATT_DH
            o_ref[qb * c:(qb + 1) * c, col:col + 2 * ATT_DH] = jnp.concatenate(
                [outs[2 * pair], outs[2 * pair + 1]], axis=1).astype(BF16)

    units = [(qb, hk) for hk in range(ATT_KV_HEADS) for qb in range(n_blocks)]
    hb = h_ref[...].astype(BF16)
    chunk = proj_ref.shape[1] // len(units)

    def project(u):
        cols = slice(u * chunk, (u + 1) * chunk)
        proj_ref[:, cols] = _dot(hb, w_ref[:, cols]).astype(BF16)

    s_next = scores(*units[0])
    pending = None
    for u, unit in enumerate(units):
        s_cur = s_next
        if u + 1 < len(units):
            s_next = scores(*units[u + 1])
        project(u)
        e4, sink_terms = softmax(*unit, s_cur)
        if pending is not None:
            store(*pending)
        pending = (*unit, weighted_values(*unit, e4), sink_terms)
    store(*pending)


def _swa_and_proj(aqkv, sinks, h, w, batch, seq, casts=(), *, n_blocks=4):
    t, d = h.shape
    n_out = w.shape[1]
    tq = n_blocks * ATT_BLOCK
    nq = seq // tq
    kv_col = ATT_Q // (2 * ATT_KV)
    cast_specs = _side_cast_specs(casts, batch * nq, lambda b, n: b * nq + n)
    return pl.pallas_call(
        functools.partial(_swa_body, n_blocks=n_blocks),
        out_shape=[jax.ShapeDtypeStruct((t, ATT_Q), BF16), jax.ShapeDtypeStruct((t, n_out), BF16)]
        + [jax.ShapeDtypeStruct(a.shape, BF16) for a in casts],
        grid=(batch, nq),
        in_specs=[
            pl.BlockSpec(memory_space=pltpu.SMEM),
            pl.BlockSpec((tq, ATT_Q), lambda b, n: (b * nq + n, 0)),
            pl.BlockSpec((tq, 2 * ATT_KV), lambda b, n: (b * nq + n, kv_col)),
            pl.BlockSpec((ATT_BLOCK, 2 * ATT_KV),
                         lambda b, n: ((b * nq + n) * n_blocks - jnp.minimum(n, 1), kv_col)),
            pl.BlockSpec((tq, d), lambda b, n: (b * nq + n, 0)),
            _resident((d, n_out), lambda b, n: (0, 0)),
        ] + cast_specs,
        out_specs=[pl.BlockSpec((tq, ATT_Q), lambda b, n: (b * nq + n, 0)),
                   pl.BlockSpec((tq, n_out), lambda b, n: (b * nq + n, 0))] + cast_specs,
        compiler_params=_params("parallel", "arbitrary"),
        name="swa_proj",
    )(sinks, aqkv, aqkv, aqkv, h, w, *casts)


def _mix_body(h_ref, ret_ref, att_ref, gr_ref, ga_ref, wr_ref, wa_ref, wm_ref, g_ref, b_ref,
              o_ref, *, alpha):
    ret_branch = _dot(ret_ref[...], wr_ref[...])
    att_branch = _dot(att_ref[...], wa_ref[...])
    merged = (jax.nn.sigmoid(gr_ref[...].astype(F32)) * ret_branch
              + jax.nn.sigmoid(ga_ref[...].astype(F32)) * att_branch)
    mixed = _dot(merged.astype(BF16), wm_ref[...])
    o_ref[...] = _layer_norm(alpha * h_ref[...] + mixed, g_ref[...], b_ref[...])


def _mix_ln(h, ret, att, gates, w_ret, w_att, w_mix, g, b, *, alpha, tm=512):
    t, d = h.shape
    row = pl.BlockSpec((1, d), lambda i: (0, 0))
    return pl.pallas_call(
        functools.partial(_mix_body, alpha=alpha),
        out_shape=jax.ShapeDtypeStruct((t, d), F32),
        grid=(t // tm,),
        in_specs=[
            pl.BlockSpec((tm, d), lambda i: (i, 0)),
            pl.BlockSpec((tm, RET_V), lambda i: (i, 0)),
            pl.BlockSpec((tm, ATT_Q), lambda i: (i, 0)),
            pl.BlockSpec((tm, d), lambda i: (i, 0)),
            pl.BlockSpec((tm, d), lambda i: (i, 1)),
            _resident((RET_V, d), lambda i: (0, 0)),
            _resident((ATT_Q, d), lambda i: (0, 0)),
            _resident((d, d), lambda i: (0, 0)),
            row, row,
        ],
        out_specs=pl.BlockSpec((tm, d), lambda i: (i, 0)),
        compiler_params=_params("parallel"),
        name="mix_ln",
    )(h, ret, att, gates, gates, w_ret, w_att, w_mix, g, b)


def _ple_body(h_ref, p_ref, wg_ref, wp_ref, g_ref, b_ref, o_ref, *, alpha, n_sub):
    sub = h_ref.shape[0] // n_sub
    for r0 in range(0, h_ref.shape[0], sub):
        h = h_ref[r0:r0 + sub, :]
        gate = jax.nn.sigmoid(_dot(h.astype(BF16), wg_ref[...]))
        ple = gate * _dot(p_ref[r0:r0 + sub, :].astype(BF16), wp_ref[...])
        o_ref[r0:r0 + sub, :] = _layer_norm(alpha * h + ple, g_ref[...], b_ref[...])


def _ple_ln(h, p, w_gate, w_proj, g, b, *, alpha, tm=1024, n_sub=2):
    t, d = h.shape
    pd = p.shape[1]
    row = pl.BlockSpec((1, d), lambda i: (0, 0))
    return pl.pallas_call(
        functools.partial(_ple_body, alpha=alpha, n_sub=n_sub),
        out_shape=jax.ShapeDtypeStruct((t, d), F32),
        grid=(t // tm,),
        in_specs=[
            pl.BlockSpec((tm, d), lambda i: (i, 0)),
            pl.BlockSpec((tm, pd), lambda i: (i, 0)),
            _resident((d, d), lambda i: (0, 0)),
            _resident((pd, d), lambda i: (0, 0)),
            row, row,
        ],
        out_specs=pl.BlockSpec((tm, d), lambda i: (i, 0)),
        compiler_params=_params("parallel"),
        name="ple_ln",
    )(h, p, w_gate, w_proj, g, b)


def kernel(x, p, positions, ln_g, ln_b, w_ffn1_gu, w_ffn1_down, w_in, ret_gn_g, att_sinks,
           w_ret_out, w_att_out, w_mix_out, w_ffn2_gu, w_ffn2_down, w_ple_gate, w_ple_proj):
    batch, seq, d = x.shape
    depth = ln_g.shape[0]
    t = batch * seq
    alpha = (2.0 * depth) ** 0.25
    h = x.reshape(t, d)
    inner_decay, q_decay, k_decay, chunk_decay = _retention_decays()
    bounds = [0]
    for size in IN_SIZES:
        bounds.append(bounds[-1] + size)
    group_bounds = [bounds[0], bounds[2], bounds[4], bounds[7], bounds[9]]
    for i in range(depth):
        g = ln_g[i][:, None, :]
        b = ln_b[i][:, None, :]
        cr, sr, ca, s1, s2, w_qk, w_vg, w_att, w_gates, w1_gu, w1_down = _rope_tables(
            positions.reshape(t, 1), w_in[i], group_bounds, (w_ffn1_gu[i], w_ffn1_down[i]))

        h = _ffn_ln(h, w1_gu, w1_down, g[0], b[0], alpha=alpha)

        qkd = _proj(functools.partial(_proj_ret_qk_body, n_sub=2), h, w_qk, (cr, sr), (k_decay,),
                    n_out=3 * RET_QK, tm=1024, name="proj_ret_qk")
        aqkv = _proj(functools.partial(_proj_att_body, n_sub=2), h, w_att, (ca, s1, s2),
                     tm=1024, name="proj_att")
        att, vg, w2_gu, w2_down = _swa_and_proj(
            aqkv, att_sinks[i], h, w_vg, batch, seq, (w_ffn2_gu[i], w_ffn2_down[i]))
        ret, gates, w_ro, w_ao, w_mo, w_pg = _retention_and_proj(
            qkd, vg, inner_decay, q_decay, chunk_decay, ret_gn_g[i], h, w_gates, batch, seq,
            (w_ret_out[i], w_att_out[i], w_mix_out[i], w_ple_gate[i]))

        h = _mix_ln(h, ret, att, gates, w_ro, w_ao, w_mo, g[1], b[1], alpha=alpha)
        h = _ffn_ln(h, w2_gu, w2_down, g[2], b[2], alpha=alpha)
        h = _ple_ln(h, p[i].reshape(t, PLE_DIM), w_pg, w_ple_proj[i].astype(BF16), g[3], b[3],
                    alpha=alpha)
    return h.reshape(batch, seq, d)
```

```python
import functools

import jax
import jax.numpy as jnp
from jax import lax
from jax.experimental import pallas as pl
from jax.experimental.pallas import tpu as pltpu

F32 = jnp.float32
BF16 = jnp.bfloat16

D_MODEL = 2048
PLE_DIM = 256
D_FF = 5632
RET_HEADS = 8
RET_DK = 128
RET_DV = 256
RET_CHUNK = 256
RET_THETA = 10000.0
ATT_HEADS = 16
ATT_KV_HEADS = 4
ATT_DH = 64
ATT_GROUP = ATT_HEADS // ATT_KV_HEADS
WINDOW = 128
ATT_BLOCK = 128
ROPE_THETA = 500000.0
ROPE_DIMS = ATT_DH // 4
LN_EPS = 1e-5
GN_EPS = 1e-5

RET_QK = RET_HEADS * RET_DK
RET_V = RET_HEADS * RET_DV
ATT_Q = ATT_HEADS * ATT_DH
ATT_KV = ATT_KV_HEADS * ATT_DH
IN_SIZES = (RET_QK, RET_QK, RET_V, RET_V, ATT_Q, ATT_KV, ATT_KV, D_MODEL, D_MODEL)

LANES = 128
VMEM_LIMIT_BYTES = 60 * 1024 * 1024


def _params(*semantics):
    return pltpu.CompilerParams(dimension_semantics=semantics, vmem_limit_bytes=VMEM_LIMIT_BYTES)


def _resident(shape, index_map):
    return pl.BlockSpec(shape, index_map, pipeline_mode=pl.Buffered(1))


def _side_cast_specs(arrays, n_steps, step_of):
    specs = []
    for a in arrays:
        r, c = a.shape
        gc = next(k for k in (1, 2, 4, 8) if (r * k) % (16 * n_steps) == 0 and c % (k * LANES) == 0)
        gr = n_steps // gc
        specs.append(pl.BlockSpec(
            (r // gr, c // gc), lambda *idx, gc=gc: (step_of(*idx) // gc, step_of(*idx) % gc)))
    return specs


def _side_casts(refs, n_fixed_out):
    n = (len(refs) - n_fixed_out) // 2
    for src, dst in zip(refs[:n], refs[n + n_fixed_out:]):
        dst[...] = src[...].astype(BF16)
    return refs[n:n + n_fixed_out]


def _layer_norm(y, g, b, eps=LN_EPS):
    mu = jnp.mean(y, axis=-1, keepdims=True)
    d = y - mu
    var = jnp.mean(d * d, axis=-1, keepdims=True)
    return d * lax.rsqrt(var + eps) * g + b


def _dot(a, b):
    return jnp.dot(a, b, preferred_element_type=F32)


def _ffn_body(x_ref, wg_ref, wu_ref, wd_ref, g_ref, b_ref, o_ref, xb_ref, *, alpha, n_sub):
    j = pl.program_id(1)
    sub = x_ref.shape[0] // n_sub

    @pl.when(j == 0)
    def _():
        x = x_ref[...]
        xb_ref[...] = x.astype(BF16)
        o_ref[...] = x

    for r in range(n_sub):
        rows = pl.ds(r * sub, sub)
        xb = xb_ref[rows, :]
        gate = _dot(xb, wg_ref[...])
        up = _dot(xb, wu_ref[...])
        act = (gate * jax.nn.sigmoid(gate) * up).astype(BF16)
        o_ref[rows, :] += (0.5 / alpha) * _dot(act, wd_ref[...])

    @pl.when(j == pl.num_programs(1) - 1)
    def _():
        for r in range(n_sub):
            rows = pl.ds(r * sub, sub)
            o_ref[rows, :] = _layer_norm(o_ref[rows, :], g_ref[...], b_ref[...],
                                         eps=LN_EPS / alpha ** 2)


def _ffn_ln(x, w_gu, w_down, g, b, *, alpha, tm=1024, tf=512, n_sub=2):
    t, d = x.shape
    f = w_down.shape[0]
    nj = f // tf
    return pl.pallas_call(
        functools.partial(_ffn_body, alpha=alpha, n_sub=n_sub),
        out_shape=jax.ShapeDtypeStruct((t, d), F32),
        grid=(t // tm, nj),
        in_specs=[
            pl.BlockSpec((tm, d), lambda i, j: (i, 0)),
            pl.BlockSpec((d, tf), lambda i, j: (0, j)),
            pl.BlockSpec((d, tf), lambda i, j: (0, j + nj)),
            pl.BlockSpec((tf, d), lambda i, j: (j, 0)),
            pl.BlockSpec((1, d), lambda i, j: (0, 0)),
            pl.BlockSpec((1, d), lambda i, j: (0, 0)),
        ],
        out_specs=pl.BlockSpec((tm, d), lambda i, j: (i, 0)),
        scratch_shapes=[pltpu.VMEM((tm, d), BF16)],
        compiler_params=_params("parallel", "arbitrary"),
        name="ffn_ln",
    )(x, w_gu, w_gu, w_down, g, b)


def _rope_body(pos_ref, inv_ref, w_in_ref, *refs, in_bounds):
    n_seg = len(in_bounds) - 1
    cr_ref, sr_ref, ca_ref, s1_ref, s2_ref, *seg_refs = _side_casts(refs, 5 + n_seg)
    for seg_ref, lo, hi in zip(seg_refs, in_bounds[:-1], in_bounds[1:]):
        seg_ref[...] = w_in_ref[:, lo:hi].astype(BF16)
    pos = pos_ref[...].astype(F32)
    ang = pos * inv_ref[...]
    c = jnp.cos(ang)
    s = jnp.sin(ang)
    half_r = RET_DK // 2
    half_a = ROPE_DIMS // 2
    lane = lax.broadcasted_iota(jnp.int32, c.shape, 1)
    low = lane < half_r
    c_swap = pltpu.roll(c, half_r, axis=1)
    s_swap = pltpu.roll(s, half_r, axis=1)
    cr_ref[...] = jnp.where(low, c, c_swap)
    sr_ref[...] = jnp.where(low, -s, s_swap)
    c_att = jnp.where(low, c_swap, c)
    s_att = jnp.where(low, s_swap, s)
    in_head = lane & (ATT_DH - 1)
    first = in_head < half_a
    second = (in_head >= half_a) & (in_head < ROPE_DIMS)
    ca_ref[...] = jnp.where(first, c_att, jnp.where(second, pltpu.roll(c_att, half_a, axis=1), 1.0))
    s1_ref[...] = jnp.where(second, pltpu.roll(s_att, half_a, axis=1), 0.0)
    s2_ref[...] = jnp.where(first, -s_att, 0.0)


def _rope_tables(pos, w_in, in_bounds, casts=(), tm=1024):
    t = pos.shape[0]
    n_steps = t // tm
    d_in, n_in = w_in.shape
    w_rows = d_in // n_steps
    widths = [hi - lo for lo, hi in zip(in_bounds[:-1], in_bounds[1:])]
    cast_specs = _side_cast_specs(casts, n_steps, lambda i: i)
    half_r = RET_DK // 2
    half_a = ROPE_DIMS // 2
    inv_r = 1.0 / (RET_THETA ** (jnp.arange(half_r, dtype=F32) / half_r))
    inv_a = 1.0 / (ROPE_THETA ** (jnp.arange(half_a, dtype=F32) / half_a))
    inv = jnp.concatenate([inv_r, inv_a, jnp.zeros((LANES - half_r - half_a,), F32)])[None, :]
    tab = pl.BlockSpec((tm, LANES), lambda i: (i, 0))
    return pl.pallas_call(
        functools.partial(_rope_body, in_bounds=tuple(in_bounds)),
        out_shape=[jax.ShapeDtypeStruct((t, LANES), F32)] * 5
        + [jax.ShapeDtypeStruct((d_in, w), BF16) for w in widths]
        + [jax.ShapeDtypeStruct(a.shape, BF16) for a in casts],
        grid=(n_steps,),
        in_specs=[pl.BlockSpec((tm, 1), lambda i: (i, 0)),
                  pl.BlockSpec((1, LANES), lambda i: (0, 0)),
                  pl.BlockSpec((w_rows, n_in), lambda i: (i, 0))] + cast_specs,
        out_specs=[tab] * 5 + [pl.BlockSpec((w_rows, w), lambda i: (i, 0)) for w in widths]
        + cast_specs,
        compiler_params=_params("parallel"),
        name="rope_tables",
    )(pos, inv, w_in, *casts)


def _proj_ret_qk_body(x_ref, w_ref, cr_ref, sr_ref, kdec_ref, o_ref, *, n_sub):
    sub = x_ref.shape[0] // n_sub
    k_scale = RET_DK ** -0.5
    for r0 in range(0, x_ref.shape[0], sub):
        r = _dot(x_ref[r0:r0 + sub, :].astype(BF16), w_ref[...])
        cr = cr_ref[r0:r0 + sub, :]
        sr = sr_ref[r0:r0 + sub, :]
        for s in range(2 * RET_HEADS):
            x = r[:, s * LANES:(s + 1) * LANES]
            rot = x * cr + pltpu.roll(x, RET_DK // 2, axis=1) * sr
            if s < RET_HEADS:
                o_ref[r0:r0 + sub, s * LANES:(s + 1) * LANES] = rot.astype(BF16)
                continue
            rot = rot * k_scale
            o_ref[r0:r0 + sub, s * LANES:(s + 1) * LANES] = rot.astype(BF16)
            k_decay = kdec_ref[:, (s - RET_HEADS) * LANES:(s - RET_HEADS + 1) * LANES]
            d = s + RET_HEADS
            for c0 in range(0, sub, RET_CHUNK):
                o_ref[r0 + c0:r0 + c0 + RET_CHUNK, d * LANES:(d + 1) * LANES] = (
                    rot[c0:c0 + RET_CHUNK] * k_decay).astype(BF16)


def _proj_att_body(x_ref, w_ref, ca_ref, s1_ref, s2_ref, o_ref, *, n_sub):
    sub = x_ref.shape[0] // n_sub
    half = ROPE_DIMS // 2
    n_rot = (ATT_Q + ATT_KV) // LANES
    for r0 in range(0, x_ref.shape[0], sub):
        r = _dot(x_ref[r0:r0 + sub, :].astype(BF16), w_ref[...])
        ca = ca_ref[r0:r0 + sub, :]
        s1 = s1_ref[r0:r0 + sub, :]
        s2 = s2_ref[r0:r0 + sub, :]
        for s in range(n_rot):
            x = r[:, s * LANES:(s + 1) * LANES]
            rot = (x * ca + pltpu.roll(x, half, axis=1) * s1
                   + pltpu.roll(x, LANES - half, axis=1) * s2)
            o_ref[r0:r0 + sub, s * LANES:(s + 1) * LANES] = rot.astype(BF16)
        o_ref[r0:r0 + sub, n_rot * LANES:] = r[:, n_rot * LANES:].astype(BF16)


def _proj(body, x, w, tables=(), consts=(), *, n_out=None, tm=512, name):
    t, d = x.shape
    n = w.shape[1]
    n_out = n if n_out is None else n_out
    tab = pl.BlockSpec((tm, LANES), lambda i: (i, 0))
    return pl.pallas_call(
        body,
        out_shape=jax.ShapeDtypeStruct((t, n_out), BF16),
        grid=(t // tm,),
        in_specs=[pl.BlockSpec((tm, d), lambda i: (i, 0)), _resident((d, n), lambda i: (0, 0))]
        + [tab] * len(tables) + [_resident(c.shape, lambda i: (0, 0)) for c in consts],
        out_specs=pl.BlockSpec((tm, n_out), lambda i: (i, 0)),
        compiler_params=_params("parallel"),
        name=name,
    )(x, w, *tables, *consts)


def _retention_body(q_ref, k_ref, kd_ref, v_ref, g_ref, inner_ref, qd_ref, cd_ref, gn_ref,
                    h_ref, w_ref, *refs, n_chunks, proj_every):
    state_ref = refs[-1]
    o_ref, proj_ref = _side_casts(refs[:-1], 2)

    @pl.when(pl.program_id(2) == 0)
    def _():
        state_ref[...] = jnp.zeros_like(state_ref)

    inner_decay = inner_ref[...]
    q_decay = qd_ref[...]
    chunk_decay = cd_ref[...]
    gn = gn_ref[...]
    hb = h_ref[...].astype(BF16)
    proj_cols = proj_ref.shape[1] * proj_every // n_chunks
    for c in range(n_chunks):
        if c % proj_every == 0:
            cols = pl.ds((c // proj_every) * proj_cols, proj_cols)
            proj_ref[:, cols] = _dot(hb, w_ref[:, cols]).astype(BF16)
        rows = pl.ds(c * RET_CHUNK, RET_CHUNK)
        qn = q_ref[rows, :]
        vn = v_ref[rows, :]
        scores = lax.dot_general(qn, k_ref[rows, :], (((1,), (1,)), ((), ())),
                                 preferred_element_type=F32)
        inner = _dot((scores * inner_decay).astype(BF16), vn)
        state = state_ref[...]
        cross = _dot(qn, state.astype(BF16)) * q_decay
        state_ref[...] = state * chunk_decay + lax.dot_general(
            kd_ref[rows, :], vn, (((0,), (0,)), ((), ())), preferred_element_type=F32)
        o = inner + cross
        mu = jnp.mean(o, axis=-1, keepdims=True)
        d = o - mu
        var = jnp.mean(d * d, axis=-1, keepdims=True)
        y = d * lax.rsqrt(var + GN_EPS) * gn
        gate = g_ref[rows, :].astype(F32)
        o_ref[rows, :] = (gate * jax.nn.sigmoid(gate) * y).astype(BF16)


def _retention_decays():
    c = RET_CHUNK
    log_g = jnp.log1p(-jnp.exp2(-5.0 - jnp.arange(RET_HEADS, dtype=F32)))
    idx = jnp.arange(c, dtype=F32)
    q_decay = jnp.exp((idx[None, :] + 1.0) * log_g[:, None])
    k_decay = jnp.exp((c - 1.0 - idx)[None, :] * log_g[:, None])
    diff = idx[:, None] - idx[None, :]
    inner = jnp.where(diff[None] >= 0,
                      jnp.exp(jnp.maximum(diff, 0.0)[None] * log_g[:, None, None]), 0.0)
    chunk_decay = jnp.exp(c * log_g)
    q_decay = jnp.broadcast_to(q_decay[:, :, None], (RET_HEADS, c, RET_DV))
    k_decay = jnp.broadcast_to(k_decay.T[:, :, None], (c, RET_HEADS, RET_DK)).reshape(c, RET_QK)
    chunk_decay = jnp.broadcast_to(chunk_decay[:, None, None], (RET_HEADS, 1, RET_DV))
    return inner, q_decay, k_decay, chunk_decay


def _retention_and_proj(qkd, vg, inner, q_decay, chunk_decay, gn_g, h, w, batch, seq, casts=(), *,
                        ts=4096, proj_every=1):
    t, d = h.shape
    n_out = w.shape[1]
    ns = seq // ts
    tp = t // (batch * RET_HEADS * ns)
    gn = gn_g.reshape(RET_HEADS, 1, RET_DV)

    def rows(b, hd, s):
        return b * ns + s

    def tile(b, hd, s):
        return (b * RET_HEADS + hd) * ns + s

    head_tab = lambda shape: pl.BlockSpec((None,) + shape, lambda b, hd, s: (hd, 0, 0))
    cast_specs = _side_cast_specs(casts, batch * RET_HEADS * ns, tile)
    return pl.pallas_call(
        functools.partial(_retention_body, n_chunks=ts // RET_CHUNK, proj_every=proj_every),
        out_shape=[jax.ShapeDtypeStruct((t, RET_V), BF16), jax.ShapeDtypeStruct((t, n_out), BF16)]
        + [jax.ShapeDtypeStruct(a.shape, BF16) for a in casts],
        grid=(batch, RET_HEADS, ns),
        in_specs=[
            pl.BlockSpec((ts, RET_DK), lambda b, hd, s: (rows(b, hd, s), hd)),
            pl.BlockSpec((ts, RET_DK), lambda b, hd, s: (rows(b, hd, s), RET_HEADS + hd)),
            pl.BlockSpec((ts, RET_DK), lambda b, hd, s: (rows(b, hd, s), 2 * RET_HEADS + hd)),
            pl.BlockSpec((ts, RET_DV), lambda b, hd, s: (rows(b, hd, s), hd)),
            pl.BlockSpec((ts, RET_DV), lambda b, hd, s: (rows(b, hd, s), RET_HEADS + hd)),
            head_tab((RET_CHUNK, RET_CHUNK)),
            head_tab((RET_CHUNK, RET_DV)),
            head_tab((1, RET_DV)),
            head_tab((1, RET_DV)),
            pl.BlockSpec((tp, d), lambda b, hd, s: (tile(b, hd, s), 0)),
            _resident((d, n_out), lambda b, hd, s: (0, 0)),
        ] + cast_specs,
        out_specs=[pl.BlockSpec((ts, RET_DV), lambda b, hd, s: (rows(b, hd, s), hd)),
                   pl.BlockSpec((tp, n_out), lambda b, hd, s: (tile(b, hd, s), 0))] + cast_specs,
        scratch_shapes=[pltpu.VMEM((RET_DK, RET_DV), F32)],
        compiler_params=_params("parallel", "parallel", "arbitrary"),
        name="retention_proj",
    )(qkd, qkd, qkd, vg, vg, inner, q_decay, chunk_decay, gn, h, w, *casts)


def _swa_body(sink_ref, q_ref, kvc_ref, kvp_ref, h_ref, w_ref, *refs, n_blocks):
    o_ref, proj_ref = _side_casts(refs, 2)
    n = pl.program_id(1)
    c = ATT_BLOCK
    qi = lax.broadcasted_iota(jnp.int32, (c, 2 * c), 0) + c
    kj = lax.broadcasted_iota(jnp.int32, (c, 2 * c), 1)
    rel = qi - kj
    in_window = (rel >= 0) & (rel < WINDOW)
    valid_first = in_window & (kj >= jnp.where(n > 0, 0, c))
    scale = ATT_DH ** -0.5

    def heads_of(hk):
        return [hk * ATT_GROUP + g for g in range(ATT_GROUP)]

    def band(qb, hk, offset):
        cols = slice(offset + hk * ATT_DH, offset + (hk + 1) * ATT_DH)
        prev = kvp_ref[:, cols] if qb == 0 else kvc_ref[(qb - 1) * c:qb * c, cols]
        return jnp.concatenate([prev, kvc_ref[qb * c:(qb + 1) * c, cols]], axis=0)

    def scores(qb, hk):
        q4 = jnp.concatenate([q_ref[qb * c:(qb + 1) * c, hd * ATT_DH:(hd + 1) * ATT_DH]
                              for hd in heads_of(hk)], axis=0)
        return lax.dot_general(q4 * scale, band(qb, hk, 0), (((1,), (1,)), ((), ())),
                               preferred_element_type=F32)

    def softmax(qb, hk, s4):
        valid = valid_first if qb == 0 else in_window
        masked = [jnp.where(valid, s4[g * c:(g + 1) * c], -jnp.inf) for g in range(ATT_GROUP)]
        sinks = [sink_ref[hd] for hd in heads_of(hk)]
        maxes = [jnp.maximum(jnp.max(s, axis=-1, keepdims=True), sink)
                 for s, sink in zip(masked, sinks)]
        probs = [jnp.exp(s - m).astype(BF16) for s, m in zip(masked, maxes)]
        sink_terms = [jnp.exp(sink - m) for sink, m in zip(sinks, maxes)]
        return jnp.concatenate(probs, axis=0), sink_terms

    ones = jnp.ones((2 * c, 2 * c - ATT_DH), BF16)

    def weighted_values(qb, hk, e4):
        return _dot(e4, jnp.concatenate([band(qb, hk, ATT_KV), ones], axis=1))

    def store(qb, hk, ov, sink_terms):
        groups = [ov[g * c:(g + 1) * c] for g in range(ATT_GROUP)]
        inv = [1.0 / (og[:, c:c + ATT_DH] + st) for og, st in zip(groups, sink_terms)]
        outs = [og[:, :ATT_DH] * r for og, r in zip(groups, inv)]
        for pair in range(ATT_GROUP // 2):
            col = heads_of(hk)[2 * pair] * ATT_DH
            o_ref[qb * c:(qb + 1) * c, col:col + 2 * ATT_DH] = jnp.concatenate(
                [outs[2 * pair], outs[2 * pair + 1]], axis=1).astype(BF16)

    units = [(qb, hk) for hk in range(ATT_KV_HEADS) for qb in range(n_blocks)]
    hb = h_ref[...].astype(BF16)
    chunk = proj_ref.shape[1] // len(units)

    def project(u):
        cols = slice(u * chunk, (u + 1) * chunk)
        proj_ref[:, cols] = _dot(hb, w_ref[:, cols]).astype(BF16)

    s_next = scores(*units[0])
    pending = None
    for u, unit in enumerate(units):
        s_cur = s_next
        if u + 1 < len(units):
            s_next = scores(*units[u + 1])
        project(u)
        e4, sink_terms = softmax(*unit, s_cur)
        if pending is not None:
            store(*pending)
        pending = (*unit, weighted_values(*unit, e4), sink_terms)
    store(*pending)


def _swa_and_proj(aqkv, sinks, h, w, batch, seq, casts=(), *, n_blocks=4):
    t, d = h.shape
    n_out = w.shape[1]
    tq = n_blocks * ATT_BLOCK
    nq = seq // tq
    kv_col = ATT_Q // (2 * ATT_KV)
    cast_specs = _side_cast_specs(casts, batch * nq, lambda b, n: b * nq + n)
    return pl.pallas_call(
        functools.partial(_swa_body, n_blocks=n_blocks),
        out_shape=[jax.ShapeDtypeStruct((t, ATT_Q), BF16), jax.ShapeDtypeStruct((t, n_out), BF16)]
        + [jax.ShapeDtypeStruct(a.shape, BF16) for a in casts],
        grid=(batch, nq),
        in_specs=[
            pl.BlockSpec(memory_space=pltpu.SMEM),
            pl.BlockSpec((tq, ATT_Q), lambda b, n: (b * nq + n, 0)),
            pl.BlockSpec((tq, 2 * ATT_KV), lambda b, n: (b * nq + n, kv_col)),
            pl.BlockSpec((ATT_BLOCK, 2 * ATT_KV),
                         lambda b, n: ((b * nq + n) * n_blocks - jnp.minimum(n, 1), kv_col)),
            pl.BlockSpec((tq, d), lambda b, n: (b * nq + n, 0)),
            _resident((d, n_out), lambda b, n: (0, 0)),
        ] + cast_specs,
        out_specs=[pl.BlockSpec((tq, ATT_Q), lambda b, n: (b * nq + n, 0)),
                   pl.BlockSpec((tq, n_out), lambda b, n: (b * nq + n, 0))] + cast_specs,
        compiler_params=_params("parallel", "arbitrary"),
        name="swa_proj",
    )(sinks, aqkv, aqkv, aqkv, h, w, *casts)


def _mix_body(h_ref, ret_ref, att_ref, gr_ref, ga_ref, wr_ref, wa_ref, wm_ref, g_ref, b_ref,
              o_ref, *, alpha):
    ret_branch = _dot(ret_ref[...], wr_ref[...])
    att_branch = _dot(att_ref[...], wa_ref[...])
    merged = (jax.nn.sigmoid(gr_ref[...].astype(F32)) * ret_branch
              + jax.nn.sigmoid(ga_ref[...].astype(F32)) * att_branch)
    mixed = _dot(merged.astype(BF16), wm_ref[...])
    o_ref[...] = _layer_norm(alpha * h_ref[...] + mixed, g_ref[...], b_ref[...])


def _mix_ln(h, ret, att, gates, w_ret, w_att, w_mix, g, b, *, alpha, tm=512):
    t, d = h.shape
    row = pl.BlockSpec((1, d), lambda i: (0, 0))
    return pl.pallas_call(
        functools.partial(_mix_body, alpha=alpha),
        out_shape=jax.ShapeDtypeStruct((t, d), F32),
        grid=(t // tm,),
        in_specs=[
            pl.BlockSpec((tm, d), lambda i: (i, 0)),
            pl.BlockSpec((tm, RET_V), lambda i: (i, 0)),
            pl.BlockSpec((tm, ATT_Q), lambda i: (i, 0)),
            pl.BlockSpec((tm, d), lambda i: (i, 0)),
            pl.BlockSpec((tm, d), lambda i: (i, 1)),
            _resident((RET_V, d), lambda i: (0, 0)),
            _resident((ATT_Q, d), lambda i: (0, 0)),
            _resident((d, d), lambda i: (0, 0)),
            row, row,
        ],
        out_specs=pl.BlockSpec((tm, d), lambda i: (i, 0)),
        compiler_params=_params("parallel"),
        name="mix_ln",
    )(h, ret, att, gates, gates, w_ret, w_att, w_mix, g, b)


def _ple_body(h_ref, p_ref, wg_ref, wp_ref, g_ref, b_ref, o_ref, *, alpha, n_sub):
    sub = h_ref.shape[0] // n_sub
    for r0 in range(0, h_ref.shape[0], sub):
        h = h_ref[r0:r0 + sub, :]
        gate = jax.nn.sigmoid(_dot(h.astype(BF16), wg_ref[...]))
        ple = gate * _dot(p_ref[r0:r0 + sub, :].astype(BF16), wp_ref[...])
        o_ref[r0:r0 + sub, :] = _layer_norm(alpha * h + ple, g_ref[...], b_ref[...])


def _ple_ln(h, p, w_gate, w_proj, g, b, *, alpha, tm=1024, n_sub=2):
    t, d = h.shape
    pd = p.shape[1]
    row = pl.BlockSpec((1, d), lambda i: (0, 0))
    return pl.pallas_call(
        functools.partial(_ple_body, alpha=alpha, n_sub=n_sub),
        out_shape=jax.ShapeDtypeStruct((t, d), F32),
        grid=(t // tm,),
        in_specs=[
            pl.BlockSpec((tm, d), lambda i: (i, 0)),
            pl.BlockSpec((tm, pd), lambda i: (i, 0)),
            _resident((d, d), lambda i: (0, 0)),
            _resident((pd, d), lambda i: (0, 0)),
            row, row,
        ],
        out_specs=pl.BlockSpec((tm, d), lambda i: (i, 0)),
        compiler_params=_params("parallel"),
        name="ple_ln",
    )(h, p, w_gate, w_proj, g, b)


def kernel(x, p, positions, ln_g, ln_b, w_ffn1_gu, w_ffn1_down, w_in, ret_gn_g, att_sinks,
           w_ret_out, w_att_out, w_mix_out, w_ffn2_gu, w_ffn2_down, w_ple_gate, w_ple_proj):
    batch, seq, d = x.shape
    depth = ln_g.shape[0]
    t = batch * seq
    alpha = (2.0 * depth) ** 0.25
    h = x.reshape(t, d)
    inner_decay, q_decay, k_decay, chunk_decay = _retention_decays()
    bounds = [0]
    for size in IN_SIZES:
        bounds.append(bounds[-1] + size)
    group_bounds = [bounds[0], bounds[2], bounds[4], bounds[7], bounds[9]]
    for i in range(depth):
        g = ln_g[i][:, None, :]
        b = ln_b[i][:, None, :]
        cr, sr, ca, s1, s2, w_qk, w_vg, w_att, w_gates, w1_gu, w1_down = _rope_tables(
            positions.reshape(t, 1), w_in[i], group_bounds, (w_ffn1_gu[i], w_ffn1_down[i]))

        h = _ffn_ln(h, w1_gu, w1_down, g[0], b[0], alpha=alpha)

        qkd = _proj(functools.partial(_proj_ret_qk_body, n_sub=2), h, w_qk, (cr, sr), (k_decay,),
                    n_out=3 * RET_QK, tm=1024, name="proj_ret_qk")
        aqkv = _proj(functools.partial(_proj_att_body, n_sub=2), h, w_att, (ca, s1, s2),
                     tm=1024, name="proj_att")
        att, vg, w2_gu, w2_down = _swa_and_proj(
            aqkv, att_sinks[i], h, w_vg, batch, seq, (w_ffn2_gu[i], w_ffn2_down[i]))
        ret, gates, w_ro, w_ao, w_mo, w_pg = _retention_and_proj(
            qkd, vg, inner_decay, q_decay, chunk_decay, ret_gn_g[i], h, w_gates, batch, seq,
            (w_ret_out[i], w_att_out[i], w_mix_out[i], w_ple_gate[i]))

        h = _mix_ln(h, ret, att, gates, w_ro, w_ao, w_mo, g[1], b[1], alpha=alpha)
        h = _ffn_ln(h, w2_gu, w2_down, g[2], b[2], alpha=alpha)
        h = _ple_ln(h, p[i].reshape(t, PLE_DIM), w_pg, w_ple_proj[i].astype(BF16), g[3], b[3],
                    alpha=alpha)
    return h.reshape(batch, seq, d)
```

```python
import functools

import jax
import jax.numpy as jnp
from jax import lax
from jax.experimental import pallas as pl
from jax.experimental.pallas import tpu as pltpu

F32 = jnp.float32
BF16 = jnp.bfloat16

D_MODEL = 2048
PLE_DIM = 256
D_FF = 5632
RET_HEADS = 8
RET_DK = 128
RET_DV = 256
RET_CHUNK = 256
RET_THETA = 10000.0
ATT_HEADS = 16
ATT_KV_HEADS = 4
ATT_DH = 64
ATT_GROUP = ATT_HEADS // ATT_KV_HEADS
WINDOW = 128
ATT_BLOCK = 128
ROPE_THETA = 500000.0
ROPE_DIMS = ATT_DH // 4
LN_EPS = 1e-5
GN_EPS = 1e-5

RET_QK = RET_HEADS * RET_DK
RET_V = RET_HEADS * RET_DV
ATT_Q = ATT_HEADS * ATT_DH
ATT_KV = ATT_KV_HEADS * ATT_DH
IN_SIZES = (RET_QK, RET_QK, RET_V, RET_V, ATT_Q, ATT_KV, ATT_KV, D_MODEL, D_MODEL)

LANES = 128
VMEM_LIMIT_BYTES = 60 * 1024 * 1024


def _params(*semantics):
    return pltpu.CompilerParams(dimension_semantics=semantics, vmem_limit_bytes=VMEM_LIMIT_BYTES)


def _resident(shape, index_map):
    return pl.BlockSpec(shape, index_map, pipeline_mode=pl.Buffered(1))


def _side_cast_specs(items, n_steps, step_of):
    in_specs, out_specs, out_shapes = [], [], []
    for item in items:
        if isinstance(item, tuple):
            a, lo, hi = item
            r, width = a.shape[0], hi - lo
            rows = r // n_steps
            in_specs.append(pl.BlockSpec(
                (rows, width), lambda *idx, cb=lo // width: (step_of(*idx), cb)))
            out_specs.append(pl.BlockSpec((rows, width), lambda *idx: (step_of(*idx), 0)))
            out_shapes.append(jax.ShapeDtypeStruct((r, width), BF16))
            continue
        r, c = item.shape
        gc = next(k for k in (1, 2, 4, 8) if (r * k) % (16 * n_steps) == 0 and c % (k * LANES) == 0)
        gr = n_steps // gc
        spec = pl.BlockSpec(
            (r // gr, c // gc), lambda *idx, gc=gc: (step_of(*idx) // gc, step_of(*idx) % gc))
        in_specs.append(spec)
        out_specs.append(spec)
        out_shapes.append(jax.ShapeDtypeStruct(item.shape, BF16))
    return in_specs, out_specs, out_shapes


def _cast_operands(items):
    return [item[0] if isinstance(item, tuple) else item for item in items]


def _side_casts(refs, n_fixed_out):
    n = (len(refs) - n_fixed_out) // 2
    for src, dst in zip(refs[:n], refs[n + n_fixed_out:]):
        dst[...] = src[...].astype(BF16)
    return refs[n:n + n_fixed_out]


def _layer_norm(y, g, b, eps=LN_EPS):
    mu = jnp.mean(y, axis=-1, keepdims=True)
    d = y - mu
    var = jnp.mean(d * d, axis=-1, keepdims=True)
    return d * lax.rsqrt(var + eps) * g + b


def _dot(a, b):
    return jnp.dot(a, b, preferred_element_type=F32)


def _ffn_body(x_ref, wg_ref, wu_ref, wd_ref, g_ref, b_ref, o_ref, xb_ref, *, alpha, n_sub):
    j = pl.program_id(1)
    sub = x_ref.shape[0] // n_sub

    @pl.when(j == 0)
    def _():
        x = x_ref[...]
        xb_ref[...] = x.astype(BF16)
        o_ref[...] = x

    for r in range(n_sub):
        rows = pl.ds(r * sub, sub)
        xb = xb_ref[rows, :]
        gate = _dot(xb, wg_ref[...])
        up = _dot(xb, wu_ref[...])
        act = (gate * jax.nn.sigmoid(gate) * up).astype(BF16)
        o_ref[rows, :] += (0.5 / alpha) * _dot(act, wd_ref[...])

    @pl.when(j == pl.num_programs(1) - 1)
    def _():
        for r in range(n_sub):
            rows = pl.ds(r * sub, sub)
            o_ref[rows, :] = _layer_norm(o_ref[rows, :], g_ref[...], b_ref[...],
                                         eps=LN_EPS / alpha ** 2)


def _ffn_ln(x, w_gu, w_down, g, b, *, alpha, tm=1024, tf=512, n_sub=2):
    t, d = x.shape
    f = w_down.shape[0]
    nj = f // tf
    return pl.pallas_call(
        functools.partial(_ffn_body, alpha=alpha, n_sub=n_sub),
        out_shape=jax.ShapeDtypeStruct((t, d), F32),
        grid=(t // tm, nj),
        in_specs=[
            pl.BlockSpec((tm, d), lambda i, j: (i, 0)),
            pl.BlockSpec((d, tf), lambda i, j: (0, j)),
            pl.BlockSpec((d, tf), lambda i, j: (0, j + nj)),
            pl.BlockSpec((tf, d), lambda i, j: (j, 0)),
            pl.BlockSpec((1, d), lambda i, j: (0, 0)),
            pl.BlockSpec((1, d), lambda i, j: (0, 0)),
        ],
        out_specs=pl.BlockSpec((tm, d), lambda i, j: (i, 0)),
        scratch_shapes=[pltpu.VMEM((tm, d), BF16)],
        compiler_params=_params("parallel", "arbitrary"),
        name="ffn_ln",
    )(x, w_gu, w_gu, w_down, g, b)


def _rope_body(pos_ref, inv_ref, *refs):
    cr_ref, sr_ref, ca_ref, s1_ref, s2_ref = _side_casts(refs, 5)
    pos = pos_ref[...].astype(F32)
    ang = pos * inv_ref[...]
    c = jnp.cos(ang)
    s = jnp.sin(ang)
    half_r = RET_DK // 2
    half_a = ROPE_DIMS // 2
    lane = lax.broadcasted_iota(jnp.int32, c.shape, 1)
    low = lane < half_r
    c_swap = pltpu.roll(c, half_r, axis=1)
    s_swap = pltpu.roll(s, half_r, axis=1)
    cr_ref[...] = jnp.where(low, c, c_swap)
    sr_ref[...] = jnp.where(low, -s, s_swap)
    c_att = jnp.where(low, c_swap, c)
    s_att = jnp.where(low, s_swap, s)
    in_head = lane & (ATT_DH - 1)
    first = in_head < half_a
    second = (in_head >= half_a) & (in_head < ROPE_DIMS)
    ca_ref[...] = jnp.where(first, c_att, jnp.where(second, pltpu.roll(c_att, half_a, axis=1), 1.0))
    s1_ref[...] = jnp.where(second, pltpu.roll(s_att, half_a, axis=1), 0.0)
    s2_ref[...] = jnp.where(first, -s_att, 0.0)


def _rope_tables(pos, casts=(), tm=1024):
    t = pos.shape[0]
    n_steps = t // tm
    cast_in, cast_out, cast_shapes = _side_cast_specs(casts, n_steps, lambda i: i)
    half_r = RET_DK // 2
    half_a = ROPE_DIMS // 2
    inv_r = 1.0 / (RET_THETA ** (jnp.arange(half_r, dtype=F32) / half_r))
    inv_a = 1.0 / (ROPE_THETA ** (jnp.arange(half_a, dtype=F32) / half_a))
    inv = jnp.concatenate([inv_r, inv_a, jnp.zeros((LANES - half_r - half_a,), F32)])[None, :]
    tab = pl.BlockSpec((tm, LANES), lambda i: (i, 0))
    return pl.pallas_call(
        _rope_body,
        out_shape=[jax.ShapeDtypeStruct((t, LANES), F32)] * 5 + cast_shapes,
        grid=(n_steps,),
        in_specs=[pl.BlockSpec((tm, 1), lambda i: (i, 0)),
                  pl.BlockSpec((1, LANES), lambda i: (0, 0))] + cast_in,
        out_specs=[tab] * 5 + cast_out,
        compiler_params=_params("parallel"),
        name="rope_tables",
    )(pos, inv, *_cast_operands(casts))


def _proj_ret_qk_body(x_ref, w_ref, cr_ref, sr_ref, kdec_ref, w_in_ref, o_ref, *seg_refs,
                      n_sub, seg_bounds):
    for seg_ref, (lo, hi) in zip(seg_refs, seg_bounds):
        seg_ref[...] = w_in_ref[:, lo:hi].astype(BF16)
    sub = x_ref.shape[0] // n_sub
    k_scale = RET_DK ** -0.5
    for r0 in range(0, x_ref.shape[0], sub):
        r = _dot(x_ref[r0:r0 + sub, :].astype(BF16), w_ref[...])
        cr = cr_ref[r0:r0 + sub, :]
        sr = sr_ref[r0:r0 + sub, :]
        for s in range(2 * RET_HEADS):
            x = r[:, s * LANES:(s + 1) * LANES]
            rot = x * cr + pltpu.roll(x, RET_DK // 2, axis=1) * sr
            if s < RET_HEADS:
                o_ref[r0:r0 + sub, s * LANES:(s + 1) * LANES] = rot.astype(BF16)
                continue
            rot = rot * k_scale
            o_ref[r0:r0 + sub, s * LANES:(s + 1) * LANES] = rot.astype(BF16)
            k_decay = kdec_ref[:, (s - RET_HEADS) * LANES:(s - RET_HEADS + 1) * LANES]
            d = s + RET_HEADS
            for c0 in range(0, sub, RET_CHUNK):
                o_ref[r0 + c0:r0 + c0 + RET_CHUNK, d * LANES:(d + 1) * LANES] = (
                    rot[c0:c0 + RET_CHUNK] * k_decay).astype(BF16)


def _proj_att_body(x_ref, w_ref, ca_ref, s1_ref, s2_ref, o_ref, *, n_sub):
    sub = x_ref.shape[0] // n_sub
    half = ROPE_DIMS // 2
    n_rot = (ATT_Q + ATT_KV) // LANES
    for r0 in range(0, x_ref.shape[0], sub):
        r = _dot(x_ref[r0:r0 + sub, :].astype(BF16), w_ref[...])
        ca = ca_ref[r0:r0 + sub, :]
        s1 = s1_ref[r0:r0 + sub, :]
        s2 = s2_ref[r0:r0 + sub, :]
        for s in range(n_rot):
            x = r[:, s * LANES:(s + 1) * LANES]
            rot = (x * ca + pltpu.roll(x, half, axis=1) * s1
                   + pltpu.roll(x, LANES - half, axis=1) * s2)
            o_ref[r0:r0 + sub, s * LANES:(s + 1) * LANES] = rot.astype(BF16)
        o_ref[r0:r0 + sub, n_rot * LANES:] = r[:, n_rot * LANES:].astype(BF16)


def _proj(body, x, w, tables=(), consts=(), split=None, *, n_out=None, tm=512, name):
    t, d = x.shape
    n = w.shape[1]
    n_out = n if n_out is None else n_out
    n_steps = t // tm
    tab = pl.BlockSpec((tm, LANES), lambda i: (i, 0))
    in_specs = [pl.BlockSpec((tm, d), lambda i: (i, 0)), _resident((d, n), lambda i: (0, 0))]
    in_specs += [tab] * len(tables) + [_resident(c.shape, lambda i: (0, 0)) for c in consts]
    out_shape = [jax.ShapeDtypeStruct((t, n_out), BF16)]
    out_specs = [pl.BlockSpec((tm, n_out), lambda i: (i, 0))]
    operands = [x, w, *tables, *consts]
    if split is not None:
        src, ranges = split
        rows = src.shape[0] // n_steps
        in_specs.append(pl.BlockSpec((rows, src.shape[1]), lambda i: (i, 0)))
        operands.append(src)
        for lo, hi in ranges:
            out_shape.append(jax.ShapeDtypeStruct((src.shape[0], hi - lo), BF16))
            out_specs.append(pl.BlockSpec((rows, hi - lo), lambda i: (i, 0)))
    return pl.pallas_call(
        body,
        out_shape=out_shape,
        grid=(n_steps,),
        in_specs=in_specs,
        out_specs=out_specs,
        compiler_params=_params("parallel"),
        name=name,
    )(*operands)


def _retention_body(q_ref, k_ref, kd_ref, v_ref, g_ref, inner_ref, qd_ref, cd_ref, gn_ref,
                    h_ref, w_ref, *refs, n_chunks, proj_every):
    state_ref = refs[-1]
    o_ref, proj_ref = _side_casts(refs[:-1], 2)

    @pl.when(pl.program_id(2) == 0)
    def _():
        state_ref[...] = jnp.zeros_like(state_ref)

    inner_decay = inner_ref[...]
    q_decay = qd_ref[...]
    chunk_decay = cd_ref[...]
    gn = gn_ref[...]
    hb = h_ref[...].astype(BF16)
    proj_cols = proj_ref.shape[1] * proj_every // n_chunks
    for c in range(n_chunks):
        if c % proj_every == 0:
            cols = pl.ds((c // proj_every) * proj_cols, proj_cols)
            proj_ref[:, cols] = _dot(hb, w_ref[:, cols]).astype(BF16)
        rows = pl.ds(c * RET_CHUNK, RET_CHUNK)
        qn = q_ref[rows, :]
        vn = v_ref[rows, :]
        scores = lax.dot_general(qn, k_ref[rows, :], (((1,), (1,)), ((), ())),
                                 preferred_element_type=F32)
        inner = _dot((scores * inner_decay).astype(BF16), vn)
        state = state_ref[...]
        cross = _dot(qn, state.astype(BF16)) * q_decay
        state_ref[...] = state * chunk_decay + lax.dot_general(
            kd_ref[rows, :], vn, (((0,), (0,)), ((), ())), preferred_element_type=F32)
        o = inner + cross
        mu = jnp.mean(o, axis=-1, keepdims=True)
        d = o - mu
        var = jnp.mean(d * d, axis=-1, keepdims=True)
        y = d * lax.rsqrt(var + GN_EPS) * gn
        gate = g_ref[rows, :].astype(F32)
        o_ref[rows, :] = (gate * jax.nn.sigmoid(gate) * y).astype(BF16)


def _retention_decays():
    c = RET_CHUNK
    log_g = jnp.log1p(-jnp.exp2(-5.0 - jnp.arange(RET_HEADS, dtype=F32)))
    idx = jnp.arange(c, dtype=F32)
    q_decay = jnp.exp((idx[None, :] + 1.0) * log_g[:, None])
    k_decay = jnp.exp((c - 1.0 - idx)[None, :] * log_g[:, None])
    diff = idx[:, None] - idx[None, :]
    inner = jnp.where(diff[None] >= 0,
                      jnp.exp(jnp.maximum(diff, 0.0)[None] * log_g[:, None, None]), 0.0)
    chunk_decay = jnp.exp(c * log_g)
    q_decay = jnp.broadcast_to(q_decay[:, :, None], (RET_HEADS, c, RET_DV))
    k_decay = jnp.broadcast_to(k_decay.T[:, :, None], (c, RET_HEADS, RET_DK)).reshape(c, RET_QK)
    chunk_decay = jnp.broadcast_to(chunk_decay[:, None, None], (RET_HEADS, 1, RET_DV))
    return inner, q_decay, k_decay, chunk_decay


def _retention_and_proj(qkd, vg, inner, q_decay, chunk_decay, gn_g, h, w, batch, seq, casts=(), *,
                        ts=4096, proj_every=1):
    t, d = h.shape
    n_out = w.shape[1]
    ns = seq // ts
    tp = t // (batch * RET_HEADS * ns)
    gn = gn_g.reshape(RET_HEADS, 1, RET_DV)

    def rows(b, hd, s):
        return b * ns + s

    def tile(b, hd, s):
        return (b * RET_HEADS + hd) * ns + s

    head_tab = lambda shape: pl.BlockSpec((None,) + shape, lambda b, hd, s: (hd, 0, 0))
    cast_in, cast_out, cast_shapes = _side_cast_specs(casts, batch * RET_HEADS * ns, tile)
    return pl.pallas_call(
        functools.partial(_retention_body, n_chunks=ts // RET_CHUNK, proj_every=proj_every),
        out_shape=[jax.ShapeDtypeStruct((t, RET_V), BF16), jax.ShapeDtypeStruct((t, n_out), BF16)]
        + cast_shapes,
        grid=(batch, RET_HEADS, ns),
        in_specs=[
            pl.BlockSpec((ts, RET_DK), lambda b, hd, s: (rows(b, hd, s), hd)),
            pl.BlockSpec((ts, RET_DK), lambda b, hd, s: (rows(b, hd, s), RET_HEADS + hd)),
            pl.BlockSpec((ts, RET_DK), lambda b, hd, s: (rows(b, hd, s), 2 * RET_HEADS + hd)),
            pl.BlockSpec((ts, RET_DV), lambda b, hd, s: (rows(b, hd, s), hd)),
            pl.BlockSpec((ts, RET_DV), lambda b, hd, s: (rows(b, hd, s), RET_HEADS + hd)),
            head_tab((RET_CHUNK, RET_CHUNK)),
            head_tab((RET_CHUNK, RET_DV)),
            head_tab((1, RET_DV)),
            head_tab((1, RET_DV)),
            pl.BlockSpec((tp, d), lambda b, hd, s: (tile(b, hd, s), 0)),
            _resident((d, n_out), lambda b, hd, s: (0, 0)),
        ] + cast_in,
        out_specs=[pl.BlockSpec((ts, RET_DV), lambda b, hd, s: (rows(b, hd, s), hd)),
                   pl.BlockSpec((tp, n_out), lambda b, hd, s: (tile(b, hd, s), 0))] + cast_out,
        scratch_shapes=[pltpu.VMEM((RET_DK, RET_DV), F32)],
        compiler_params=_params("parallel", "parallel", "arbitrary"),
        name="retention_proj",
    )(qkd, qkd, qkd, vg, vg, inner, q_decay, chunk_decay, gn, h, w, *_cast_operands(casts))


def _swa_body(sink_ref, q_ref, kvc_ref, kvp_ref, h_ref, w_ref, *refs, n_blocks):
    o_ref, proj_ref = _side_casts(refs, 2)
    n = pl.program_id(1)
    c = ATT_BLOCK
    qi = lax.broadcasted_iota(jnp.int32, (c, 2 * c), 0) + c
    kj = lax.broadcasted_iota(jnp.int32, (c, 2 * c), 1)
    rel = qi - kj
    in_window = (rel >= 0) & (rel < WINDOW)
    valid_first = in_window & (kj >= jnp.where(n > 0, 0, c))
    scale = ATT_DH ** -0.5

    def heads_of(hk):
        return [hk * ATT_GROUP + g for g in range(ATT_GROUP)]

    def band(qb, hk, offset):
        cols = slice(offset + hk * ATT_DH, offset + (hk + 1) * ATT_DH)
        prev = kvp_ref[:, cols] if qb == 0 else kvc_ref[(qb - 1) * c:qb * c, cols]
        return jnp.concatenate([prev, kvc_ref[qb * c:(qb + 1) * c, cols]], axis=0)

    def scores(qb, hk):
        q4 = jnp.concatenate([q_ref[qb * c:(qb + 1) * c, hd * ATT_DH:(hd + 1) * ATT_DH]
                              for hd in heads_of(hk)], axis=0)
        return lax.dot_general(q4 * scale, band(qb, hk, 0), (((1,), (1,)), ((), ())),
                               preferred_element_type=F32)

    def softmax(qb, hk, s4):
        valid = valid_first if qb == 0 else in_window
        masked = [jnp.where(valid, s4[g * c:(g + 1) * c], -jnp.inf) for g in range(ATT_GROUP)]
        sinks = [sink_ref[hd] for hd in heads_of(hk)]
        maxes = [jnp.maximum(jnp.max(s, axis=-1, keepdims=True), sink)
                 for s, sink in zip(masked, sinks)]
        probs = [jnp.exp(s - m).astype(BF16) for s, m in zip(masked, maxes)]
        sink_terms = [jnp.exp(sink - m) for sink, m in zip(sinks, maxes)]
        return jnp.concatenate(probs, axis=0), sink_terms

    ones = jnp.ones((2 * c, 2 * c - ATT_DH), BF16)

    def weighted_values(qb, hk, e4):
        return _dot(e4, jnp.concatenate([band(qb, hk, ATT_KV), ones], axis=1))

    def store(qb, hk, ov, sink_terms):
        groups = [ov[g * c:(g + 1) * c] for g in range(ATT_GROUP)]
        inv = [1.0 / (og[:, c:c + ATT_DH] + st) for og, st in zip(groups, sink_terms)]
        outs = [og[:, :ATT_DH] * r for og, r in zip(groups, inv)]
        for pair in range(ATT_GROUP // 2):
            col = heads_of(hk)[2 * pair] * ATT_DH
            o_ref[qb * c:(qb + 1) * c, col:col + 2 * ATT_DH] = jnp.concatenate(
                [outs[2 * pair], outs[2 * pair + 1]], axis=1).astype(BF16)

    units = [(qb, hk) for hk in range(ATT_KV_HEADS) for qb in range(n_blocks)]
    hb = h_ref[...].astype(BF16)
    chunk = proj_ref.shape[1] // len(units)

    def project(u):
        cols = slice(u * chunk, (u + 1) * chunk)
        proj_ref[:, cols] = _dot(hb, w_ref[:, cols]).astype(BF16)

    s_next = scores(*units[0])
    pending = None
    for u, unit in enumerate(units):
        s_cur = s_next
        if u + 1 < len(units):
            s_next = scores(*units[u + 1])
        project(u)
        e4, sink_terms = softmax(*unit, s_cur)
        if pending is not None:
            store(*pending)
        pending = (*unit, weighted_values(*unit, e4), sink_terms)
    store(*pending)


def _swa_and_proj(aqkv, sinks, h, w, batch, seq, casts=(), *, n_blocks=4):
    t, d = h.shape
    n_out = w.shape[1]
    tq = n_blocks * ATT_BLOCK
    nq = seq // tq
    kv_col = ATT_Q // (2 * ATT_KV)
    cast_in, cast_out, cast_shapes = _side_cast_specs(casts, batch * nq, lambda b, n: b * nq + n)
    return pl.pallas_call(
        functools.partial(_swa_body, n_blocks=n_blocks),
        out_shape=[jax.ShapeDtypeStruct((t, ATT_Q), BF16), jax.ShapeDtypeStruct((t, n_out), BF16)]
        + cast_shapes,
        grid=(batch, nq),
        in_specs=[
            pl.BlockSpec(memory_space=pltpu.SMEM),
            pl.BlockSpec((tq, ATT_Q), lambda b, n: (b * nq + n, 0)),
            pl.BlockSpec((tq, 2 * ATT_KV), lambda b, n: (b * nq + n, kv_col)),
            pl.BlockSpec((ATT_BLOCK, 2 * ATT_KV),
                         lambda b, n: ((b * nq + n) * n_blocks - jnp.minimum(n, 1), kv_col)),
            pl.BlockSpec((tq, d), lambda b, n: (b * nq + n, 0)),
            _resident((d, n_out), lambda b, n: (0, 0)),
        ] + cast_in,
        out_specs=[pl.BlockSpec((tq, ATT_Q), lambda b, n: (b * nq + n, 0)),
                   pl.BlockSpec((tq, n_out), lambda b, n: (b * nq + n, 0))] + cast_out,
        compiler_params=_params("parallel", "arbitrary"),
        name="swa_proj",
    )(sinks, aqkv, aqkv, aqkv, h, w, *_cast_operands(casts))


def _mix_body(h_ref, ret_ref, att_ref, gr_ref, ga_ref, wr_ref, wa_ref, wm_ref, g_ref, b_ref,
              o_ref, *, alpha):
    ret_branch = _dot(ret_ref[...], wr_ref[...])
    att_branch = _dot(att_ref[...], wa_ref[...])
    merged = (jax.nn.sigmoid(gr_ref[...].astype(F32)) * ret_branch
              + jax.nn.sigmoid(ga_ref[...].astype(F32)) * att_branch)
    mixed = _dot(merged.astype(BF16), wm_ref[...])
    o_ref[...] = _layer_norm(alpha * h_ref[...] + mixed, g_ref[...], b_ref[...])


def _mix_ln(h, ret, att, gates, w_ret, w_att, w_mix, g, b, *, alpha, tm=512):
    t, d = h.shape
    row = pl.BlockSpec((1, d), lambda i: (0, 0))
    return pl.pallas_call(
        functools.partial(_mix_body, alpha=alpha),
        out_shape=jax.ShapeDtypeStruct((t, d), F32),
        grid=(t // tm,),
        in_specs=[
            pl.BlockSpec((tm, d), lambda i: (i, 0)),
            pl.BlockSpec((tm, RET_V), lambda i: (i, 0)),
            pl.BlockSpec((tm, ATT_Q), lambda i: (i, 0)),
            pl.BlockSpec((tm, d), lambda i: (i, 0)),
            pl.BlockSpec((tm, d), lambda i: (i, 1)),
            _resident((RET_V, d), lambda i: (0, 0)),
            _resident((ATT_Q, d), lambda i: (0, 0)),
            _resident((d, d), lambda i: (0, 0)),
            row, row,
        ],
        out_specs=pl.BlockSpec((tm, d), lambda i: (i, 0)),
        compiler_params=_params("parallel"),
        name="mix_ln",
    )(h, ret, att, gates, gates, w_ret, w_att, w_mix, g, b)


def _ple_body(h_ref, p_ref, wg_ref, wp_ref, g_ref, b_ref, o_ref, *, alpha, n_sub):
    sub = h_ref.shape[0] // n_sub
    for r0 in range(0, h_ref.shape[0], sub):
        h = h_ref[r0:r0 + sub, :]
        gate = jax.nn.sigmoid(_dot(h.astype(BF16), wg_ref[...]))
        ple = gate * _dot(p_ref[r0:r0 + sub, :].astype(BF16), wp_ref[...])
        o_ref[r0:r0 + sub, :] = _layer_norm(alpha * h + ple, g_ref[...], b_ref[...])


def _ple_ln(h, p, w_gate, w_proj, g, b, *, alpha, tm=1024, n_sub=2):
    t, d = h.shape
    pd = p.shape[1]
    row = pl.BlockSpec((1, d), lambda i: (0, 0))
    return pl.pallas_call(
        functools.partial(_ple_body, alpha=alpha, n_sub=n_sub),
        out_shape=jax.ShapeDtypeStruct((t, d), F32),
        grid=(t // tm,),
        in_specs=[
            pl.BlockSpec((tm, d), lambda i: (i, 0)),
            pl.BlockSpec((tm, pd), lambda i: (i, 0)),
            _resident((d, d), lambda i: (0, 0)),
            _resident((pd, d), lambda i: (0, 0)),
            row, row,
        ],
        out_specs=pl.BlockSpec((tm, d), lambda i: (i, 0)),
        compiler_params=_params("parallel"),
        name="ple_ln",
    )(h, p, w_gate, w_proj, g, b)


def kernel(x, p, positions, ln_g, ln_b, w_ffn1_gu, w_ffn1_down, w_in, ret_gn_g, att_sinks,
           w_ret_out, w_att_out, w_mix_out, w_ffn2_gu, w_ffn2_down, w_ple_gate, w_ple_proj):
    batch, seq, d = x.shape
    depth = ln_g.shape[0]
    t = batch * seq
    alpha = (2.0 * depth) ** 0.25
    h = x.reshape(t, d)
    inner_decay, q_decay, k_decay, chunk_decay = _retention_decays()
    bounds = [0]
    for size in IN_SIZES:
        bounds.append(bounds[-1] + size)
    qk_cols, vg_cols = (bounds[0], bounds[2]), (bounds[2], bounds[4])
    att_cols, gate_cols = (bounds[4], bounds[7]), (bounds[7], bounds[9])
    for i in range(depth):
        g = ln_g[i][:, None, :]
        b = ln_b[i][:, None, :]
        cr, sr, ca, s1, s2, w_qk, w_att, w1_gu, w1_down = _rope_tables(
            positions.reshape(t, 1),
            ((w_in[i], *qk_cols), (w_in[i], *att_cols), w_ffn1_gu[i], w_ffn1_down[i]))

        h = _ffn_ln(h, w1_gu, w1_down, g[0], b[0], alpha=alpha)

        qkd, w_vg, w_gates = _proj(
            functools.partial(_proj_ret_qk_body, n_sub=2, seg_bounds=(vg_cols, gate_cols)),
            h, w_qk, (cr, sr), (k_decay,), (w_in[i], (vg_cols, gate_cols)),
            n_out=3 * RET_QK, tm=1024, name="proj_ret_qk")
        aqkv, = _proj(functools.partial(_proj_att_body, n_sub=2), h, w_att, (ca, s1, s2),
                      tm=1024, name="proj_att")
        att, vg, w2_gu, w2_down = _swa_and_proj(
            aqkv, att_sinks[i], h, w_vg, batch, seq, (w_ffn2_gu[i], w_ffn2_down[i]))
        ret, gates, w_ro, w_ao, w_mo, w_pg = _retention_and_proj(
            qkd, vg, inner_decay, q_decay, chunk_decay, ret_gn_g[i], h, w_gates, batch, seq,
            (w_ret_out[i], w_att_out[i], w_mix_out[i], w_ple_gate[i]))

        h = _mix_ln(h, ret, att, gates, w_ro, w_ao, w_mo, g[1], b[1], alpha=alpha)
        h = _ffn_ln(h, w2_gu, w2_down, g[2], b[2], alpha=alpha)
        h = _ple_ln(h, p[i].reshape(t, PLE_DIM), w_pg, w_ple_proj[i].astype(BF16), g[3], b[3],
                    alpha=alpha)
    return h.reshape(batch, seq, d)
```

```python
import functools

import jax
import jax.numpy as jnp
from jax import lax
from jax.experimental import pallas as pl
from jax.experimental.pallas import tpu as pltpu

F32 = jnp.float32
BF16 = jnp.bfloat16

D_MODEL = 2048
PLE_DIM = 256
D_FF = 5632
RET_HEADS = 8
RET_DK = 128
RET_DV = 256
RET_CHUNK = 256
RET_THETA = 10000.0
ATT_HEADS = 16
ATT_KV_HEADS = 4
ATT_DH = 64
ATT_GROUP = ATT_HEADS // ATT_KV_HEADS
WINDOW = 128
ATT_BLOCK = 128
ROPE_THETA = 500000.0
ROPE_DIMS = ATT_DH // 4
LN_EPS = 1e-5
GN_EPS = 1e-5

RET_QK = RET_HEADS * RET_DK
RET_V = RET_HEADS * RET_DV
ATT_Q = ATT_HEADS * ATT_DH
ATT_KV = ATT_KV_HEADS * ATT_DH
IN_SIZES = (RET_QK, RET_QK, RET_V, RET_V, ATT_Q, ATT_KV, ATT_KV, D_MODEL, D_MODEL)

LANES = 128
VMEM_LIMIT_BYTES = 60 * 1024 * 1024


def _params(*semantics):
    return pltpu.CompilerParams(dimension_semantics=semantics, vmem_limit_bytes=VMEM_LIMIT_BYTES)


def _resident(shape, index_map):
    return pl.BlockSpec(shape, index_map, pipeline_mode=pl.Buffered(1))


def _side_cast_specs(items, n_steps, step_of):
    in_specs, out_specs, out_shapes = [], [], []
    for item in items:
        if isinstance(item, tuple):
            a, lo, hi = item
            r, width = a.shape[0], hi - lo
            rows = r // n_steps
            in_specs.append(pl.BlockSpec(
                (rows, width), lambda *idx, cb=lo // width: (step_of(*idx), cb)))
            out_specs.append(pl.BlockSpec((rows, width), lambda *idx: (step_of(*idx), 0)))
            out_shapes.append(jax.ShapeDtypeStruct((r, width), BF16))
            continue
        r, c = item.shape
        gc = next(k for k in (1, 2, 4, 8) if (r * k) % (16 * n_steps) == 0 and c % (k * LANES) == 0)
        gr = n_steps // gc
        spec = pl.BlockSpec(
            (r // gr, c // gc), lambda *idx, gc=gc: (step_of(*idx) // gc, step_of(*idx) % gc))
        in_specs.append(spec)
        out_specs.append(spec)
        out_shapes.append(jax.ShapeDtypeStruct(item.shape, BF16))
    return in_specs, out_specs, out_shapes


def _cast_operands(items):
    return [item[0] if isinstance(item, tuple) else item for item in items]


def _side_casts(refs, n_fixed_out):
    n = (len(refs) - n_fixed_out) // 2
    for src, dst in zip(refs[:n], refs[n + n_fixed_out:]):
        dst[...] = src[...].astype(BF16)
    return refs[n:n + n_fixed_out]


def _layer_norm(y, g, b, eps=LN_EPS):
    mu = jnp.mean(y, axis=-1, keepdims=True)
    d = y - mu
    var = jnp.mean(d * d, axis=-1, keepdims=True)
    return d * lax.rsqrt(var + eps) * g + b


def _dot(a, b):
    return jnp.dot(a, b, preferred_element_type=F32)


def _ffn_body(x_ref, wg_ref, wu_ref, wd_ref, g_ref, b_ref, o_ref, *, alpha, n_sub):
    j = pl.program_id(1)
    sub = x_ref.shape[0] // n_sub

    @pl.when(j == 0)
    def _():
        o_ref[...] = x_ref[...]

    for r in range(n_sub):
        rows = pl.ds(r * sub, sub)
        xb = x_ref[rows, :].astype(BF16)
        gate = _dot(xb, wg_ref[...])
        up = _dot(xb, wu_ref[...])
        act = (gate * jax.nn.sigmoid(gate) * up).astype(BF16)
        o_ref[rows, :] += (0.5 / alpha) * _dot(act, wd_ref[...])

    @pl.when(j == pl.num_programs(1) - 1)
    def _():
        for r in range(n_sub):
            rows = pl.ds(r * sub, sub)
            o_ref[rows, :] = _layer_norm(o_ref[rows, :], g_ref[...], b_ref[...],
                                         eps=LN_EPS / alpha ** 2)


def _ffn_ln(x, w_gu, w_down, g, b, *, alpha, tm=1024, tf=512, n_sub=2):
    t, d = x.shape
    f = w_down.shape[0]
    nj = f // tf
    return pl.pallas_call(
        functools.partial(_ffn_body, alpha=alpha, n_sub=n_sub),
        out_shape=jax.ShapeDtypeStruct((t, d), F32),
        grid=(t // tm, nj),
        in_specs=[
            pl.BlockSpec((tm, d), lambda i, j: (i, 0)),
            pl.BlockSpec((d, tf), lambda i, j: (0, j)),
            pl.BlockSpec((d, tf), lambda i, j: (0, j + nj)),
            pl.BlockSpec((tf, d), lambda i, j: (j, 0)),
            pl.BlockSpec((1, d), lambda i, j: (0, 0)),
            pl.BlockSpec((1, d), lambda i, j: (0, 0)),
        ],
        out_specs=pl.BlockSpec((tm, d), lambda i, j: (i, 0)),
        compiler_params=_params("parallel", "arbitrary"),
        name="ffn_ln",
    )(x, w_gu, w_gu, w_down, g, b)


def _rope_body(pos_ref, inv_ref, *refs):
    cr_ref, sr_ref, ca_ref, s1_ref, s2_ref = _side_casts(refs, 5)
    pos = pos_ref[...].astype(F32)
    ang = pos * inv_ref[...]
    c = jnp.cos(ang)
    s = jnp.sin(ang)
    half_r = RET_DK // 2
    half_a = ROPE_DIMS // 2
    lane = lax.broadcasted_iota(jnp.int32, c.shape, 1)
    low = lane < half_r
    c_swap = pltpu.roll(c, half_r, axis=1)
    s_swap = pltpu.roll(s, half_r, axis=1)
    cr_ref[...] = jnp.where(low, c, c_swap)
    sr_ref[...] = jnp.where(low, -s, s_swap)
    c_att = jnp.where(low, c_swap, c)
    s_att = jnp.where(low, s_swap, s)
    in_head = lane & (ATT_DH - 1)
    first = in_head < half_a
    second = (in_head >= half_a) & (in_head < ROPE_DIMS)
    ca_ref[...] = jnp.where(first, c_att, jnp.where(second, pltpu.roll(c_att, half_a, axis=1), 1.0))
    s1_ref[...] = jnp.where(second, pltpu.roll(s_att, half_a, axis=1), 0.0)
    s2_ref[...] = jnp.where(first, -s_att, 0.0)


def _rope_tables(pos, casts=(), tm=1024):
    t = pos.shape[0]
    n_steps = t // tm
    cast_in, cast_out, cast_shapes = _side_cast_specs(casts, n_steps, lambda i: i)
    half_r = RET_DK // 2
    half_a = ROPE_DIMS // 2
    inv_r = 1.0 / (RET_THETA ** (jnp.arange(half_r, dtype=F32) / half_r))
    inv_a = 1.0 / (ROPE_THETA ** (jnp.arange(half_a, dtype=F32) / half_a))
    inv = jnp.concatenate([inv_r, inv_a, jnp.zeros((LANES - half_r - half_a,), F32)])[None, :]
    tab = pl.BlockSpec((tm, LANES), lambda i: (i, 0))
    return pl.pallas_call(
        _rope_body,
        out_shape=[jax.ShapeDtypeStruct((t, LANES), F32)] * 5 + cast_shapes,
        grid=(n_steps,),
        in_specs=[pl.BlockSpec((tm, 1), lambda i: (i, 0)),
                  pl.BlockSpec((1, LANES), lambda i: (0, 0))] + cast_in,
        out_specs=[tab] * 5 + cast_out,
        compiler_params=_params("parallel"),
        name="rope_tables",
    )(pos, inv, *_cast_operands(casts))


def _proj_ret_qk_body(x_ref, w_ref, cr_ref, sr_ref, kdec_ref, w_in_ref, o_ref, *seg_refs,
                      n_sub, seg_bounds):
    for seg_ref, (lo, hi) in zip(seg_refs, seg_bounds):
        seg_ref[...] = w_in_ref[:, lo:hi].astype(BF16)
    sub = x_ref.shape[0] // n_sub
    k_scale = RET_DK ** -0.5
    for r0 in range(0, x_ref.shape[0], sub):
        r = _dot(x_ref[r0:r0 + sub, :].astype(BF16), w_ref[...])
        cr = cr_ref[r0:r0 + sub, :]
        sr = sr_ref[r0:r0 + sub, :]
        for s in range(2 * RET_HEADS):
            x = r[:, s * LANES:(s + 1) * LANES]
            rot = x * cr + pltpu.roll(x, RET_DK // 2, axis=1) * sr
            if s < RET_HEADS:
                o_ref[r0:r0 + sub, s * LANES:(s + 1) * LANES] = rot.astype(BF16)
                continue
            rot = rot * k_scale
            o_ref[r0:r0 + sub, s * LANES:(s + 1) * LANES] = rot.astype(BF16)
            k_decay = kdec_ref[:, (s - RET_HEADS) * LANES:(s - RET_HEADS + 1) * LANES]
            d = s + RET_HEADS
            for c0 in range(0, sub, RET_CHUNK):
                o_ref[r0 + c0:r0 + c0 + RET_CHUNK, d * LANES:(d + 1) * LANES] = (
                    rot[c0:c0 + RET_CHUNK] * k_decay).astype(BF16)


def _proj_att_body(x_ref, w_ref, ca_ref, s1_ref, s2_ref, o_ref, *, n_sub):
    sub = x_ref.shape[0] // n_sub
    half = ROPE_DIMS // 2
    n_rot = (ATT_Q + ATT_KV) // LANES
    for r0 in range(0, x_ref.shape[0], sub):
        r = _dot(x_ref[r0:r0 + sub, :].astype(BF16), w_ref[...])
        ca = ca_ref[r0:r0 + sub, :]
        s1 = s1_ref[r0:r0 + sub, :]
        s2 = s2_ref[r0:r0 + sub, :]
        for s in range(n_rot):
            x = r[:, s * LANES:(s + 1) * LANES]
            rot = (x * ca + pltpu.roll(x, half, axis=1) * s1
                   + pltpu.roll(x, LANES - half, axis=1) * s2)
            o_ref[r0:r0 + sub, s * LANES:(s + 1) * LANES] = rot.astype(BF16)
        o_ref[r0:r0 + sub, n_rot * LANES:] = r[:, n_rot * LANES:].astype(BF16)


def _proj(body, x, w, tables=(), consts=(), split=None, *, n_out=None, tm=512, name):
    t, d = x.shape
    n = w.shape[1]
    n_out = n if n_out is None else n_out
    n_steps = t // tm
    tab = pl.BlockSpec((tm, LANES), lambda i: (i, 0))
    in_specs = [pl.BlockSpec((tm, d), lambda i: (i, 0)), _resident((d, n), lambda i: (0, 0))]
    in_specs += [tab] * len(tables) + [_resident(c.shape, lambda i: (0, 0)) for c in consts]
    out_shape = [jax.ShapeDtypeStruct((t, n_out), BF16)]
    out_specs = [pl.BlockSpec((tm, n_out), lambda i: (i, 0))]
    operands = [x, w, *tables, *consts]
    if split is not None:
        src, ranges = split
        rows = src.shape[0] // n_steps
        in_specs.append(pl.BlockSpec((rows, src.shape[1]), lambda i: (i, 0)))
        operands.append(src)
        for lo, hi in ranges:
            out_shape.append(jax.ShapeDtypeStruct((src.shape[0], hi - lo), BF16))
            out_specs.append(pl.BlockSpec((rows, hi - lo), lambda i: (i, 0)))
    return pl.pallas_call(
        body,
        out_shape=out_shape,
        grid=(n_steps,),
        in_specs=in_specs,
        out_specs=out_specs,
        compiler_params=_params("parallel"),
        name=name,
    )(*operands)


def _retention_body(q_ref, k_ref, kd_ref, v_ref, g_ref, inner_ref, qd_ref, cd_ref, gn_ref,
                    h_ref, w_ref, *refs, n_chunks, proj_every):
    state_ref = refs[-1]
    o_ref, proj_ref = _side_casts(refs[:-1], 2)

    @pl.when(pl.program_id(2) == 0)
    def _():
        state_ref[...] = jnp.zeros_like(state_ref)

    inner_decay = inner_ref[...]
    q_decay = qd_ref[...]
    chunk_decay = cd_ref[...]
    gn = gn_ref[...]
    hb = h_ref[...].astype(BF16)
    proj_cols = proj_ref.shape[1] * proj_every // n_chunks
    for c in range(n_chunks):
        if c % proj_every == 0:
            cols = pl.ds((c // proj_every) * proj_cols, proj_cols)
            proj_ref[:, cols] = _dot(hb, w_ref[:, cols]).astype(BF16)
        rows = pl.ds(c * RET_CHUNK, RET_CHUNK)
        qn = q_ref[rows, :]
        vn = v_ref[rows, :]
        scores = lax.dot_general(qn, k_ref[rows, :], (((1,), (1,)), ((), ())),
                                 preferred_element_type=F32)
        inner = _dot((scores * inner_decay).astype(BF16), vn)
        state = state_ref[...]
        cross = _dot(qn, state.astype(BF16)) * q_decay
        state_ref[...] = state * chunk_decay + lax.dot_general(
            kd_ref[rows, :], vn, (((0,), (0,)), ((), ())), preferred_element_type=F32)
        o = inner + cross
        mu = jnp.mean(o, axis=-1, keepdims=True)
        d = o - mu
        var = jnp.mean(d * d, axis=-1, keepdims=True)
        y = d * lax.rsqrt(var + GN_EPS) * gn
        gate = g_ref[rows, :].astype(F32)
        o_ref[rows, :] = (gate * jax.nn.sigmoid(gate) * y).astype(BF16)


def _retention_decays():
    c = RET_CHUNK
    log_g = jnp.log1p(-jnp.exp2(-5.0 - jnp.arange(RET_HEADS, dtype=F32)))
    idx = jnp.arange(c, dtype=F32)
    q_decay = jnp.exp((idx[None, :] + 1.0) * log_g[:, None])
    k_decay = jnp.exp((c - 1.0 - idx)[None, :] * log_g[:, None])
    diff = idx[:, None] - idx[None, :]
    inner = jnp.where(diff[None] >= 0,
                      jnp.exp(jnp.maximum(diff, 0.0)[None] * log_g[:, None, None]), 0.0)
    chunk_decay = jnp.exp(c * log_g)
    q_decay = jnp.broadcast_to(q_decay[:, :, None], (RET_HEADS, c, RET_DV))
    k_decay = jnp.broadcast_to(k_decay.T[:, :, None], (c, RET_HEADS, RET_DK)).reshape(c, RET_QK)
    chunk_decay = jnp.broadcast_to(chunk_decay[:, None, None], (RET_HEADS, 1, RET_DV))
    return inner, q_decay, k_decay, chunk_decay


def _retention_and_proj(qkd, vg, inner, q_decay, chunk_decay, gn_g, h, w, batch, seq, casts=(), *,
                        ts=4096, proj_every=1):
    t, d = h.shape
    n_out = w.shape[1]
    ns = seq // ts
    tp = t // (batch * RET_HEADS * ns)
    gn = gn_g.reshape(RET_HEADS, 1, RET_DV)

    def rows(b, hd, s):
        return b * ns + s

    def tile(b, hd, s):
        return (b * RET_HEADS + hd) * ns + s

    head_tab = lambda shape: pl.BlockSpec((None,) + shape, lambda b, hd, s: (hd, 0, 0))
    cast_in, cast_out, cast_shapes = _side_cast_specs(casts, batch * RET_HEADS * ns, tile)
    return pl.pallas_call(
        functools.partial(_retention_body, n_chunks=ts // RET_CHUNK, proj_every=proj_every),
        out_shape=[jax.ShapeDtypeStruct((t, RET_V), BF16), jax.ShapeDtypeStruct((t, n_out), BF16)]
        + cast_shapes,
        grid=(batch, RET_HEADS, ns),
        in_specs=[
            pl.BlockSpec((ts, RET_DK), lambda b, hd, s: (rows(b, hd, s), hd)),
            pl.BlockSpec((ts, RET_DK), lambda b, hd, s: (rows(b, hd, s), RET_HEADS + hd)),
            pl.BlockSpec((ts, RET_DK), lambda b, hd, s: (rows(b, hd, s), 2 * RET_HEADS + hd)),
            pl.BlockSpec((ts, RET_DV), lambda b, hd, s: (rows(b, hd, s), hd)),
            pl.BlockSpec((ts, RET_DV), lambda b, hd, s: (rows(b, hd, s), RET_HEADS + hd)),
            head_tab((RET_CHUNK, RET_CHUNK)),
            head_tab((RET_CHUNK, RET_DV)),
            head_tab((1, RET_DV)),
            head_tab((1, RET_DV)),
            pl.BlockSpec((tp, d), lambda b, hd, s: (tile(b, hd, s), 0)),
            _resident((d, n_out), lambda b, hd, s: (0, 0)),
        ] + cast_in,
        out_specs=[pl.BlockSpec((ts, RET_DV), lambda b, hd, s: (rows(b, hd, s), hd)),
                   pl.BlockSpec((tp, n_out), lambda b, hd, s: (tile(b, hd, s), 0))] + cast_out,
        scratch_shapes=[pltpu.VMEM((RET_DK, RET_DV), F32)],
        compiler_params=_params("parallel", "parallel", "arbitrary"),
        name="retention_proj",
    )(qkd, qkd, qkd, vg, vg, inner, q_decay, chunk_decay, gn, h, w, *_cast_operands(casts))


def _swa_body(sink_ref, q_ref, kvc_ref, kvp_ref, h_ref, w_ref, *refs, n_blocks):
    o_ref, proj_ref = _side_casts(refs, 2)
    n = pl.program_id(1)
    c = ATT_BLOCK
    qi = lax.broadcasted_iota(jnp.int32, (c, 2 * c), 0) + c
    kj = lax.broadcasted_iota(jnp.int32, (c, 2 * c), 1)
    rel = qi - kj
    in_window = (rel >= 0) & (rel < WINDOW)
    valid_first = in_window & (kj >= jnp.where(n > 0, 0, c))
    scale = ATT_DH ** -0.5

    def heads_of(hk):
        return [hk * ATT_GROUP + g for g in range(ATT_GROUP)]

    def band(qb, hk, offset):
        cols = slice(offset + hk * ATT_DH, offset + (hk + 1) * ATT_DH)
        prev = kvp_ref[:, cols] if qb == 0 else kvc_ref[(qb - 1) * c:qb * c, cols]
        return jnp.concatenate([prev, kvc_ref[qb * c:(qb + 1) * c, cols]], axis=0)

    def scores(qb, hk):
        q4 = jnp.concatenate([q_ref[qb * c:(qb + 1) * c, hd * ATT_DH:(hd + 1) * ATT_DH]
                              for hd in heads_of(hk)], axis=0)
        return lax.dot_general(q4 * scale, band(qb, hk, 0), (((1,), (1,)), ((), ())),
                               preferred_element_type=F32)

    def softmax(qb, hk, s4):
        valid = valid_first if qb == 0 else in_window
        masked = [jnp.where(valid, s4[g * c:(g + 1) * c], -jnp.inf) for g in range(ATT_GROUP)]
        sinks = [sink_ref[hd] for hd in heads_of(hk)]
        maxes = [jnp.maximum(jnp.max(s, axis=-1, keepdims=True), sink)
                 for s, sink in zip(masked, sinks)]
        probs = [jnp.exp(s - m).astype(BF16) for s, m in zip(masked, maxes)]
        sink_terms = [jnp.exp(sink - m) for sink, m in zip(sinks, maxes)]
        return jnp.concatenate(probs, axis=0), sink_terms

    ones = jnp.ones((2 * c, 2 * c - ATT_DH), BF16)

    def weighted_values(qb, hk, e4):
        return _dot(e4, jnp.concatenate([band(qb, hk, ATT_KV), ones], axis=1))

    def store(qb, hk, ov, sink_terms):
        groups = [ov[g * c:(g + 1) * c] for g in range(ATT_GROUP)]
        inv = [1.0 / (og[:, c:c + ATT_DH] + st) for og, st in zip(groups, sink_terms)]
        outs = [og[:, :ATT_DH] * r for og, r in zip(groups, inv)]
        for pair in range(ATT_GROUP // 2):
            col = heads_of(hk)[2 * pair] * ATT_DH
            o_ref[qb * c:(qb + 1) * c, col:col + 2 * ATT_DH] = jnp.concatenate(
                [outs[2 * pair], outs[2 * pair + 1]], axis=1).astype(BF16)

    units = [(qb, hk) for hk in range(ATT_KV_HEADS) for qb in range(n_blocks)]
    hb = h_ref[...].astype(BF16)
    chunk = proj_ref.shape[1] // len(units)

    def project(u):
        cols = slice(u * chunk, (u + 1) * chunk)
        proj_ref[:, cols] = _dot(hb, w_ref[:, cols]).astype(BF16)

    s_next = scores(*units[0])
    pending = None
    for u, unit in enumerate(units):
        s_cur = s_next
        if u + 1 < len(units):
            s_next = scores(*units[u + 1])
        project(u)
        e4, sink_terms = softmax(*unit, s_cur)
        if pending is not None:
            store(*pending)
        pending = (*unit, weighted_values(*unit, e4), sink_terms)
    store(*pending)


def _swa_and_proj(aqkv, sinks, h, w, batch, seq, casts=(), *, n_blocks=4):
    t, d = h.shape
    n_out = w.shape[1]
    tq = n_blocks * ATT_BLOCK
    nq = seq // tq
    kv_col = ATT_Q // (2 * ATT_KV)
    cast_in, cast_out, cast_shapes = _side_cast_specs(casts, batch * nq, lambda b, n: b * nq + n)
    return pl.pallas_call(
        functools.partial(_swa_body, n_blocks=n_blocks),
        out_shape=[jax.ShapeDtypeStruct((t, ATT_Q), BF16), jax.ShapeDtypeStruct((t, n_out), BF16)]
        + cast_shapes,
        grid=(batch, nq),
        in_specs=[
            pl.BlockSpec(memory_space=pltpu.SMEM),
            pl.BlockSpec((tq, ATT_Q), lambda b, n: (b * nq + n, 0)),
            pl.BlockSpec((tq, 2 * ATT_KV), lambda b, n: (b * nq + n, kv_col)),
            pl.BlockSpec((ATT_BLOCK, 2 * ATT_KV),
                         lambda b, n: ((b * nq + n) * n_blocks - jnp.minimum(n, 1), kv_col)),
            pl.BlockSpec((tq, d), lambda b, n: (b * nq + n, 0)),
            _resident((d, n_out), lambda b, n: (0, 0)),
        ] + cast_in,
        out_specs=[pl.BlockSpec((tq, ATT_Q), lambda b, n: (b * nq + n, 0)),
                   pl.BlockSpec((tq, n_out), lambda b, n: (b * nq + n, 0))] + cast_out,
        compiler_params=_params("parallel", "arbitrary"),
        name="swa_proj",
    )(sinks, aqkv, aqkv, aqkv, h, w, *_cast_operands(casts))


def _mix_body(h_ref, ret_ref, att_ref, gr_ref, ga_ref, wr_ref, wa_ref, wm_ref, g_ref, b_ref,
              o_ref, *, alpha):
    ret_branch = _dot(ret_ref[...], wr_ref[...])
    att_branch = _dot(att_ref[...], wa_ref[...])
    merged = (jax.nn.sigmoid(gr_ref[...].astype(F32)) * ret_branch
              + jax.nn.sigmoid(ga_ref[...].astype(F32)) * att_branch)
    mixed = _dot(merged.astype(BF16), wm_ref[...])
    o_ref[...] = _layer_norm(alpha * h_ref[...] + mixed, g_ref[...], b_ref[...])


def _mix_ln(h, ret, att, gates, w_ret, w_att, w_mix, g, b, *, alpha, tm=512):
    t, d = h.shape
    row = pl.BlockSpec((1, d), lambda i: (0, 0))
    return pl.pallas_call(
        functools.partial(_mix_body, alpha=alpha),
        out_shape=jax.ShapeDtypeStruct((t, d), F32),
        grid=(t // tm,),
        in_specs=[
            pl.BlockSpec((tm, d), lambda i: (i, 0)),
            pl.BlockSpec((tm, RET_V), lambda i: (i, 0)),
            pl.BlockSpec((tm, ATT_Q), lambda i: (i, 0)),
            pl.BlockSpec((tm, d), lambda i: (i, 0)),
            pl.BlockSpec((tm, d), lambda i: (i, 1)),
            _resident((RET_V, d), lambda i: (0, 0)),
            _resident((ATT_Q, d), lambda i: (0, 0)),
            _resident((d, d), lambda i: (0, 0)),
            row, row,
        ],
        out_specs=pl.BlockSpec((tm, d), lambda i: (i, 0)),
        compiler_params=_params("parallel"),
        name="mix_ln",
    )(h, ret, att, gates, gates, w_ret, w_att, w_mix, g, b)


def _ple_body(h_ref, p_ref, wg_ref, wp_ref, g_ref, b_ref, o_ref, *, alpha, n_sub):
    sub = h_ref.shape[0] // n_sub
    for r0 in range(0, h_ref.shape[0], sub):
        h = h_ref[r0:r0 + sub, :]
        gate = jax.nn.sigmoid(_dot(h.astype(BF16), wg_ref[...]))
        ple = gate * _dot(p_ref[r0:r0 + sub, :].astype(BF16), wp_ref[...])
        o_ref[r0:r0 + sub, :] = _layer_norm(alpha * h + ple, g_ref[...], b_ref[...])


def _ple_ln(h, p, w_gate, w_proj, g, b, *, alpha, tm=1024, n_sub=2):
    t, d = h.shape
    pd = p.shape[1]
    row = pl.BlockSpec((1, d), lambda i: (0, 0))
    return pl.pallas_call(
        functools.partial(_ple_body, alpha=alpha, n_sub=n_sub),
        out_shape=jax.ShapeDtypeStruct((t, d), F32),
        grid=(t // tm,),
        in_specs=[
            pl.BlockSpec((tm, d), lambda i: (i, 0)),
            pl.BlockSpec((tm, pd), lambda i: (i, 0)),
            _resident((d, d), lambda i: (0, 0)),
            _resident((pd, d), lambda i: (0, 0)),
            row, row,
        ],
        out_specs=pl.BlockSpec((tm, d), lambda i: (i, 0)),
        compiler_params=_params("parallel"),
        name="ple_ln",
    )(h, p, w_gate, w_proj, g, b)


def kernel(x, p, positions, ln_g, ln_b, w_ffn1_gu, w_ffn1_down, w_in, ret_gn_g, att_sinks,
           w_ret_out, w_att_out, w_mix_out, w_ffn2_gu, w_ffn2_down, w_ple_gate, w_ple_proj):
    batch, seq, d = x.shape
    depth = ln_g.shape[0]
    t = batch * seq
    alpha = (2.0 * depth) ** 0.25
    h = x.reshape(t, d)
    inner_decay, q_decay, k_decay, chunk_decay = _retention_decays()
    bounds = [0]
    for size in IN_SIZES:
        bounds.append(bounds[-1] + size)
    qk_cols, vg_cols = (bounds[0], bounds[2]), (bounds[2], bounds[4])
    att_cols, gate_cols = (bounds[4], bounds[7]), (bounds[7], bounds[9])
    for i in range(depth):
        g = ln_g[i][:, None, :]
        b = ln_b[i][:, None, :]
        cr, sr, ca, s1, s2, w_qk, w_att, w1_gu, w1_down = _rope_tables(
            positions.reshape(t, 1),
            ((w_in[i], *qk_cols), (w_in[i], *att_cols), w_ffn1_gu[i], w_ffn1_down[i]))

        h = _ffn_ln(h, w1_gu, w1_down, g[0], b[0], alpha=alpha)

        qkd, w_vg, w_gates = _proj(
            functools.partial(_proj_ret_qk_body, n_sub=2, seg_bounds=(vg_cols, gate_cols)),
            h, w_qk, (cr, sr), (k_decay,), (w_in[i], (vg_cols, gate_cols)),
            n_out=3 * RET_QK, tm=1024, name="proj_ret_qk")
        aqkv, = _proj(functools.partial(_proj_att_body, n_sub=2), h, w_att, (ca, s1, s2),
                      tm=1024, name="proj_att")
        att, vg, w2_gu, w2_down = _swa_and_proj(
            aqkv, att_sinks[i], h, w_vg, batch, seq, (w_ffn2_gu[i], w_ffn2_down[i]))
        ret, gates, w_ro, w_ao, w_mo, w_pg = _retention_and_proj(
            qkd, vg, inner_decay, q_decay, chunk_decay, ret_gn_g[i], h, w_gates, batch, seq,
            (w_ret_out[i], w_att_out[i], w_mix_out[i], w_ple_gate[i]))

        h = _mix_ln(h, ret, att, gates, w_ro, w_ao, w_mo, g[1], b[1], alpha=alpha)
        h = _ffn_ln(h, w2_gu, w2_down, g[2], b[2], alpha=alpha)
        h = _ple_ln(h, p[i].reshape(t, PLE_DIM), w_pg, w_ple_proj[i].astype(BF16), g[3], b[3],
                    alpha=alpha)
    return h.reshape(batch, seq, d)
```

```python
import functools

import jax
import jax.numpy as jnp
from jax import lax
from jax.experimental import pallas as pl
from jax.experimental.pallas import tpu as pltpu

F32 = jnp.float32
BF16 = jnp.bfloat16

D_MODEL = 2048
PLE_DIM = 256
D_FF = 5632
RET_HEADS = 8
RET_DK = 128
RET_DV = 256
RET_CHUNK = 256
RET_THETA = 10000.0
ATT_HEADS = 16
ATT_KV_HEADS = 4
ATT_DH = 64
ATT_GROUP = ATT_HEADS // ATT_KV_HEADS
WINDOW = 128
ATT_BLOCK = 128
ROPE_THETA = 500000.0
ROPE_DIMS = ATT_DH // 4
LN_EPS = 1e-5
GN_EPS = 1e-5

RET_QK = RET_HEADS * RET_DK
RET_V = RET_HEADS * RET_DV
ATT_Q = ATT_HEADS * ATT_DH
ATT_KV = ATT_KV_HEADS * ATT_DH
IN_SIZES = (RET_QK, RET_QK, RET_V, RET_V, ATT_Q, ATT_KV, ATT_KV, D_MODEL, D_MODEL)

LANES = 128
FFN_TILE = 512
VMEM_LIMIT_BYTES = 60 * 1024 * 1024


def _params(*semantics):
    return pltpu.CompilerParams(dimension_semantics=semantics, vmem_limit_bytes=VMEM_LIMIT_BYTES)


def _resident(shape, index_map):
    return pl.BlockSpec(shape, index_map, pipeline_mode=pl.Buffered(1))


def _side_cast_specs(items, n_steps, step_of):
    in_specs, out_specs, out_shapes = [], [], []
    for item in items:
        if isinstance(item, tuple):
            a, lo, hi = item
            r, width = a.shape[0], hi - lo
            rows = r // n_steps
            in_specs.append(pl.BlockSpec(
                (rows, width), lambda *idx, cb=lo // width: (step_of(*idx), cb)))
            out_specs.append(pl.BlockSpec((rows, width), lambda *idx: (step_of(*idx), 0)))
            out_shapes.append(jax.ShapeDtypeStruct((r, width), BF16))
            continue
        r, c = item.shape
        gc = next(k for k in (1, 2, 4, 8) if (r * k) % (16 * n_steps) == 0 and c % (k * LANES) == 0)
        gr = n_steps // gc
        spec = pl.BlockSpec(
            (r // gr, c // gc), lambda *idx, gc=gc: (step_of(*idx) // gc, step_of(*idx) % gc))
        in_specs.append(spec)
        out_specs.append(spec)
        out_shapes.append(jax.ShapeDtypeStruct(item.shape, BF16))
    return in_specs, out_specs, out_shapes


def _cast_operands(items):
    return [item[0] if isinstance(item, tuple) else item for item in items]


def _side_casts(refs, n_fixed_out):
    n = (len(refs) - n_fixed_out) // 2
    for src, dst in zip(refs[:n], refs[n + n_fixed_out:]):
        if src.shape[1] == 2 * D_FF:
            for j in range(D_FF // FFN_TILE):
                lo = j * FFN_TILE
                dst[:, 2 * lo:2 * lo + FFN_TILE] = src[:, lo:lo + FFN_TILE].astype(BF16)
                dst[:, 2 * lo + FFN_TILE:2 * lo + 2 * FFN_TILE] = (
                    src[:, D_FF + lo:D_FF + lo + FFN_TILE].astype(BF16))
        else:
            dst[...] = src[...].astype(BF16)
    return refs[n:n + n_fixed_out]


def _layer_norm(y, g, b, eps=LN_EPS):
    mu = jnp.mean(y, axis=-1, keepdims=True)
    d = y - mu
    var = jnp.mean(d * d, axis=-1, keepdims=True)
    return d * lax.rsqrt(var + eps) * g + b


def _dot(a, b):
    return jnp.dot(a, b, preferred_element_type=F32)


def _ffn_body(x_ref, wgu_ref, wd_ref, g_ref, b_ref, o_ref, xb_ref, *, alpha, n_sub):
    j = pl.program_id(1)
    sub = x_ref.shape[0] // n_sub

    @pl.when(j == 0)
    def _():
        x = x_ref[...]
        xb_ref[...] = x.astype(BF16)
        o_ref[...] = x

    for r in range(n_sub):
        rows = pl.ds(r * sub, sub)
        xb = xb_ref[rows, :]
        gate_up = _dot(xb, wgu_ref[...])
        tf = gate_up.shape[1] // 2
        gate, up = gate_up[:, :tf], gate_up[:, tf:]
        act = (gate * jax.nn.sigmoid(gate) * up).astype(BF16)
        o_ref[rows, :] += (0.5 / alpha) * _dot(act, wd_ref[...])

    @pl.when(j == pl.num_programs(1) - 1)
    def _():
        for r in range(n_sub):
            rows = pl.ds(r * sub, sub)
            o_ref[rows, :] = _layer_norm(o_ref[rows, :], g_ref[...], b_ref[...],
                                         eps=LN_EPS / alpha ** 2)


def _ffn_ln(x, w_gu, w_down, g, b, *, alpha, tm=1024, tf=FFN_TILE, n_sub=2):
    t, d = x.shape
    f = w_down.shape[0]
    nj = f // tf
    return pl.pallas_call(
        functools.partial(_ffn_body, alpha=alpha, n_sub=n_sub),
        out_shape=jax.ShapeDtypeStruct((t, d), F32),
        grid=(t // tm, nj),
        in_specs=[
            pl.BlockSpec((tm, d), lambda i, j: (i, 0)),
            pl.BlockSpec((d, 2 * tf), lambda i, j: (0, j)),
            pl.BlockSpec((tf, d), lambda i, j: (j, 0)),
            pl.BlockSpec((1, d), lambda i, j: (0, 0)),
            pl.BlockSpec((1, d), lambda i, j: (0, 0)),
        ],
        out_specs=pl.BlockSpec((tm, d), lambda i, j: (i, 0)),
        scratch_shapes=[pltpu.VMEM((tm, d), BF16)],
        compiler_params=_params("parallel", "arbitrary"),
        name="ffn_ln",
    )(x, w_gu, w_down, g, b)


def _rope_body(pos_ref, inv_ref, *refs):
    cr_ref, sr_ref, ca_ref, s1_ref, s2_ref = _side_casts(refs, 5)
    pos = pos_ref[...].astype(F32)
    ang = pos * inv_ref[...]
    c = jnp.cos(ang)
    s = jnp.sin(ang)
    half_r = RET_DK // 2
    half_a = ROPE_DIMS // 2
    lane = lax.broadcasted_iota(jnp.int32, c.shape, 1)
    low = lane < half_r
    c_swap = pltpu.roll(c, half_r, axis=1)
    s_swap = pltpu.roll(s, half_r, axis=1)
    cr_ref[...] = jnp.where(low, c, c_swap)
    sr_ref[...] = jnp.where(low, -s, s_swap)
    c_att = jnp.where(low, c_swap, c)
    s_att = jnp.where(low, s_swap, s)
    in_head = lane & (ATT_DH - 1)
    first = in_head < half_a
    second = (in_head >= half_a) & (in_head < ROPE_DIMS)
    ca_ref[...] = jnp.where(first, c_att, jnp.where(second, pltpu.roll(c_att, half_a, axis=1), 1.0))
    s1_ref[...] = jnp.where(second, pltpu.roll(s_att, half_a, axis=1), 0.0)
    s2_ref[...] = jnp.where(first, -s_att, 0.0)


def _rope_tables(pos, casts=(), tm=1024):
    t = pos.shape[0]
    n_steps = t // tm
    cast_in, cast_out, cast_shapes = _side_cast_specs(casts, n_steps, lambda i: i)
    half_r = RET_DK // 2
    half_a = ROPE_DIMS // 2
    inv_r = 1.0 / (RET_THETA ** (jnp.arange(half_r, dtype=F32) / half_r))
    inv_a = 1.0 / (ROPE_THETA ** (jnp.arange(half_a, dtype=F32) / half_a))
    inv = jnp.concatenate([inv_r, inv_a, jnp.zeros((LANES - half_r - half_a,), F32)])[None, :]
    tab = pl.BlockSpec((tm, LANES), lambda i: (i, 0))
    return pl.pallas_call(
        _rope_body,
        out_shape=[jax.ShapeDtypeStruct((t, LANES), F32)] * 5 + cast_shapes,
        grid=(n_steps,),
        in_specs=[pl.BlockSpec((tm, 1), lambda i: (i, 0)),
                  pl.BlockSpec((1, LANES), lambda i: (0, 0))] + cast_in,
        out_specs=[tab] * 5 + cast_out,
        compiler_params=_params("parallel"),
        name="rope_tables",
    )(pos, inv, *_cast_operands(casts))


def _proj_ret_qk_body(x_ref, w_ref, cr_ref, sr_ref, kdec_ref, w_in_ref, o_ref, *seg_refs,
                      n_sub, seg_bounds):
    for seg_ref, (lo, hi) in zip(seg_refs, seg_bounds):
        seg_ref[...] = w_in_ref[:, lo:hi].astype(BF16)
    sub = x_ref.shape[0] // n_sub
    k_scale = RET_DK ** -0.5
    for r0 in range(0, x_ref.shape[0], sub):
        r = _dot(x_ref[r0:r0 + sub, :].astype(BF16), w_ref[...])
        cr = cr_ref[r0:r0 + sub, :]
        sr = sr_ref[r0:r0 + sub, :]
        for s in range(2 * RET_HEADS):
            x = r[:, s * LANES:(s + 1) * LANES]
            rot = x * cr + pltpu.roll(x, RET_DK // 2, axis=1) * sr
            if s < RET_HEADS:
                o_ref[r0:r0 + sub, s * LANES:(s + 1) * LANES] = rot.astype(BF16)
                continue
            rot = rot * k_scale
            o_ref[r0:r0 + sub, s * LANES:(s + 1) * LANES] = rot.astype(BF16)
            k_decay = kdec_ref[:, (s - RET_HEADS) * LANES:(s - RET_HEADS + 1) * LANES]
            d = s + RET_HEADS
            for c0 in range(0, sub, RET_CHUNK):
                o_ref[r0 + c0:r0 + c0 + RET_CHUNK, d * LANES:(d + 1) * LANES] = (
                    rot[c0:c0 + RET_CHUNK] * k_decay).astype(BF16)


def _proj_att_body(x_ref, w_ref, ca_ref, s1_ref, s2_ref, o_ref, *, n_sub):
    sub = x_ref.shape[0] // n_sub
    half = ROPE_DIMS // 2
    n_rot = (ATT_Q + ATT_KV) // LANES
    for r0 in range(0, x_ref.shape[0], sub):
        r = _dot(x_ref[r0:r0 + sub, :].astype(BF16), w_ref[...])
        ca = ca_ref[r0:r0 + sub, :]
        s1 = s1_ref[r0:r0 + sub, :]
        s2 = s2_ref[r0:r0 + sub, :]
        for s in range(n_rot):
            x = r[:, s * LANES:(s + 1) * LANES]
            rot = (x * ca + pltpu.roll(x, half, axis=1) * s1
                   + pltpu.roll(x, LANES - half, axis=1) * s2)
            o_ref[r0:r0 + sub, s * LANES:(s + 1) * LANES] = rot.astype(BF16)
        o_ref[r0:r0 + sub, n_rot * LANES:] = r[:, n_rot * LANES:].astype(BF16)


def _proj(body, x, w, tables=(), consts=(), split=None, *, n_out=None, tm=512, name):
    t, d = x.shape
    n = w.shape[1]
    n_out = n if n_out is None else n_out
    n_steps = t // tm
    tab = pl.BlockSpec((tm, LANES), lambda i: (i, 0))
    in_specs = [pl.BlockSpec((tm, d), lambda i: (i, 0)), _resident((d, n), lambda i: (0, 0))]
    in_specs += [tab] * len(tables) + [_resident(c.shape, lambda i: (0, 0)) for c in consts]
    out_shape = [jax.ShapeDtypeStruct((t, n_out), BF16)]
    out_specs = [pl.BlockSpec((tm, n_out), lambda i: (i, 0))]
    operands = [x, w, *tables, *consts]
    if split is not None:
        src, ranges = split
        rows = src.shape[0] // n_steps
        in_specs.append(pl.BlockSpec((rows, src.shape[1]), lambda i: (i, 0)))
        operands.append(src)
        for lo, hi in ranges:
            out_shape.append(jax.ShapeDtypeStruct((src.shape[0], hi - lo), BF16))
            out_specs.append(pl.BlockSpec((rows, hi - lo), lambda i: (i, 0)))
    return pl.pallas_call(
        body,
        out_shape=out_shape,
        grid=(n_steps,),
        in_specs=in_specs,
        out_specs=out_specs,
        compiler_params=_params("parallel"),
        name=name,
    )(*operands)


def _retention_body(q_ref, k_ref, kd_ref, v_ref, g_ref, inner_ref, qd_ref, cd_ref, gn_ref,
                    h_ref, w_ref, *refs, n_chunks, proj_every):
    state_ref = refs[-1]
    o_ref, proj_ref = _side_casts(refs[:-1], 2)

    @pl.when(pl.program_id(2) == 0)
    def _():
        state_ref[...] = jnp.zeros_like(state_ref)

    inner_decay = inner_ref[...]
    q_decay = qd_ref[...]
    chunk_decay = cd_ref[...]
    gn = gn_ref[...]
    hb = h_ref[...].astype(BF16)
    proj_cols = proj_ref.shape[1] * proj_every // n_chunks
    for c in range(n_chunks):
        if c % proj_every == 0:
            cols = pl.ds((c // proj_every) * proj_cols, proj_cols)
            proj_ref[:, cols] = _dot(hb, w_ref[:, cols]).astype(BF16)
        rows = pl.ds(c * RET_CHUNK, RET_CHUNK)
        qn = q_ref[rows, :]
        vn = v_ref[rows, :]
        scores = lax.dot_general(qn, k_ref[rows, :], (((1,), (1,)), ((), ())),
                                 preferred_element_type=F32)
        inner = _dot((scores * inner_decay).astype(BF16), vn)
        state = state_ref[...]
        cross = _dot(qn, state.astype(BF16)) * q_decay
        state_ref[...] = state * chunk_decay + lax.dot_general(
            kd_ref[rows, :], vn, (((0,), (0,)), ((), ())), preferred_element_type=F32)
        o = inner + cross
        mu = jnp.mean(o, axis=-1, keepdims=True)
        d = o - mu
        var = jnp.mean(d * d, axis=-1, keepdims=True)
        y = d * lax.rsqrt(var + GN_EPS) * gn
        gate = g_ref[rows, :].astype(F32)
        o_ref[rows, :] = (gate * jax.nn.sigmoid(gate) * y).astype(BF16)


def _retention_decays():
    c = RET_CHUNK
    log_g = jnp.log1p(-jnp.exp2(-5.0 - jnp.arange(RET_HEADS, dtype=F32)))
    idx = jnp.arange(c, dtype=F32)
    q_decay = jnp.exp((idx[None, :] + 1.0) * log_g[:, None])
    k_decay = jnp.exp((c - 1.0 - idx)[None, :] * log_g[:, None])
    diff = idx[:, None] - idx[None, :]
    inner = jnp.where(diff[None] >= 0,
                      jnp.exp(jnp.maximum(diff, 0.0)[None] * log_g[:, None, None]), 0.0)
    chunk_decay = jnp.exp(c * log_g)
    q_decay = jnp.broadcast_to(q_decay[:, :, None], (RET_HEADS, c, RET_DV))
    k_decay = jnp.broadcast_to(k_decay.T[:, :, None], (c, RET_HEADS, RET_DK)).reshape(c, RET_QK)
    chunk_decay = jnp.broadcast_to(chunk_decay[:, None, None], (RET_HEADS, 1, RET_DV))
    return inner, q_decay, k_decay, chunk_decay


def _retention_and_proj(qkd, vg, inner, q_decay, chunk_decay, gn_g, h, w, batch, seq, casts=(), *,
                        ts=4096, proj_every=1):
    t, d = h.shape
    n_out = w.shape[1]
    ns = seq // ts
    tp = t // (batch * RET_HEADS * ns)
    gn = gn_g.reshape(RET_HEADS, 1, RET_DV)

    def rows(b, hd, s):
        return b * ns + s

    def tile(b, hd, s):
        return (b * RET_HEADS + hd) * ns + s

    head_tab = lambda shape: pl.BlockSpec((None,) + shape, lambda b, hd, s: (hd, 0, 0))
    cast_in, cast_out, cast_shapes = _side_cast_specs(casts, batch * RET_HEADS * ns, tile)
    return pl.pallas_call(
        functools.partial(_retention_body, n_chunks=ts // RET_CHUNK, proj_every=proj_every),
        out_shape=[jax.ShapeDtypeStruct((t, RET_V), BF16), jax.ShapeDtypeStruct((t, n_out), BF16)]
        + cast_shapes,
        grid=(batch, RET_HEADS, ns),
        in_specs=[
            pl.BlockSpec((ts, RET_DK), lambda b, hd, s: (rows(b, hd, s), hd)),
            pl.BlockSpec((ts, RET_DK), lambda b, hd, s: (rows(b, hd, s), RET_HEADS + hd)),
            pl.BlockSpec((ts, RET_DK), lambda b, hd, s: (rows(b, hd, s), 2 * RET_HEADS + hd)),
            pl.BlockSpec((ts, RET_DV), lambda b, hd, s: (rows(b, hd, s), hd)),
            pl.BlockSpec((ts, RET_DV), lambda b, hd, s: (rows(b, hd, s), RET_HEADS + hd)),
            head_tab((RET_CHUNK, RET_CHUNK)),
            head_tab((RET_CHUNK, RET_DV)),
            head_tab((1, RET_DV)),
            head_tab((1, RET_DV)),
            pl.BlockSpec((tp, d), lambda b, hd, s: (tile(b, hd, s), 0)),
            _resident((d, n_out), lambda b, hd, s: (0, 0)),
        ] + cast_in,
        out_specs=[pl.BlockSpec((ts, RET_DV), lambda b, hd, s: (rows(b, hd, s), hd)),
                   pl.BlockSpec((tp, n_out), lambda b, hd, s: (tile(b, hd, s), 0))] + cast_out,
        scratch_shapes=[pltpu.VMEM((RET_DK, RET_DV), F32)],
        compiler_params=_params("parallel", "parallel", "arbitrary"),
        name="retention_proj",
    )(qkd, qkd, qkd, vg, vg, inner, q_decay, chunk_decay, gn, h, w, *_cast_operands(casts))


def _swa_body(sink_ref, q_ref, kvc_ref, kvp_ref, h_ref, w_ref, *refs, n_blocks):
    o_ref, proj_ref = _side_casts(refs, 2)
    n = pl.program_id(1)
    c = ATT_BLOCK
    qi = lax.broadcasted_iota(jnp.int32, (c, 2 * c), 0) + c
    kj = lax.broadcasted_iota(jnp.int32, (c, 2 * c), 1)
    rel = qi - kj
    in_window = (rel >= 0) & (rel < WINDOW)
    valid_first = in_window & (kj >= jnp.where(n > 0, 0, c))
    scale = ATT_DH ** -0.5

    def heads_of(hk):
        return [hk * ATT_GROUP + g for g in range(ATT_GROUP)]

    def band(qb, hk, offset):
        cols = slice(offset + hk * ATT_DH, offset + (hk + 1) * ATT_DH)
        prev = kvp_ref[:, cols] if qb == 0 else kvc_ref[(qb - 1) * c:qb * c, cols]
        return jnp.concatenate([prev, kvc_ref[qb * c:(qb + 1) * c, cols]], axis=0)

    def scores(qb, hk):
        q4 = jnp.concatenate([q_ref[qb * c:(qb + 1) * c, hd * ATT_DH:(hd + 1) * ATT_DH]
                              for hd in heads_of(hk)], axis=0)
        return lax.dot_general(q4 * scale, band(qb, hk, 0), (((1,), (1,)), ((), ())),
                               preferred_element_type=F32)

    def softmax(qb, hk, s4):
        valid = valid_first if qb == 0 else in_window
        masked = [jnp.where(valid, s4[g * c:(g + 1) * c], -jnp.inf) for g in range(ATT_GROUP)]
        sinks = [sink_ref[hd] for hd in heads_of(hk)]
        maxes = [jnp.maximum(jnp.max(s, axis=-1, keepdims=True), sink)
                 for s, sink in zip(masked, sinks)]
        probs = [jnp.exp(s - m).astype(BF16) for s, m in zip(masked, maxes)]
        sink_terms = [jnp.exp(sink - m) for sink, m in zip(sinks, maxes)]
        return jnp.concatenate(probs, axis=0), sink_terms

    ones = jnp.ones((2 * c, 2 * c - ATT_DH), BF16)

    def weighted_values(qb, hk, e4):
        return _dot(e4, jnp.concatenate([band(qb, hk, ATT_KV), ones], axis=1))

    def store(qb, hk, ov, sink_terms):
        groups = [ov[g * c:(g + 1) * c] for g in range(ATT_GROUP)]
        inv = [1.0 / (og[:, c:c + ATT_DH] + st) for og, st in zip(groups, sink_terms)]
        outs = [og[:, :ATT_DH] * r for og, r in zip(groups, inv)]
        for pair in range(ATT_GROUP // 2):
            col = heads_of(hk)[2 * pair] * ATT_DH
            o_ref[qb * c:(qb + 1) * c, col:col + 2 * ATT_DH] = jnp.concatenate(
                [outs[2 * pair], outs[2 * pair + 1]], axis=1).astype(BF16)

    units = [(qb, hk) for hk in range(ATT_KV_HEADS) for qb in range(n_blocks)]
    hb = h_ref[...].astype(BF16)
    chunk = proj_ref.shape[1] // len(units)

    def project(u):
        cols = slice(u * chunk, (u + 1) * chunk)
        proj_ref[:, cols] = _dot(hb, w_ref[:, cols]).astype(BF16)

    s_next = scores(*units[0])
    pending = None
    for u, unit in enumerate(units):
        s_cur = s_next
        if u + 1 < len(units):
            s_next = scores(*units[u + 1])
        project(u)
        e4, sink_terms = softmax(*unit, s_cur)
        if pending is not None:
            store(*pending)
        pending = (*unit, weighted_values(*unit, e4), sink_terms)
    store(*pending)


def _swa_and_proj(aqkv, sinks, h, w, batch, seq, casts=(), *, n_blocks=4):
    t, d = h.shape
    n_out = w.shape[1]
    tq = n_blocks * ATT_BLOCK
    nq = seq // tq
    kv_col = ATT_Q // (2 * ATT_KV)
    cast_in, cast_out, cast_shapes = _side_cast_specs(casts, batch * nq, lambda b, n: b * nq + n)
    return pl.pallas_call(
        functools.partial(_swa_body, n_blocks=n_blocks),
        out_shape=[jax.ShapeDtypeStruct((t, ATT_Q), BF16), jax.ShapeDtypeStruct((t, n_out), BF16)]
        + cast_shapes,
        grid=(batch, nq),
        in_specs=[
            pl.BlockSpec(memory_space=pltpu.SMEM),
            pl.BlockSpec((tq, ATT_Q), lambda b, n: (b * nq + n, 0)),
            pl.BlockSpec((tq, 2 * ATT_KV), lambda b, n: (b * nq + n, kv_col)),
            pl.BlockSpec((ATT_BLOCK, 2 * ATT_KV),
                         lambda b, n: ((b * nq + n) * n_blocks - jnp.minimum(n, 1), kv_col)),
            pl.BlockSpec((tq, d), lambda b, n: (b * nq + n, 0)),
            _resident((d, n_out), lambda b, n: (0, 0)),
        ] + cast_in,
        out_specs=[pl.BlockSpec((tq, ATT_Q), lambda b, n: (b * nq + n, 0)),
                   pl.BlockSpec((tq, n_out), lambda b, n: (b * nq + n, 0))] + cast_out,
        compiler_params=_params("parallel", "arbitrary"),
        name="swa_proj",
    )(sinks, aqkv, aqkv, aqkv, h, w, *_cast_operands(casts))


def _mix_body(h_ref, ret_ref, att_ref, gr_ref, ga_ref, wr_ref, wa_ref, wm_ref, g_ref, b_ref,
              o_ref, *, alpha):
    ret_branch = _dot(ret_ref[...], wr_ref[...])
    att_branch = _dot(att_ref[...], wa_ref[...])
    merged = (jax.nn.sigmoid(gr_ref[...].astype(F32)) * ret_branch
              + jax.nn.sigmoid(ga_ref[...].astype(F32)) * att_branch)
    mixed = _dot(merged.astype(BF16), wm_ref[...])
    o_ref[...] = _layer_norm(alpha * h_ref[...] + mixed, g_ref[...], b_ref[...])


def _mix_ln(h, ret, att, gates, w_ret, w_att, w_mix, g, b, *, alpha, tm=512):
    t, d = h.shape
    row = pl.BlockSpec((1, d), lambda i: (0, 0))
    return pl.pallas_call(
        functools.partial(_mix_body, alpha=alpha),
        out_shape=jax.ShapeDtypeStruct((t, d), F32),
        grid=(t // tm,),
        in_specs=[
            pl.BlockSpec((tm, d), lambda i: (i, 0)),
            pl.BlockSpec((tm, RET_V), lambda i: (i, 0)),
            pl.BlockSpec((tm, ATT_Q), lambda i: (i, 0)),
            pl.BlockSpec((tm, d), lambda i: (i, 0)),
            pl.BlockSpec((tm, d), lambda i: (i, 1)),
            _resident((RET_V, d), lambda i: (0, 0)),
            _resident((ATT_Q, d), lambda i: (0, 0)),
            _resident((d, d), lambda i: (0, 0)),
            row, row,
        ],
        out_specs=pl.BlockSpec((tm, d), lambda i: (i, 0)),
        compiler_params=_params("parallel"),
        name="mix_ln",
    )(h, ret, att, gates, gates, w_ret, w_att, w_mix, g, b)


def _ple_body(h_ref, p_ref, wg_ref, wp_ref, g_ref, b_ref, o_ref, *, alpha, n_sub):
    sub = h_ref.shape[0] // n_sub
    for r0 in range(0, h_ref.shape[0], sub):
        h = h_ref[r0:r0 + sub, :]
        gate = jax.nn.sigmoid(_dot(h.astype(BF16), wg_ref[...]))
        ple = gate * _dot(p_ref[r0:r0 + sub, :].astype(BF16), wp_ref[...])
        o_ref[r0:r0 + sub, :] = _layer_norm(alpha * h + ple, g_ref[...], b_ref[...])


def _ple_ln(h, p, w_gate, w_proj, g, b, *, alpha, tm=1024, n_sub=2):
    t, d = h.shape
    pd = p.shape[1]
    row = pl.BlockSpec((1, d), lambda i: (0, 0))
    return pl.pallas_call(
        functools.partial(_ple_body, alpha=alpha, n_sub=n_sub),
        out_shape=jax.ShapeDtypeStruct((t, d), F32),
        grid=(t // tm,),
        in_specs=[
            pl.BlockSpec((tm, d), lambda i: (i, 0)),
            pl.BlockSpec((tm, pd), lambda i: (i, 0)),
            _resident((d, d), lambda i: (0, 0)),
            _resident((pd, d), lambda i: (0, 0)),
            row, row,
        ],
        out_specs=pl.BlockSpec((tm, d), lambda i: (i, 0)),
        compiler_params=_params("parallel"),
        name="ple_ln",
    )(h, p, w_gate, w_proj, g, b)


def kernel(x, p, positions, ln_g, ln_b, w_ffn1_gu, w_ffn1_down, w_in, ret_gn_g, att_sinks,
           w_ret_out, w_att_out, w_mix_out, w_ffn2_gu, w_ffn2_down, w_ple_gate, w_ple_proj):
    batch, seq, d = x.shape
    depth = ln_g.shape[0]
    t = batch * seq
    alpha = (2.0 * depth) ** 0.25
    h = x.reshape(t, d)
    inner_decay, q_decay, k_decay, chunk_decay = _retention_decays()
    bounds = [0]
    for size in IN_SIZES:
        bounds.append(bounds[-1] + size)
    qk_cols, vg_cols = (bounds[0], bounds[2]), (bounds[2], bounds[4])
    att_cols, gate_cols = (bounds[4], bounds[7]), (bounds[7], bounds[9])
    for i in range(depth):
        g = ln_g[i][:, None, :]
        b = ln_b[i][:, None, :]
        cr, sr, ca, s1, s2, w_qk, w_att, w1_gu, w1_down = _rope_tables(
            positions.reshape(t, 1),
            ((w_in[i], *qk_cols), (w_in[i], *att_cols), w_ffn1_gu[i], w_ffn1_down[i]))

        h = _ffn_ln(h, w1_gu, w1_down, g[0], b[0], alpha=alpha)

        qkd, w_vg, w_gates = _proj(
            functools.partial(_proj_ret_qk_body, n_sub=2, seg_bounds=(vg_cols, gate_cols)),
            h, w_qk, (cr, sr), (k_decay,), (w_in[i], (vg_cols, gate_cols)),
            n_out=3 * RET_QK, tm=1024, name="proj_ret_qk")
        aqkv, = _proj(functools.partial(_proj_att_body, n_sub=2), h, w_att, (ca, s1, s2),
                      tm=1024, name="proj_att")
        att, vg, w2_gu, w2_down = _swa_and_proj(
            aqkv, att_sinks[i], h, w_vg, batch, seq, (w_ffn2_gu[i], w_ffn2_down[i]))
        ret, gates, w_ro, w_ao, w_mo, w_pg = _retention_and_proj(
            qkd, vg, inner_decay, q_decay, chunk_decay, ret_gn_g[i], h, w_gates, batch, seq,
            (w_ret_out[i], w_att_out[i], w_mix_out[i], w_ple_gate[i]))

        h = _mix_ln(h, ret, att, gates, w_ro, w_ao, w_mo, g[1], b[1], alpha=alpha)
        h = _ffn_ln(h, w2_gu, w2_down, g[2], b[2], alpha=alpha)
        h = _ple_ln(h, p[i].reshape(t, PLE_DIM), w_pg, w_ple_proj[i].astype(BF16), g[3], b[3],
                    alpha=alpha)
    return h.reshape(batch, seq, d)
```

```python
import functools

import jax
import jax.numpy as jnp
from jax import lax
from jax.experimental import pallas as pl
from jax.experimental.pallas import tpu as pltpu

F32 = jnp.float32
BF16 = jnp.bfloat16

D_MODEL = 2048
PLE_DIM = 256
D_FF = 5632
RET_HEADS = 8
RET_DK = 128
RET_DV = 256
RET_CHUNK = 256
RET_THETA = 10000.0
ATT_HEADS = 16
ATT_KV_HEADS = 4
ATT_DH = 64
ATT_GROUP = ATT_HEADS // ATT_KV_HEADS
WINDOW = 128
ATT_BLOCK = 128
ROPE_THETA = 500000.0
ROPE_DIMS = ATT_DH // 4
LN_EPS = 1e-5
GN_EPS = 1e-5

RET_QK = RET_HEADS * RET_DK
RET_V = RET_HEADS * RET_DV
ATT_Q = ATT_HEADS * ATT_DH
ATT_KV = ATT_KV_HEADS * ATT_DH
IN_SIZES = (RET_QK, RET_QK, RET_V, RET_V, ATT_Q, ATT_KV, ATT_KV, D_MODEL, D_MODEL)

LANES = 128
VMEM_LIMIT_BYTES = 60 * 1024 * 1024


def _params(*semantics):
    return pltpu.CompilerParams(dimension_semantics=semantics, vmem_limit_bytes=VMEM_LIMIT_BYTES)


def _resident(shape, index_map):
    return pl.BlockSpec(shape, index_map, pipeline_mode=pl.Buffered(1))


def _side_cast_specs(items, n_steps, step_of):
    in_specs, out_specs, out_shapes = [], [], []
    for item in items:
        if isinstance(item, tuple):
            a, lo, hi = item
            r, width = a.shape[0], hi - lo
            rows = r // n_steps
            in_specs.append(pl.BlockSpec(
                (rows, width), lambda *idx, cb=lo // width: (step_of(*idx), cb)))
            out_specs.append(pl.BlockSpec((rows, width), lambda *idx: (step_of(*idx), 0)))
            out_shapes.append(jax.ShapeDtypeStruct((r, width), BF16))
            continue
        r, c = item.shape
        gc = next(k for k in (1, 2, 4, 8) if (r * k) % (16 * n_steps) == 0 and c % (k * LANES) == 0)
        gr = n_steps // gc
        spec = pl.BlockSpec(
            (r // gr, c // gc), lambda *idx, gc=gc: (step_of(*idx) // gc, step_of(*idx) % gc))
        in_specs.append(spec)
        out_specs.append(spec)
        out_shapes.append(jax.ShapeDtypeStruct(item.shape, BF16))
    return in_specs, out_specs, out_shapes


def _cast_operands(items):
    return [item[0] if isinstance(item, tuple) else item for item in items]


def _side_casts(refs, n_fixed_out):
    n = (len(refs) - n_fixed_out) // 2
    for src, dst in zip(refs[:n], refs[n + n_fixed_out:]):
        dst[...] = src[...].astype(BF16)
    return refs[n:n + n_fixed_out]


def _layer_norm(y, g, b, eps=LN_EPS):
    mu = jnp.mean(y, axis=-1, keepdims=True)
    d = y - mu
    var = jnp.mean(d * d, axis=-1, keepdims=True)
    return d * lax.rsqrt(var + eps) * g + b


def _dot(a, b):
    return jnp.dot(a, b, preferred_element_type=F32)


def _ffn_body(x_ref, wg_ref, wu_ref, wd_ref, g_ref, b_ref, o_ref, xb_ref, *, alpha, n_sub):
    j = pl.program_id(1)
    sub = x_ref.shape[0] // n_sub

    @pl.when(j == 0)
    def _():
        x = x_ref[...]
        xb_ref[...] = x.astype(BF16)
        o_ref[...] = x

    for r in range(n_sub):
        rows = pl.ds(r * sub, sub)
        xb = xb_ref[rows, :]
        gate = _dot(xb, wg_ref[...])
        up = _dot(xb, wu_ref[...])
        act = (gate * jax.nn.sigmoid(gate) * up).astype(BF16)
        o_ref[rows, :] += (0.5 / alpha) * _dot(act, wd_ref[...])

    @pl.when(j == pl.num_programs(1) - 1)
    def _():
        for r in range(n_sub):
            rows = pl.ds(r * sub, sub)
            o_ref[rows, :] = _layer_norm(o_ref[rows, :], g_ref[...], b_ref[...],
                                         eps=LN_EPS / alpha ** 2)


def _ffn_ln(x, w_gu, w_down, g, b, *, alpha, tm=1024, tf=512, n_sub=2):
    t, d = x.shape
    f = w_down.shape[0]
    nj = f // tf
    return pl.pallas_call(
        functools.partial(_ffn_body, alpha=alpha, n_sub=n_sub),
        out_shape=jax.ShapeDtypeStruct((t, d), F32),
        grid=(t // tm, nj),
        in_specs=[
            pl.BlockSpec((tm, d), lambda i, j: (i, 0)),
            pl.BlockSpec((d, tf), lambda i, j: (0, j)),
            pl.BlockSpec((d, tf), lambda i, j: (0, j + nj)),
            pl.BlockSpec((tf, d), lambda i, j: (j, 0)),
            pl.BlockSpec((1, d), lambda i, j: (0, 0)),
            pl.BlockSpec((1, d), lambda i, j: (0, 0)),
        ],
        out_specs=pl.BlockSpec((tm, d), lambda i, j: (i, 0)),
        scratch_shapes=[pltpu.VMEM((tm, d), BF16)],
        compiler_params=_params("parallel", "arbitrary"),
        name="ffn_ln",
    )(x, w_gu, w_gu, w_down, g, b)


def _rope_body(pos_ref, inv_ref, *refs):
    cr_ref, sr_ref, ca_ref, s1_ref, s2_ref = _side_casts(refs, 5)
    pos = pos_ref[...].astype(F32)
    ang = pos * inv_ref[...]
    c = jnp.cos(ang)
    s = jnp.sin(ang)
    half_r = RET_DK // 2
    half_a = ROPE_DIMS // 2
    lane = lax.broadcasted_iota(jnp.int32, c.shape, 1)
    low = lane < half_r
    c_swap = pltpu.roll(c, half_r, axis=1)
    s_swap = pltpu.roll(s, half_r, axis=1)
    cr_ref[...] = jnp.where(low, c, c_swap)
    sr_ref[...] = jnp.where(low, -s, s_swap)
    c_att = jnp.where(low, c_swap, c)
    s_att = jnp.where(low, s_swap, s)
    in_head = lane & (ATT_DH - 1)
    first = in_head < half_a
    second = (in_head >= half_a) & (in_head < ROPE_DIMS)
    ca_ref[...] = jnp.where(first, c_att, jnp.where(second, pltpu.roll(c_att, half_a, axis=1), 1.0))
    s1_ref[...] = jnp.where(second, pltpu.roll(s_att, half_a, axis=1), 0.0)
    s2_ref[...] = jnp.where(first, -s_att, 0.0)


def _rope_tables(pos, casts=(), tm=1024):
    t = pos.shape[0]
    n_steps = t // tm
    cast_in, cast_out, cast_shapes = _side_cast_specs(casts, n_steps, lambda i: i)
    half_r = RET_DK // 2
    half_a = ROPE_DIMS // 2
    inv_r = 1.0 / (RET_THETA ** (jnp.arange(half_r, dtype=F32) / half_r))
    inv_a = 1.0 / (ROPE_THETA ** (jnp.arange(half_a, dtype=F32) / half_a))
    inv = jnp.concatenate([inv_r, inv_a, jnp.zeros((LANES - half_r - half_a,), F32)])[None, :]
    tab = pl.BlockSpec((tm, LANES), lambda i: (i, 0))
    return pl.pallas_call(
        _rope_body,
        out_shape=[jax.ShapeDtypeStruct((t, LANES), F32)] * 5 + cast_shapes,
        grid=(n_steps,),
        in_specs=[pl.BlockSpec((tm, 1), lambda i: (i, 0)),
                  pl.BlockSpec((1, LANES), lambda i: (0, 0))] + cast_in,
        out_specs=[tab] * 5 + cast_out,
        compiler_params=_params("parallel"),
        name="rope_tables",
    )(pos, inv, *_cast_operands(casts))


def _proj_ret_qk_body(x_ref, w_ref, cr_ref, sr_ref, kdec_ref, w_in_ref, o_ref, *seg_refs,
                      n_sub, seg_bounds):
    for seg_ref, (lo, hi) in zip(seg_refs, seg_bounds):
        seg_ref[...] = w_in_ref[:, lo:hi].astype(BF16)
    sub = x_ref.shape[0] // n_sub
    k_scale = RET_DK ** -0.5
    for r0 in range(0, x_ref.shape[0], sub):
        r = _dot(x_ref[r0:r0 + sub, :].astype(BF16), w_ref[...])
        cr = cr_ref[r0:r0 + sub, :]
        sr = sr_ref[r0:r0 + sub, :]
        for s in range(2 * RET_HEADS):
            x = r[:, s * LANES:(s + 1) * LANES]
            rot = x * cr + pltpu.roll(x, RET_DK // 2, axis=1) * sr
            if s < RET_HEADS:
                o_ref[r0:r0 + sub, s * LANES:(s + 1) * LANES] = rot.astype(BF16)
                continue
            rot = rot * k_scale
            o_ref[r0:r0 + sub, s * LANES:(s + 1) * LANES] = rot.astype(BF16)
            k_decay = kdec_ref[:, (s - RET_HEADS) * LANES:(s - RET_HEADS + 1) * LANES]
            d = s + RET_HEADS
            for c0 in range(0, sub, RET_CHUNK):
                o_ref[r0 + c0:r0 + c0 + RET_CHUNK, d * LANES:(d + 1) * LANES] = (
                    rot[c0:c0 + RET_CHUNK] * k_decay).astype(BF16)


def _proj_att_body(x_ref, w_ref, ca_ref, s1_ref, s2_ref, o_ref, *, n_sub):
    sub = x_ref.shape[0] // n_sub
    half = ROPE_DIMS // 2
    n_rot = (ATT_Q + ATT_KV) // LANES
    for r0 in range(0, x_ref.shape[0], sub):
        r = _dot(x_ref[r0:r0 + sub, :].astype(BF16), w_ref[...])
        ca = ca_ref[r0:r0 + sub, :]
        s1 = s1_ref[r0:r0 + sub, :]
        s2 = s2_ref[r0:r0 + sub, :]
        for s in range(n_rot):
            x = r[:, s * LANES:(s + 1) * LANES]
            rot = (x * ca + pltpu.roll(x, half, axis=1) * s1
                   + pltpu.roll(x, LANES - half, axis=1) * s2)
            o_ref[r0:r0 + sub, s * LANES:(s + 1) * LANES] = rot.astype(BF16)
        o_ref[r0:r0 + sub, n_rot * LANES:] = r[:, n_rot * LANES:].astype(BF16)


def _proj(body, x, w, tables=(), consts=(), split=None, *, n_out=None, tm=512, name):
    t, d = x.shape
    n = w.shape[1]
    n_out = n if n_out is None else n_out
    n_steps = t // tm
    tab = pl.BlockSpec((tm, LANES), lambda i: (i, 0))
    in_specs = [pl.BlockSpec((tm, d), lambda i: (i, 0)), _resident((d, n), lambda i: (0, 0))]
    in_specs += [tab] * len(tables) + [_resident(c.shape, lambda i: (0, 0)) for c in consts]
    out_shape = [jax.ShapeDtypeStruct((t, n_out), BF16)]
    out_specs = [pl.BlockSpec((tm, n_out), lambda i: (i, 0))]
    operands = [x, w, *tables, *consts]
    if split is not None:
        src, ranges = split
        rows = src.shape[0] // n_steps
        in_specs.append(pl.BlockSpec((rows, src.shape[1]), lambda i: (i, 0)))
        operands.append(src)
        for lo, hi in ranges:
            out_shape.append(jax.ShapeDtypeStruct((src.shape[0], hi - lo), BF16))
            out_specs.append(pl.BlockSpec((rows, hi - lo), lambda i: (i, 0)))
    return pl.pallas_call(
        body,
        out_shape=out_shape,
        grid=(n_steps,),
        in_specs=in_specs,
        out_specs=out_specs,
        compiler_params=_params("parallel"),
        name=name,
    )(*operands)


def _retention_body(q_ref, k_ref, kd_ref, v_ref, g_ref, inner_ref, qd_ref, cd_ref, gn_ref,
                    h_ref, w_ref, *refs, n_chunks, proj_every):
    state_ref = refs[-1]
    o_ref, proj_ref = _side_casts(refs[:-1], 2)

    @pl.when(pl.program_id(2) == 0)
    def _():
        state_ref[...] = jnp.zeros_like(state_ref)

    inner_decay = inner_ref[...]
    q_decay = qd_ref[...]
    chunk_decay = cd_ref[...]
    gn = gn_ref[...]
    hb = h_ref[...].astype(BF16)
    proj_cols = proj_ref.shape[1] * proj_every // n_chunks
    for c in range(n_chunks):
        if c % proj_every == 0:
            cols = pl.ds((c // proj_every) * proj_cols, proj_cols)
            proj_ref[:, cols] = _dot(hb, w_ref[:, cols]).astype(BF16)
        rows = pl.ds(c * RET_CHUNK, RET_CHUNK)
        qn = q_ref[rows, :]
        vn = v_ref[rows, :]
        scores = lax.dot_general(qn, k_ref[rows, :], (((1,), (1,)), ((), ())),
                                 preferred_element_type=F32)
        inner = _dot((scores * inner_decay).astype(BF16), vn)
        state = state_ref[...]
        cross = _dot(qn, state.astype(BF16)) * q_decay
        state_ref[...] = state * chunk_decay + lax.dot_general(
            kd_ref[rows, :], vn, (((0,), (0,)), ((), ())), preferred_element_type=F32)
        o = inner + cross
        mu = jnp.mean(o, axis=-1, keepdims=True)
        d = o - mu
        var = jnp.mean(d * d, axis=-1, keepdims=True)
        y = d * lax.rsqrt(var + GN_EPS) * gn
        gate = g_ref[rows, :].astype(F32)
        o_ref[rows, :] = (gate * jax.nn.sigmoid(gate) * y).astype(BF16)


def _retention_decays():
    c = RET_CHUNK
    log_g = jnp.log1p(-jnp.exp2(-5.0 - jnp.arange(RET_HEADS, dtype=F32)))
    idx = jnp.arange(c, dtype=F32)
    q_decay = jnp.exp((idx[None, :] + 1.0) * log_g[:, None])
    k_decay = jnp.exp((c - 1.0 - idx)[None, :] * log_g[:, None])
    diff = idx[:, None] - idx[None, :]
    inner = jnp.where(diff[None] >= 0,
                      jnp.exp(jnp.maximum(diff, 0.0)[None] * log_g[:, None, None]), 0.0)
    chunk_decay = jnp.exp(c * log_g)
    q_decay = jnp.broadcast_to(q_decay[:, :, None], (RET_HEADS, c, RET_DV))
    k_decay = jnp.broadcast_to(k_decay.T[:, :, None], (c, RET_HEADS, RET_DK)).reshape(c, RET_QK)
    chunk_decay = jnp.broadcast_to(chunk_decay[:, None, None], (RET_HEADS, 1, RET_DV))
    return inner, q_decay, k_decay, chunk_decay


def _retention_and_proj(qkd, vg, inner, q_decay, chunk_decay, gn_g, h, w, batch, seq, casts=(), *,
                        ts=4096, proj_every=1):
    t, d = h.shape
    n_out = w.shape[1]
    ns = seq // ts
    tp = t // (batch * RET_HEADS * ns)
    gn = gn_g.reshape(RET_HEADS, 1, RET_DV)

    def rows(b, hd, s):
        return b * ns + s

    def tile(b, hd, s):
        return (b * RET_HEADS + hd) * ns + s

    head_tab = lambda shape: pl.BlockSpec((None,) + shape, lambda b, hd, s: (hd, 0, 0))
    cast_in, cast_out, cast_shapes = _side_cast_specs(casts, batch * RET_HEADS * ns, tile)
    return pl.pallas_call(
        functools.partial(_retention_body, n_chunks=ts // RET_CHUNK, proj_every=proj_every),
        out_shape=[jax.ShapeDtypeStruct((t, RET_V), BF16), jax.ShapeDtypeStruct((t, n_out), BF16)]
        + cast_shapes,
        grid=(batch, RET_HEADS, ns),
        in_specs=[
            pl.BlockSpec((ts, RET_DK), lambda b, hd, s: (rows(b, hd, s), hd)),
            pl.BlockSpec((ts, RET_DK), lambda b, hd, s: (rows(b, hd, s), RET_HEADS + hd)),
            pl.BlockSpec((ts, RET_DK), lambda b, hd, s: (rows(b, hd, s), 2 * RET_HEADS + hd)),
            pl.BlockSpec((ts, RET_DV), lambda b, hd, s: (rows(b, hd, s), hd)),
            pl.BlockSpec((ts, RET_DV), lambda b, hd, s: (rows(b, hd, s), RET_HEADS + hd)),
            head_tab((RET_CHUNK, RET_CHUNK)),
            head_tab((RET_CHUNK, RET_DV)),
            head_tab((1, RET_DV)),
            head_tab((1, RET_DV)),
            pl.BlockSpec((tp, d), lambda b, hd, s: (tile(b, hd, s), 0)),
            _resident((d, n_out), lambda b, hd, s: (0, 0)),
        ] + cast_in,
        out_specs=[pl.BlockSpec((ts, RET_DV), lambda b, hd, s: (rows(b, hd, s), hd)),
                   pl.BlockSpec((tp, n_out), lambda b, hd, s: (tile(b, hd, s), 0))] + cast_out,
        scratch_shapes=[pltpu.VMEM((RET_DK, RET_DV), F32)],
        compiler_params=_params("parallel", "parallel", "arbitrary"),
        name="retention_proj",
    )(qkd, qkd, qkd, vg, vg, inner, q_decay, chunk_decay, gn, h, w, *_cast_operands(casts))


def _swa_body(sink_ref, q_ref, kvc_ref, kvp_ref, h_ref, w_ref, *refs, n_blocks):
    o_ref, proj_ref = _side_casts(refs, 2)
    n = pl.program_id(1)
    c = ATT_BLOCK
    qi = lax.broadcasted_iota(jnp.int32, (c, 2 * c), 0) + c
    kj = lax.broadcasted_iota(jnp.int32, (c, 2 * c), 1)
    rel = qi - kj
    in_window = (rel >= 0) & (rel < WINDOW)
    valid_first = in_window & (kj >= jnp.where(n > 0, 0, c))
    scale = ATT_DH ** -0.5

    def heads_of(hk):
        return [hk * ATT_GROUP + g for g in range(ATT_GROUP)]

    def band(qb, hk, offset):
        cols = slice(offset + hk * ATT_DH, offset + (hk + 1) * ATT_DH)
        prev = kvp_ref[:, cols] if qb == 0 else kvc_ref[(qb - 1) * c:qb * c, cols]
        return jnp.concatenate([prev, kvc_ref[qb * c:(qb + 1) * c, cols]], axis=0)

    def scores(qb, hk):
        q4 = jnp.concatenate([q_ref[qb * c:(qb + 1) * c, hd * ATT_DH:(hd + 1) * ATT_DH]
                              for hd in heads_of(hk)], axis=0)
        return lax.dot_general(q4 * scale, band(qb, hk, 0), (((1,), (1,)), ((), ())),
                               preferred_element_type=F32)

    def softmax(qb, hk, s4):
        valid = valid_first if qb == 0 else in_window
        masked = [jnp.where(valid, s4[g * c:(g + 1) * c], -jnp.inf) for g in range(ATT_GROUP)]
        sinks = [sink_ref[hd] for hd in heads_of(hk)]
        maxes = [jnp.maximum(jnp.max(s, axis=-1, keepdims=True), sink)
                 for s, sink in zip(masked, sinks)]
        probs = [jnp.exp(s - m).astype(BF16) for s, m in zip(masked, maxes)]
        sink_terms = [jnp.exp(sink - m) for sink, m in zip(sinks, maxes)]
        return jnp.concatenate(probs, axis=0), sink_terms

    ones = jnp.ones((2 * c, 2 * c - ATT_DH), BF16)

    def weighted_values(qb, hk, e4):
        return _dot(e4, jnp.concatenate([band(qb, hk, ATT_KV), ones], axis=1))

    def store(qb, hk, ov, sink_terms):
        groups = [ov[g * c:(g + 1) * c] for g in range(ATT_GROUP)]
        inv = [1.0 / (og[:, c:c + ATT_DH] + st) for og, st in zip(groups, sink_terms)]
        outs = [og[:, :ATT_DH] * r for og, r in zip(groups, inv)]
        for pair in range(ATT_GROUP // 2):
            col = heads_of(hk)[2 * pair] * ATT_DH
            o_ref[qb * c:(qb + 1) * c, col:col + 2 * ATT_DH] = jnp.concatenate(
                [outs[2 * pair], outs[2 * pair + 1]], axis=1).astype(BF16)

    units = [(qb, hk) for hk in range(ATT_KV_HEADS) for qb in range(n_blocks)]
    hb = h_ref[...].astype(BF16)
    chunk = proj_ref.shape[1] // len(units)

    def project(u):
        cols = slice(u * chunk, (u + 1) * chunk)
        proj_ref[:, cols] = _dot(hb, w_ref[:, cols]).astype(BF16)

    s_next = scores(*units[0])
    pending = None
    for u, unit in enumerate(units):
        s_cur = s_next
        if u + 1 < len(units):
            s_next = scores(*units[u + 1])
        project(u)
        e4, sink_terms = softmax(*unit, s_cur)
        if pending is not None:
            store(*pending)
        pending = (*unit, weighted_values(*unit, e4), sink_terms)
    store(*pending)


def _swa_and_proj(aqkv, sinks, h, w, batch, seq, casts=(), *, n_blocks=4):
    t, d = h.shape
    n_out = w.shape[1]
    tq = n_blocks * ATT_BLOCK
    nq = seq // tq
    kv_col = ATT_Q // (2 * ATT_KV)
    cast_in, cast_out, cast_shapes = _side_cast_specs(casts, batch * nq, lambda b, n: b * nq + n)
    return pl.pallas_call(
        functools.partial(_swa_body, n_blocks=n_blocks),
        out_shape=[jax.ShapeDtypeStruct((t, ATT_Q), BF16), jax.ShapeDtypeStruct((t, n_out), BF16)]
        + cast_shapes,
        grid=(batch, nq),
        in_specs=[
            pl.BlockSpec(memory_space=pltpu.SMEM),
            pl.BlockSpec((tq, ATT_Q), lambda b, n: (b * nq + n, 0)),
            pl.BlockSpec((tq, 2 * ATT_KV), lambda b, n: (b * nq + n, kv_col)),
            pl.BlockSpec((ATT_BLOCK, 2 * ATT_KV),
                         lambda b, n: ((b * nq + n) * n_blocks - jnp.minimum(n, 1), kv_col)),
            pl.BlockSpec((tq, d), lambda b, n: (b * nq + n, 0)),
            _resident((d, n_out), lambda b, n: (0, 0)),
        ] + cast_in,
        out_specs=[pl.BlockSpec((tq, ATT_Q), lambda b, n: (b * nq + n, 0)),
                   pl.BlockSpec((tq, n_out), lambda b, n: (b * nq + n, 0))] + cast_out,
        compiler_params=_params("parallel", "arbitrary"),
        name="swa_proj",
    )(sinks, aqkv, aqkv, aqkv, h, w, *_cast_operands(casts))


def _mix_body(h_ref, ret_ref, att_ref, gr_ref, ga_ref, wr_ref, wa_ref, wm_ref, g_ref, b_ref,
              o_ref, *, alpha):
    ret_branch = _dot(ret_ref[...], wr_ref[...])
    att_branch = _dot(att_ref[...], wa_ref[...])
    merged = (jax.nn.sigmoid(gr_ref[...].astype(F32)) * ret_branch
              + jax.nn.sigmoid(ga_ref[...].astype(F32)) * att_branch) * (1.0 / alpha)
    mixed = _dot(merged.astype(BF16), wm_ref[...])
    o_ref[...] = _layer_norm(h_ref[...] + mixed, g_ref[...], b_ref[...], eps=LN_EPS / alpha ** 2)


def _mix_ln(h, ret, att, gates, w_ret, w_att, w_mix, g, b, *, alpha, tm=512):
    t, d = h.shape
    row = pl.BlockSpec((1, d), lambda i: (0, 0))
    return pl.pallas_call(
        functools.partial(_mix_body, alpha=alpha),
        out_shape=jax.ShapeDtypeStruct((t, d), F32),
        grid=(t // tm,),
        in_specs=[
            pl.BlockSpec((tm, d), lambda i: (i, 0)),
            pl.BlockSpec((tm, RET_V), lambda i: (i, 0)),
            pl.BlockSpec((tm, ATT_Q), lambda i: (i, 0)),
            pl.BlockSpec((tm, d), lambda i: (i, 0)),
            pl.BlockSpec((tm, d), lambda i: (i, 1)),
            _resident((RET_V, d), lambda i: (0, 0)),
            _resident((ATT_Q, d), lambda i: (0, 0)),
            _resident((d, d), lambda i: (0, 0)),
            row, row,
        ],
        out_specs=pl.BlockSpec((tm, d), lambda i: (i, 0)),
        compiler_params=_params("parallel"),
        name="mix_ln",
    )(h, ret, att, gates, gates, w_ret, w_att, w_mix, g, b)


def _ple_body(h_ref, p_ref, wg_ref, wp_ref, g_ref, b_ref, o_ref, *, alpha, n_sub):
    sub = h_ref.shape[0] // n_sub
    for r0 in range(0, h_ref.shape[0], sub):
        h = h_ref[r0:r0 + sub, :]
        gate = jax.nn.sigmoid(_dot(h.astype(BF16), wg_ref[...]))
        ple = gate * _dot(p_ref[r0:r0 + sub, :].astype(BF16), wp_ref[...])
        o_ref[r0:r0 + sub, :] = _layer_norm(alpha * h + ple, g_ref[...], b_ref[...])


def _ple_ln(h, p, w_gate, w_proj, g, b, *, alpha, tm=1024, n_sub=2):
    t, d = h.shape
    pd = p.shape[1]
    row = pl.BlockSpec((1, d), lambda i: (0, 0))
    return pl.pallas_call(
        functools.partial(_ple_body, alpha=alpha, n_sub=n_sub),
        out_shape=jax.ShapeDtypeStruct((t, d), F32),
        grid=(t // tm,),
        in_specs=[
            pl.BlockSpec((tm, d), lambda i: (i, 0)),
            pl.BlockSpec((tm, pd), lambda i: (i, 0)),
            _resident((d, d), lambda i: (0, 0)),
            _resident((pd, d), lambda i: (0, 0)),
            row, row,
        ],
        out_specs=pl.BlockSpec((tm, d), lambda i: (i, 0)),
        compiler_params=_params("parallel"),
        name="ple_ln",
    )(h, p, w_gate, w_proj, g, b)


def kernel(x, p, positions, ln_g, ln_b, w_ffn1_gu, w_ffn1_down, w_in, ret_gn_g, att_sinks,
           w_ret_out, w_att_out, w_mix_out, w_ffn2_gu, w_ffn2_down, w_ple_gate, w_ple_proj):
    batch, seq, d = x.shape
    depth = ln_g.shape[0]
    t = batch * seq
    alpha = (2.0 * depth) ** 0.25
    h = x.reshape(t, d)
    inner_decay, q_decay, k_decay, chunk_decay = _retention_decays()
    bounds = [0]
    for size in IN_SIZES:
        bounds.append(bounds[-1] + size)
    qk_cols, vg_cols = (bounds[0], bounds[2]), (bounds[2], bounds[4])
    att_cols, gate_cols = (bounds[4], bounds[7]), (bounds[7], bounds[9])
    for i in range(depth):
        g = ln_g[i][:, None, :]
        b = ln_b[i][:, None, :]
        cr, sr, ca, s1, s2, w_qk, w_att, w1_gu, w1_down = _rope_tables(
            positions.reshape(t, 1),
            ((w_in[i], *qk_cols), (w_in[i], *att_cols), w_ffn1_gu[i], w_ffn1_down[i]))

        h = _ffn_ln(h, w1_gu, w1_down, g[0], b[0], alpha=alpha)

        qkd, w_vg, w_gates = _proj(
            functools.partial(_proj_ret_qk_body, n_sub=2, seg_bounds=(vg_cols, gate_cols)),
            h, w_qk, (cr, sr), (k_decay,), (w_in[i], (vg_cols, gate_cols)),
            n_out=3 * RET_QK, tm=1024, name="proj_ret_qk")
        aqkv, = _proj(functools.partial(_proj_att_body, n_sub=2), h, w_att, (ca, s1, s2),
                      tm=1024, name="proj_att")
        att, vg, w2_gu, w2_down = _swa_and_proj(
            aqkv, att_sinks[i], h, w_vg, batch, seq, (w_ffn2_gu[i], w_ffn2_down[i]))
        ret, gates, w_ro, w_ao, w_mo, w_pg = _retention_and_proj(
            qkd, vg, inner_decay, q_decay, chunk_decay, ret_gn_g[i], h, w_gates, batch, seq,
            (w_ret_out[i], w_att_out[i], w_mix_out[i], w_ple_gate[i]))

        h = _mix_ln(h, ret, att, gates, w_ro, w_ao, w_mo, g[1], b[1], alpha=alpha)
        h = _ffn_ln(h, w2_gu, w2_down, g[2], b[2], alpha=alpha)
        h = _ple_ln(h, p[i].reshape(t, PLE_DIM), w_pg, w_ple_proj[i].astype(BF16), g[3], b[3],
                    alpha=alpha)
    return h.reshape(batch, seq, d)
```
